```python
import math
import jax
import jax.numpy as jnp
from jax import lax
import numpy as np

D_MODEL = 4096
BATCH = 8
SEQ = 2048
DEPTH = 4

GRID_W = 64
CTX_LEN = 256
N_MIXERS = 4
ROPE_BASE = 10000.0
NORM_EPS = 1e-6
MOD_RANK = 512

SWA_HEADS = 32
SWA_KV_HEADS = 8
SWA_HEAD_DIM = 128
SWA_WINDOW = 128
SWA_BLOCK = 128

SSD_D_INNER = 2 * D_MODEL
SSD_HEAD_DIM = 64
SSD_HEADS = SSD_D_INNER // SSD_HEAD_DIM
SSD_GROUPS = 8
SSD_STATE = 128
SSD_CONV = 3
SSD_CHUNK = 128

S5_WIDTH = D_MODEL
S5_GROUP = 16
S5_NGROUPS = S5_WIDTH // S5_GROUP
S5_STATE = 64
S5_CHUNK = 128

MLA_HEADS = 32
MLA_Q_RANK = 1024
MLA_KV_RANK = 512
MLA_NOPE = 128
MLA_ROPE = 64
MLA_V = 128
MLA_BLOCK = 128

MOE_GROUPS = 4
MOE_PER_GROUP = 8
MOE_EXPERTS = MOE_GROUPS * MOE_PER_GROUP
MOE_TOPK = 2
MOE_HIDDEN = 256
MOE_BLOCK = 128

kernel_name = 'hybrid_interleaved_diffusion_trunk'


def rmsnorm(x, g):
    xf = x.astype(jnp.float32)
    xf = xf * lax.rsqrt(jnp.mean(xf * xf, axis=-1, keepdims=True) + NORM_EPS)
    return (xf * g.astype(jnp.float32)).astype(x.dtype)


def adaln(cvec, w_down, w_up, b):
    m = (jax.nn.silu(cvec) @ w_down) @ w_up + b
    return m.reshape(cvec.shape[:-1] + (6, D_MODEL))


def modulate(t, g, shift, scale):
    return rmsnorm(t, g) * (1.0 + scale) + shift


def rope_tables(n_tokens, rot_dim):
    rows = n_tokens // GRID_W
    row = jnp.repeat(jnp.arange(rows, dtype=jnp.float32), GRID_W)
    col = jnp.tile(jnp.arange(GRID_W, dtype=jnp.float32), rows)
    axis_dim = rot_dim // 2
    inv_freq = ROPE_BASE ** (-jnp.arange(0, axis_dim, 2, dtype=jnp.float32) / axis_dim)
    ang = jnp.stack([row[:, None] * inv_freq, col[:, None] * inv_freq], 0)
    return jnp.cos(ang), jnp.sin(ang)


def apply_rope_2d(x, tables):
    cos, sin = tables
    extra = x.ndim - 3
    axis_dim = x.shape[-1] // 2
    half = axis_dim // 2
    outs = []
    for a in range(2):
        xa = x[..., a * axis_dim:(a + 1) * axis_dim]
        x1, x2 = xa[..., :half], xa[..., half:]
        cs = cos[a].reshape((cos.shape[1],) + (1,) * extra + (half,)).astype(x.dtype)
        sn = sin[a].reshape((sin.shape[1],) + (1,) * extra + (half,)).astype(x.dtype)
        outs += [x1 * cs - x2 * sn, x2 * cs + x1 * sn]
    return jnp.concatenate(outs, -1)


def sink_attend(s, vals, sink):
    sk = sink[None, :, :, None]
    m = jnp.maximum(jnp.max(s, axis=-1), sk)
    p = jnp.exp(s - m[..., None])
    p = p / (jnp.sum(p, axis=-1, keepdims=True) + jnp.exp(sk - m)[..., None])
    return jnp.einsum('bkgqs,bskd->bqkgd', p.astype(vals.dtype), vals)


def mixer_swa(h, hc, w_in, q_g, k_g, sinks, w_out, ctx_out):
    n_b, n_s, _ = h.shape
    grp = SWA_HEADS // SWA_KV_HEADS
    nq = SWA_HEADS * SWA_HEAD_DIM
    nkv = SWA_KV_HEADS * SWA_HEAD_DIM
    scale = SWA_HEAD_DIM ** -0.5

    def project(t):
        n_l = t.shape[1]
        p = t @ w_in
        q = rmsnorm(p[..., :nq].reshape(n_b, n_l, SWA_KV_HEADS, grp, SWA_HEAD_DIM), q_g)
        k = rmsnorm(p[..., nq:nq + nkv].reshape(n_b, n_l, SWA_KV_HEADS, SWA_HEAD_DIM), k_g)
        v = p[..., nq + nkv:].reshape(n_b, n_l, SWA_KV_HEADS, SWA_HEAD_DIM)
        return q * scale, k, v

    q, k, v = project(h)
    qc, kc, vc = project(hc)
    rope = rope_tables(n_s, SWA_HEAD_DIM)
    q = apply_rope_2d(q, rope)
    k = apply_rope_2d(k, rope)
    sink = sinks.reshape(SWA_KV_HEADS, grp).astype(jnp.float32)

    n_blk = n_s // SWA_BLOCK
    pad = ((0, 0), (SWA_BLOCK, SWA_BLOCK), (0, 0), (0, 0))
    kpad, vpad = jnp.pad(k, pad), jnp.pad(v, pad)
    qb = q.reshape(n_b, n_blk, SWA_BLOCK, SWA_KV_HEADS, grp, SWA_HEAD_DIM).swapaxes(0, 1)

    def block(args):
        j, qj = args
        start = j * SWA_BLOCK
        kb = lax.dynamic_slice_in_dim(kpad, start, 3 * SWA_BLOCK, axis=1)
        vb = lax.dynamic_slice_in_dim(vpad, start, 3 * SWA_BLOCK, axis=1)
        qpos = start + jnp.arange(SWA_BLOCK)
        kpos = start - SWA_BLOCK + jnp.arange(3 * SWA_BLOCK)
        valid = (jnp.abs(kpos[None, :] - qpos[:, None]) <= SWA_WINDOW) & ((kpos >= 0) & (kpos < n_s))[None, :]
        s_lat = jnp.einsum('bqkgd,bskd->bkgqs', qj, kb, preferred_element_type=jnp.float32)
        s_lat = jnp.where(valid, s_lat, -jnp.inf)
        s_ctx = jnp.einsum('bqkgd,bskd->bkgqs', qj, kc, preferred_element_type=jnp.float32)
        return sink_attend(jnp.concatenate([s_lat, s_ctx], -1), jnp.concatenate([vb, vc], 1), sink)

    o = lax.map(block, (jnp.arange(n_blk), qb))
    out = o.swapaxes(0, 1).reshape(n_b, n_s, nq) @ w_out
    if not ctx_out:
        return out, None
    s_cc = jnp.einsum('bqkgd,bskd->bkgqs', qc, kc, preferred_element_type=jnp.float32)
    oc = sink_attend(s_cc, vc, sink).reshape(n_b, hc.shape[1], nq) @ w_out
    return out, oc


def dwconv_centred(t, w, b):
    width = w.shape[0]
    y = lax.conv_general_dilated(t, w[:, None, :].astype(t.dtype), (1,), [(width // 2, width // 2)],
                                 dimension_numbers=('NWC', 'WIO', 'NWC'), feature_group_count=t.shape[-1])
    return y + b


def ssd_scan(xs, dt, a, bm, cm, state0):
    f32 = jnp.float32
    n_b, n_l = xs.shape[:2]
    n_c = n_l // SSD_CHUNK

    def chunks(t):
        t = t.astype(f32)
        return t.reshape((n_b, n_c, SSD_CHUNK) + t.shape[2:]).swapaxes(0, 1)

    lower_tri = jnp.tril(jnp.ones((SSD_CHUNK, SSD_CHUNK), bool))[None, :, :, None, None]

    def step(state, inp):
        xq, dq, bq, cq = inp
        cum = jnp.cumsum(dq * a, axis=1)
        seg = cum[:, :, None] - cum[:, None, :]
        decay = jnp.exp(jnp.where(lower_tri, seg, -jnp.inf))
        xdt = xq * dq[..., None]
        cb = jnp.einsum('btgn,bsgn->btsg', cq, bq)
        y = jnp.einsum('btsg,btsgj,bsgjp->btgjp', cb, decay, xdt)
        y = y + jnp.einsum('btgn,bgjpn->btgjp', cq, state) * jnp.exp(cum)[..., None]
        to_end = jnp.exp(cum[:, -1:] - cum)
        state = state * jnp.exp(cum[:, -1])[..., None, None] + jnp.einsum('bsgj,bsgjp,bsgn->bgjpn', to_end, xdt, bq)
        return state, y

    state, ys = lax.scan(step, state0.astype(f32), (chunks(xs), chunks(dt), chunks(bm), chunks(cm)))
    return ys.swapaxes(0, 1).reshape(xs.shape), state


def mixer_ssd(h, hc, w_in, conv_w, conv_b, dt_bias, a_log, d_skip, norm_g, w_out, ctx_out):
    f32 = jnp.float32
    n_b = h.shape[0]
    hg = SSD_HEADS // SSD_GROUPS
    gn = SSD_GROUPS * SSD_STATE

    def project(t):
        n_l = t.shape[1]
        p = t @ w_in
        z = p[..., :SSD_D_INNER]
        xbc = jax.nn.silu(dwconv_centred(p[..., SSD_D_INNER:2 * SSD_D_INNER + 2 * gn], conv_w, conv_b))
        xs = xbc[..., :SSD_D_INNER].reshape(n_b, n_l, SSD_GROUPS, hg, SSD_HEAD_DIM)
        bm = xbc[..., SSD_D_INNER:SSD_D_INNER + gn].reshape(n_b, n_l, SSD_GROUPS, SSD_STATE)
        cm = xbc[..., SSD_D_INNER + gn:].reshape(n_b, n_l, SSD_GROUPS, SSD_STATE)
        dt = p[..., 2 * SSD_D_INNER + 2 * gn:].reshape(n_b, n_l, 2, SSD_GROUPS, hg)
        return z, xs, bm, cm, dt

    z, xs, bm, cm, dt = project(h)
    zc, xsc, bmc, cmc, dtc = project(hc)
    dsk = d_skip.astype(f32).reshape(SSD_GROUPS, hg)[..., None]
    y = xs.astype(f32) * dsk
    yc = xsc.astype(f32) * dsk
    zero = jnp.zeros((n_b, SSD_GROUPS, hg, SSD_HEAD_DIM, SSD_STATE), f32)

    def flip(t, d):
        return jnp.flip(t, axis=1) if d else t

    for d in range(2):
        a = -jnp.exp(a_log[d].astype(f32)).reshape(SSD_GROUPS, hg)
        bias = dt_bias[d].astype(f32).reshape(SSD_GROUPS, hg)
        dt_l = jax.nn.softplus(dt[:, :, d].astype(f32) + bias)
        dt_c = jax.nn.softplus(dtc[:, :, d].astype(f32) + bias)
        yc_d, state_c = ssd_scan(flip(xsc, d), flip(dt_c, d), a, flip(bmc, d), flip(cmc, d), zero)
        y_d, _ = ssd_scan(flip(xs, d), flip(dt_l, d), a, flip(bm, d), flip(cm, d), state_c)
        y = y + flip(y_d, d)
        if ctx_out:
            yc = yc + flip(yc_d, d)

    def finish(yy, zz):
        n_l = yy.shape[1]
        g = yy.reshape(n_b, n_l, SSD_D_INNER).astype(h.dtype) * jax.nn.silu(zz)
        g = rmsnorm(g.reshape(n_b, n_l, SSD_GROUPS, -1), norm_g.reshape(SSD_GROUPS, -1))
        return g.reshape(n_b, n_l, SSD_D_INNER) @ w_out

    return finish(y, z), (finish(yc, zc) if ctx_out else None)


def complex_affine_combine(e1, e2):
    a1r, a1i, x1r, x1i = e1
    a2r, a2i, x2r, x2i = e2
    return (a2r * a1r - a2i * a1i, a2r * a1i + a2i * a1r,
            a2r * x1r - a2i * x1i + x2r, a2r * x1i + a2i * x1r + x2i)


def s5_scan(u, ar, ai, bbr, bbi, cr, ci, s0r, s0i):
    n_b, n_l = u.shape[:2]
    n_c = n_l // S5_CHUNK
    uch = u.reshape(n_b, n_c, S5_CHUNK, S5_NGROUPS, S5_GROUP).swapaxes(0, 1)
    a_r = jnp.broadcast_to(ar, (n_b, S5_CHUNK, S5_NGROUPS, S5_STATE))
    a_i = jnp.broadcast_to(ai, (n_b, S5_CHUNK, S5_NGROUPS, S5_STATE))

    def step(carry, uq):
        sr, si = carry
        xr = jnp.einsum('bqgc,gpc->bqgp', uq, bbr)
        xi = jnp.einsum('bqgc,gpc->bqgp', uq, bbi)
        pr, pi, lr, li = lax.associative_scan(complex_affine_combine, (a_r, a_i, xr, xi), axis=1)
        st_r = pr * sr[:, None] - pi * si[:, None] + lr
        st_i = pr * si[:, None] + pi * sr[:, None] + li
        yq = jnp.einsum('gcp,bqgp->bqgc', cr, st_r) - jnp.einsum('gcp,bqgp->bqgc', ci, st_i)
        return (st_r[:, -1], st_i[:, -1]), yq

    (sr, si), ys = lax.scan(step, (s0r, s0i), uch)
    return ys.swapaxes(0, 1).reshape(u.shape), sr, si


def mixer_s5(h, hc, w_in, lam_re, lam_im, log_dt, b_re, b_im, c_re, c_im, d_skip, w_glu, ctx_out):
    f32 = jnp.float32
    n_b = h.shape[0]
    u = (h @ w_in).astype(f32)
    uc = (hc @ w_in).astype(f32)
    ug = u.reshape(u.shape[:2] + (S5_NGROUPS, S5_GROUP))
    ucg = uc.reshape(uc.shape[:2] + (S5_NGROUPS, S5_GROUP))
    br, bi, cr, ci = (t.astype(f32) for t in (b_re, b_im, c_re, c_im))
    dsk = d_skip.astype(f32)
    y = dsk * u
    yc = dsk * uc
    zero = jnp.zeros((n_b, S5_NGROUPS, S5_STATE), f32)

    def flip(t, d):
        return jnp.flip(t, axis=1) if d else t

    for d in range(2):
        lr = lam_re[d].astype(f32)
        li = lam_im[d].astype(f32)
        step = jnp.exp(log_dt[d].astype(f32))[:, None]
        mag = jnp.exp(lr * step)
        ar, ai = mag * jnp.cos(li * step), mag * jnp.sin(li * step)
        den = lr * lr + li * li
        fr = ((ar - 1.0) * lr + ai * li) / den
        fi = (ai * lr - (ar - 1.0) * li) / den
        bbr = fr[..., None] * br - fi[..., None] * bi
        bbi = fr[..., None] * bi + fi[..., None] * br
        yc_d, sr, si = s5_scan(flip(ucg, d), ar, ai, bbr, bbi, cr, ci, zero, zero)
        y_d, _, _ = s5_scan(flip(ug, d), ar, ai, bbr, bbi, cr, ci, sr, si)
        y = y + flip(y_d, d).reshape(u.shape)
        if ctx_out:
            yc = yc + flip(yc_d, d).reshape(uc.shape)

    def glu(t):
        g = jax.nn.gelu(t).astype(h.dtype) @ w_glu
        return g[..., :D_MODEL] * jax.nn.sigmoid(g[..., D_MODEL:])

    return glu(y), (glu(yc) if ctx_out else None)


def mixer_mla(h, hc, w_in, q_a_g, kv_a_g, w_uq, w_ukv, q_g, k_g, w_out, ctx_out):
    n_b, n_s, _ = h.shape
    dk = MLA_NOPE + MLA_ROPE
    scale = dk ** -0.5

    def project(t, rope):
        n_l = t.shape[1]
        p = t @ w_in
        cq = rmsnorm(p[..., :MLA_Q_RANK], q_a_g)
        ckv = rmsnorm(p[..., MLA_Q_RANK:MLA_Q_RANK + MLA_KV_RANK], kv_a_g)
        k_rope = jnp.broadcast_to(p[..., None, MLA_Q_RANK + MLA_KV_RANK:], (n_b, n_l, MLA_HEADS, MLA_ROPE))
        q = rmsnorm((cq @ w_uq).reshape(n_b, n_l, MLA_HEADS, dk), q_g)
        kv = (ckv @ w_ukv).reshape(n_b, n_l, MLA_HEADS, MLA_NOPE + MLA_V)
        k = rmsnorm(jnp.concatenate([kv[..., :MLA_NOPE], k_rope], -1), k_g)
        v = kv[..., MLA_NOPE:]
        if rope is not None:
            q = jnp.concatenate([q[..., :MLA_NOPE], apply_rope_2d(q[..., MLA_NOPE:], rope)], -1)
            k = jnp.concatenate([k[..., :MLA_NOPE], apply_rope_2d(k[..., MLA_NOPE:], rope)], -1)
        return q * scale, k, v

    q, k, v = project(h, rope_tables(n_s, MLA_ROPE))
    qc, kc, vc = project(hc, None)
    k_all = jnp.concatenate([k, kc], 1)
    v_all = jnp.concatenate([v, vc], 1)
    n_blk = n_s // MLA_BLOCK
    qb = q.reshape(n_b, n_blk, MLA_BLOCK, MLA_HEADS, dk).swapaxes(0, 1)

    def block(qj):
        s = jnp.einsum('bqhd,bkhd->bhqk', qj, k_all, preferred_element_type=jnp.float32)
        p = jax.nn.softmax(s, axis=-1)
        return jnp.einsum('bhqk,bkhd->bqhd', p.astype(v_all.dtype), v_all)

    o = lax.map(block, qb).swapaxes(0, 1).reshape(n_b, n_s, MLA_HEADS * MLA_V)
    out = o @ w_out
    if not ctx_out:
        return out, None
    s_cc = jnp.einsum('bqhd,bkhd->bhqk', qc, kc, preferred_element_type=jnp.float32)
    oc = jnp.einsum('bhqk,bkhd->bqhd', jax.nn.softmax(s_cc, axis=-1).astype(vc.dtype), vc)
    return out, oc.reshape(n_b, hc.shape[1], MLA_HEADS * MLA_V) @ w_out


def grouped_expert_ffn(h, expert, gate, w1, w3, w2):
    n_tok = h.shape[0]
    n_asg = n_tok * MOE_TOPK
    flat = expert.reshape(-1)
    order = jnp.argsort(flat)
    sorted_e = flat[order]
    counts = jnp.bincount(flat, length=MOE_EXPERTS)
    padded = (counts + MOE_BLOCK - 1) // MOE_BLOCK * MOE_BLOCK
    pad_end = jnp.cumsum(padded)
    pad_start = pad_end - padded
    start = jnp.cumsum(counts) - counts
    dest = pad_start[sorted_e] + jnp.arange(n_asg) - start[sorted_e]
    n_blk = -(-n_asg // MOE_BLOCK) + MOE_EXPERTS
    tok_sorted = (order // MOE_TOPK).astype(jnp.int32)
    src = jnp.full((n_blk * MOE_BLOCK,), n_tok, jnp.int32).at[dest].set(tok_sorted)
    blk_e = jnp.minimum(jnp.searchsorted(pad_end, jnp.arange(n_blk) * MOE_BLOCK, side='right'), MOE_EXPERTS - 1)
    hp = jnp.concatenate([h, jnp.zeros((1, h.shape[1]), h.dtype)], 0)
    xb = hp[src].reshape(n_blk, MOE_BLOCK, h.shape[1])

    def expert_block(args):
        xe, e = args
        return (jax.nn.silu(xe @ w1[e]) * (xe @ w3[e])) @ w2[e]

    yb = lax.map(expert_block, (xb, blk_e)).reshape(n_blk * MOE_BLOCK, h.shape[1])
    contrib = yb[dest].astype(jnp.float32) * gate.reshape(-1)[order][:, None]
    return jax.ops.segment_sum(contrib, tok_sorted, num_segments=n_tok).astype(h.dtype)


def hier_moe(h, w_group, b_group, w_expert, b_expert, w1, w3, w2):
    n_tok = h.shape[0]
    rows = jnp.arange(n_tok)
    lg = jnp.dot(h, w_group, preferred_element_type=jnp.float32) + b_group.astype(jnp.float32)
    grp = jnp.argmax(lg, axis=-1)
    p_grp = jax.nn.softmax(lg, axis=-1)[rows, grp][:, None]
    le = jnp.dot(h, w_expert, preferred_element_type=jnp.float32) + b_expert.astype(jnp.float32)
    le = le.reshape(n_tok, MOE_GROUPS, MOE_PER_GROUP)[rows, grp]
    top_p, top_i = lax.top_k(jax.nn.softmax(le, axis=-1), MOE_TOPK)
    gate = p_grp * top_p / jnp.sum(top_p, axis=-1, keepdims=True)
    expert = grp[:, None].astype(jnp.int32) * MOE_PER_GROUP + top_i.astype(jnp.int32)
    return grouped_expert_ffn(h, expert, gate, w1, w3, w2)


def setup_inputs(seed: int = 0) -> dict:
    key = jax.random.key(seed)
    keys = iter(jax.random.split(key, 96))

    def nrm(shape, std):
        return std * jax.random.normal(next(keys), shape, jnp.float32)

    def gain(shape):
        return 1.0 + nrm(shape, 0.02)

    def unif(shape, lo, hi):
        return jax.random.uniform(next(keys), shape, jnp.float32, lo, hi)

    n_a, n_b, n_c, n_d = [len(range(m, DEPTH, N_MIXERS)) for m in range(N_MIXERS)]
    dm = D_MODEL
    swa_cols = (SWA_HEADS + 2 * SWA_KV_HEADS) * SWA_HEAD_DIM
    gn = SSD_GROUPS * SSD_STATE
    ssd_cols = 2 * SSD_D_INNER + 2 * gn + 2 * SSD_HEADS
    dt0 = jnp.exp(unif((n_b, 2, SSD_HEADS), math.log(1e-3), math.log(1e-1)))
    mla_dk = MLA_NOPE + MLA_ROPE
    s5_shape = (n_c, 2, S5_NGROUPS, S5_STATE)
    return {
        'x': nrm((BATCH, SEQ, dm), 1.0),
        'c': nrm((BATCH, dm), 1.0),
        'ctx': nrm((BATCH, CTX_LEN, dm), 1.0),
        'c_ctx': nrm((dm,), 1.0),
        'mod_down': nrm((DEPTH, dm, MOD_RANK), dm ** -0.5),
        'mod_up': nrm((DEPTH, MOD_RANK, 6 * dm), 0.5 * MOD_RANK ** -0.5),
        'mod_b': nrm((DEPTH, 6 * dm), 0.02),
        'norm1_g': gain((DEPTH, dm)),
        'norm2_g': gain((DEPTH, dm)),
        'swa_w_in': nrm((n_a, dm, swa_cols), dm ** -0.5),
        'swa_q_g': gain((n_a, SWA_HEAD_DIM)),
        'swa_k_g': gain((n_a, SWA_HEAD_DIM)),
        'swa_sinks': nrm((n_a, SWA_HEADS), 1.0),
        'swa_w_out': nrm((n_a, SWA_HEADS * SWA_HEAD_DIM, dm), (SWA_HEADS * SWA_HEAD_DIM) ** -0.5),
        'ssd_w_in': nrm((n_b, dm, ssd_cols), dm ** -0.5),
        'ssd_conv_w': nrm((n_b, SSD_CONV, SSD_D_INNER + 2 * gn), SSD_CONV ** -0.5),
        'ssd_conv_b': nrm((n_b, SSD_D_INNER + 2 * gn), 0.02),
        'ssd_dt_bias': dt0 + jnp.log(-jnp.expm1(-dt0)),
        'ssd_a_log': jnp.log(unif((n_b, 2, SSD_HEADS), 1.0, 16.0)),
        'ssd_d': 1.0 + nrm((n_b, SSD_HEADS), 0.1),
        'ssd_norm_g': gain((n_b, SSD_D_INNER)),
        'ssd_w_out': nrm((n_b, SSD_D_INNER, dm), SSD_D_INNER ** -0.5),
        's5_w_in': nrm((n_c, dm, S5_WIDTH), dm ** -0.5),
        's5_lam_re': -0.5 + nrm(s5_shape, 0.01),
        's5_lam_im': math.pi * jnp.arange(S5_STATE, dtype=jnp.float32) + nrm(s5_shape, 0.01),
        's5_log_dt': unif((n_c, 2, S5_NGROUPS), math.log(1e-3), math.log(1e-1)),
        's5_b_re': nrm((n_c, S5_NGROUPS, S5_STATE, S5_GROUP), (2 * S5_GROUP) ** -0.5),
        's5_b_im': nrm((n_c, S5_NGROUPS, S5_STATE, S5_GROUP), (2 * S5_GROUP) ** -0.5),
        's5_c_re': nrm((n_c, S5_NGROUPS, S5_GROUP, S5_STATE), S5_STATE ** -0.5),
        's5_c_im': nrm((n_c, S5_NGROUPS, S5_GROUP, S5_STATE), S5_STATE ** -0.5),
        's5_d': nrm((n_c, S5_WIDTH), 1.0),
        's5_w_glu': nrm((n_c, S5_WIDTH, 2 * dm), S5_WIDTH ** -0.5),
        'mla_w_in': nrm((n_d, dm, MLA_Q_RANK + MLA_KV_RANK + MLA_ROPE), dm ** -0.5),
        'mla_q_a_g': gain((n_d, MLA_Q_RANK)),
        'mla_kv_a_g': gain((n_d, MLA_KV_RANK)),
        'mla_w_uq': nrm((n_d, MLA_Q_RANK, MLA_HEADS * mla_dk), MLA_Q_RANK ** -0.5),
        'mla_w_ukv': nrm((n_d, MLA_KV_RANK, MLA_HEADS * (MLA_NOPE + MLA_V)), MLA_KV_RANK ** -0.5),
        'mla_q_g': gain((n_d, mla_dk)),
        'mla_k_g': gain((n_d, mla_dk)),
        'mla_w_out': nrm((n_d, MLA_HEADS * MLA_V, dm), (MLA_HEADS * MLA_V) ** -0.5),
        'moe_w_group': nrm((DEPTH, dm, MOE_GROUPS), dm ** -0.5),
        'moe_b_group': nrm((DEPTH, MOE_GROUPS), 0.01),
        'moe_w_expert': nrm((DEPTH, dm, MOE_EXPERTS), dm ** -0.5),
        'moe_b_expert': nrm((DEPTH, MOE_EXPERTS), 0.01),
        'moe_w1': nrm((DEPTH, MOE_EXPERTS, dm, MOE_HIDDEN), dm ** -0.5),
        'moe_w3': nrm((DEPTH, MOE_EXPERTS, dm, MOE_HIDDEN), dm ** -0.5),
        'moe_w2': nrm((DEPTH, MOE_EXPERTS, MOE_HIDDEN, dm), MOE_HIDDEN ** -0.5),
    }


def reference(x, c, ctx, c_ctx, mod_down, mod_up, mod_b, norm1_g, norm2_g,
              swa_w_in, swa_q_g, swa_k_g, swa_sinks, swa_w_out,
              ssd_w_in, ssd_conv_w, ssd_conv_b, ssd_dt_bias, ssd_a_log, ssd_d, ssd_norm_g, ssd_w_out,
              s5_w_in, s5_lam_re, s5_lam_im, s5_log_dt, s5_b_re, s5_b_im, s5_c_re, s5_c_im, s5_d, s5_w_glu,
              mla_w_in, mla_q_a_g, mla_kv_a_g, mla_w_uq, mla_w_ukv, mla_q_g, mla_k_g, mla_w_out,
              moe_w_group, moe_b_group, moe_w_expert, moe_b_expert, moe_w1, moe_w3, moe_w2):
    n_s = x.shape[1]
    xc = ctx
    for i in range(DEPTH):
        kind, slot = i % N_MIXERS, i // N_MIXERS
        last = i == DEPTH - 1
        ml = adaln(c, mod_down[i], mod_up[i], mod_b[i])
        mc = adaln(c_ctx, mod_down[i], mod_up[i], mod_b[i])
        h = modulate(x, norm1_g[i], ml[:, 0, None], ml[:, 1, None])
        hc = modulate(xc, norm1_g[i], mc[0], mc[1])
        if kind == 0:
            o, oc = mixer_swa(h, hc, swa_w_in[slot], swa_q_g[slot], swa_k_g[slot], swa_sinks[slot],
                              swa_w_out[slot], not last)
        elif kind == 1:
            o, oc = mixer_ssd(h, hc, ssd_w_in[slot], ssd_conv_w[slot], ssd_conv_b[slot], ssd_dt_bias[slot],
                              ssd_a_log[slot], ssd_d[slot], ssd_norm_g[slot], ssd_w_out[slot], not last)
        elif kind == 2:
            o, oc = mixer_s5(h, hc, s5_w_in[slot], s5_lam_re[slot], s5_lam_im[slot], s5_log_dt[slot],
                             s5_b_re[slot], s5_b_im[slot], s5_c_re[slot], s5_c_im[slot], s5_d[slot],
                             s5_w_glu[slot], not last)
        else:
            o, oc = mixer_mla(h, hc, mla_w_in[slot], mla_q_a_g[slot], mla_kv_a_g[slot], mla_w_uq[slot],
                              mla_w_ukv[slot], mla_q_g[slot], mla_k_g[slot], mla_w_out[slot], not last)
        x = x + ml[:, 2, None] * o
        h = modulate(x, norm2_g[i], ml[:, 3, None], ml[:, 4, None])
        moe_p = (moe_w_group[i], moe_b_group[i], moe_w_expert[i], moe_b_expert[i], moe_w1[i], moe_w3[i], moe_w2[i])
        if last:
            y = hier_moe(h.reshape(-1, D_MODEL), *moe_p).reshape(h.shape)
        else:
            xc = xc + mc[2] * oc
            hc = modulate(xc, norm2_g[i], mc[3], mc[4])
            tok = jnp.concatenate([h, hc], 1)
            yt = hier_moe(tok.reshape(-1, D_MODEL), *moe_p).reshape(tok.shape)
            y = yt[:, :n_s]
            xc = xc + mc[5] * yt[:, n_s:]
        x = x + ml[:, 5, None] * y
    return x
```

```python
import functools
import math

import jax
import jax.numpy as jnp
from jax import lax
from jax.experimental import pallas as pl
from jax.experimental.pallas import tpu as pltpu

F32 = jnp.float32
BF16 = jnp.bfloat16
HIGHEST = lax.Precision.HIGHEST

GRID_W = 64
ROPE_BASE = 10000.0
NORM_EPS = 1e-6

SWA_HEADS = 32
SWA_KV_HEADS = 8
SWA_HEAD_DIM = 128
SWA_WINDOW = 128
SWA_BLOCK = 128

SSD_HEAD_DIM = 64
SSD_GROUPS = 8
SSD_STATE = 128
SSD_CHUNK = 128

S5_GROUP = 16
S5_STATE = 64
S5_CHUNK = 128
S5_SLAB_GROUPS = 16

MLA_HEADS = 32
MLA_Q_RANK = 1024
MLA_KV_RANK = 512
MLA_NOPE = 128
MLA_ROPE = 64
MLA_V = 128

MOE_GROUPS = 4
MOE_PER_GROUP = 8
MOE_TOPK = 2
MOE_BLOCK = 256

LANES = 128
SUBLANES = 8
VMEM_LIMIT = 56 * 1024 * 1024


def _cparams(sem, vmem=VMEM_LIMIT):
    return pltpu.CompilerParams(dimension_semantics=sem, vmem_limit_bytes=vmem)


def _pick(n, cands):
    for c in cands:
        if n % c == 0:
            return c
    raise ValueError(f"no tile in {cands} divides {n}")


def _sigmoid(x):
    return 1.0 / (1.0 + jnp.exp(-x))


def _silu(x):
    return x * _sigmoid(x)


def _softplus(x):
    return jnp.maximum(x, 0.0) + jnp.log1p(jnp.exp(-jnp.abs(x)))


def _nt_dot(a, b):
    return lax.dot_general(a, b, (((1,), (1,)), ((), ())), preferred_element_type=F32)


def _mm_body(*refs, n_w, nk, epilogue, tm, tiles_per_batch, n_lat, gate_idx):
    a_ref = refs[0]
    w_refs = refs[1:1 + n_w]
    pos = 1 + n_w
    res_ref = mod_ref = None
    if epilogue in ("residual", "glu_residual"):
        res_ref, mod_ref = refs[pos], refs[pos + 1]
        pos += 2
    o_ref = refs[pos]
    acc_refs = refs[pos + 1:]

    def finish(vals):
        val = vals[0] * _sigmoid(vals[1]) if epilogue in ("glu", "glu_residual") else vals[0]
        if res_ref is not None:
            row = (pl.program_id(0) % tiles_per_batch) * tm + lax.broadcasted_iota(jnp.int32, (tm, 1), 0)
            gate = jnp.where(row >= n_lat, mod_ref[1, pl.ds(gate_idx, 1), :], mod_ref[0, pl.ds(gate_idx, 1), :])
            val = res_ref[...] + gate * val
        o_ref[...] = val.astype(o_ref.dtype)

    a = a_ref[...]
    if nk == 1:
        finish([jnp.dot(a, w[...], preferred_element_type=F32) for w in w_refs])
        return
    k = pl.program_id(2)

    @pl.when(k == 0)
    def _():
        for acc in acc_refs:
            acc[...] = jnp.zeros_like(acc)

    for w, acc in zip(w_refs, acc_refs):
        acc[...] += jnp.dot(a, w[...], preferred_element_type=F32)

    @pl.when(k == nk - 1)
    def _():
        finish([acc[...] for acc in acc_refs])


def _matmul(a, w, *, col_blocks, n_out, tm, tn, tk=None, out_dtype=F32, epilogue="store", res=None, mods=None,
            gate_idx=0, rows_per_batch=None, n_lat=None, out_shape=None, out_map=None, name="matmul"):
    m_rows, k_dim = a.shape
    tk = tk or k_dim
    nk = k_dim // tk
    n_w = len(col_blocks)
    grid = (m_rows // tm, n_out // tn, nk)
    tiles_per_batch = (rows_per_batch // tm) if rows_per_batch else 1
    in_specs = [pl.BlockSpec((tm, tk), lambda i, j, k: (i, k))]
    args = [a]
    for off in col_blocks:
        in_specs.append(pl.BlockSpec((tk, tn), lambda i, j, k, off=off: (k, j + off)))
        args.append(w)
    if res is not None:
        in_specs.append(pl.BlockSpec((tm, tn), lambda i, j, k: (i, j)))
        in_specs.append(pl.BlockSpec((None, 2, 6, tn), lambda i, j, k: (i // tiles_per_batch, 0, 0, j)))
        args += [res, mods]
    out_shape = out_shape or (m_rows, n_out)
    out_map = out_map or (lambda i, j, k: (i, j))
    scratch = [pltpu.VMEM((tm, tn), F32) for _ in range(n_w)] if nk > 1 else []
    body = functools.partial(_mm_body, n_w=n_w, nk=nk, epilogue=epilogue, tm=tm, tiles_per_batch=tiles_per_batch,
                             n_lat=n_lat, gate_idx=gate_idx)
    return pl.pallas_call(
        body, grid=grid, in_specs=in_specs, out_specs=pl.BlockSpec((tm, tn), out_map),
        out_shape=jax.ShapeDtypeStruct(out_shape, out_dtype), scratch_shapes=scratch,
        compiler_params=_cparams(("parallel", "parallel", "arbitrary")), name=name)(*args)


def _adaln_body(c_ref, wd_ref, wu_ref, b_ref, o_ref, t_ref):
    @pl.when(pl.program_id(1) == 0)
    def _():
        cv = c_ref[...]
        t_ref[...] = jnp.dot(_silu(cv), wd_ref[...], precision=HIGHEST, preferred_element_type=F32)

    o_ref[...] = jnp.dot(t_ref[...], wu_ref[...], precision=HIGHEST, preferred_element_type=F32) + b_ref[...]


def _adaln(cvecs, mod_down, mod_up, mod_b):
    depth, d, rank = mod_down.shape
    n6 = mod_up.shape[-1]
    rows = cvecs.shape[0]
    tn = _pick(n6, (2048, 1024, 512, 256, 128))
    return pl.pallas_call(
        _adaln_body, grid=(depth, n6 // tn),
        in_specs=[pl.BlockSpec((rows, d), lambda l, j: (0, 0)),
                  pl.BlockSpec((None, d, rank), lambda l, j: (l, 0, 0)),
                  pl.BlockSpec((None, rank, tn), lambda l, j: (l, 0, j)),
                  pl.BlockSpec((None, 1, tn), lambda l, j: (l, 0, j))],
        out_specs=pl.BlockSpec((None, rows, tn), lambda l, j: (l, 0, j)),
        out_shape=jax.ShapeDtypeStruct((depth, rows, n6), F32),
        scratch_shapes=[pltpu.VMEM((rows, rank), F32)],
        compiler_params=_cparams(("parallel", "arbitrary")), name="adaln")(
            cvecs, mod_down, mod_up, mod_b.reshape(depth, 1, n6))


def _modnorm_body(*refs, shift_idx, scale_idx, router):
    if router:
        x_ref, g_ref, mod_ref, wr_ref, br_ref, h_ref, ids_ref, gates_ref = refs
    else:
        x_ref, g_ref, mod_ref, h_ref = refs
    x = x_ref[...]
    xn = x * lax.rsqrt(jnp.mean(x * x, axis=-1, keepdims=True) + NORM_EPS) * g_ref[...]
    h = xn * (1.0 + mod_ref[pl.ds(scale_idx, 1), :]) + mod_ref[pl.ds(shift_idx, 1), :]
    h_ref[...] = h.astype(h_ref.dtype)
    if not router:
        return
    logits = jnp.dot(h, wr_ref[...], precision=HIGHEST, preferred_element_type=F32) + br_ref[...]
    lane = lax.broadcasted_iota(jnp.int32, logits.shape, 1)
    lane_f = lane.astype(F32)
    neg = -jnp.inf

    def first_lane(hit):
        return jnp.min(jnp.where(hit, lane_f, float(LANES)), axis=-1, keepdims=True).astype(jnp.int32)

    is_grp = lane < MOE_GROUPS
    lg = jnp.where(is_grp, logits, neg)
    mg = jnp.max(lg, axis=-1, keepdims=True)
    grp = first_lane(lg == mg)
    p_grp = 1.0 / jnp.sum(jnp.where(is_grp, jnp.exp(lg - mg), 0.0), axis=-1, keepdims=True)
    lo = MOE_GROUPS + grp * MOE_PER_GROUP
    le = jnp.where((lane >= lo) & (lane < lo + MOE_PER_GROUP), logits, neg)
    m1 = jnp.max(le, axis=-1, keepdims=True)
    i1 = first_lane(le == m1)
    le2 = jnp.where(lane == i1, neg, le)
    m2 = jnp.max(le2, axis=-1, keepdims=True)
    i2 = first_lane(le2 == m2)
    e2 = jnp.exp(m2 - m1)
    g1 = p_grp / (1.0 + e2)
    g2 = p_grp * e2 / (1.0 + e2)
    ids_ref[...] = jnp.where(lane == 0, i1 - MOE_GROUPS, jnp.where(lane == 1, i2 - MOE_GROUPS, 0))
    gates_ref[...] = jnp.where(lane == 0, g1, jnp.where(lane == 1, g2, 0.0))


def _modnorm(x, g, mods2, *, shift_idx, scale_idx, lt, n_lat, out_dtype, router_w=None, router_b=None):
    rows, d = x.shape
    tm = _pick(math.gcd(lt, n_lat), (256, 128))
    tpb = lt // tm
    lat_tiles = n_lat // tm
    router = router_w is not None
    in_specs = [pl.BlockSpec((tm, d), lambda i: (i, 0)),
                pl.BlockSpec((1, d), lambda i: (0, 0)),
                pl.BlockSpec((None, 6, d), lambda i: ((i // tpb) * 2 + ((i % tpb) >= lat_tiles).astype(jnp.int32), 0, 0))]
    args = [x, g.reshape(1, d), mods2]
    out_specs = [pl.BlockSpec((tm, d), lambda i: (i, 0))]
    out_shape = [jax.ShapeDtypeStruct((rows, d), out_dtype)]
    if router:
        in_specs += [pl.BlockSpec((d, LANES), lambda i: (0, 0)), pl.BlockSpec((1, LANES), lambda i: (0, 0))]
        args += [router_w, router_b]
        out_specs += [pl.BlockSpec((tm, LANES), lambda i: (i, 0))] * 2
        out_shape += [jax.ShapeDtypeStruct((rows, LANES), jnp.int32), jax.ShapeDtypeStruct((rows, LANES), F32)]
    body = functools.partial(_modnorm_body, shift_idx=shift_idx, scale_idx=scale_idx, router=router)
    out = pl.pallas_call(body, grid=(rows // tm,), in_specs=in_specs, out_specs=out_specs, out_shape=out_shape,
                         compiler_params=_cparams(("parallel",)), name="modnorm_router" if router else "modnorm")(*args)
    return out if router else out[0]


def _row_gather_start(idx_ref, base, n, src_hbm, dst, sem, dst_base=0, idx_stride=1):
    def body(r, c):
        row = idx_ref[base + r * idx_stride]
        pltpu.make_async_copy(src_hbm.at[pl.ds(row, 1)], dst.at[pl.ds(dst_base + r, 1)], sem).start()
        return c

    lax.fori_loop(0, n, body, 0)


def _row_gather_wait(n, src_hbm, dst, sem):
    def body(r, c):
        pltpu.make_async_copy(src_hbm.at[pl.ds(0, 1)], dst.at[pl.ds(r, 1)], sem).wait()
        return c

    lax.fori_loop(0, n, body, 0)


def _moe_ffn_body(src_ref, blk_e_ref, n_used_ref, h_hbm, w1_ref, w3_ref, w2_ref, o_ref, xbuf, sems, *, blk):
    del blk_e_ref
    i = pl.program_id(0)
    slot = i % 2
    n_used = n_used_ref[0]

    @pl.when(i == 0)
    def _():
        _row_gather_start(src_ref, 0, blk, h_hbm, xbuf.at[0], sems.at[0])

    @pl.when(i + 1 < n_used)
    def _():
        _row_gather_start(src_ref, (i + 1) * blk, blk, h_hbm, xbuf.at[1 - slot], sems.at[1 - slot])

    @pl.when(i < n_used)
    def _():
        _row_gather_wait(blk, h_hbm, xbuf.at[slot], sems.at[slot])
        x = xbuf[slot].astype(BF16)
        a1 = jnp.dot(x, w1_ref[...], preferred_element_type=F32)
        a3 = jnp.dot(x, w3_ref[...], preferred_element_type=F32)
        mid = (_silu(a1) * a3).astype(BF16)
        o_ref[...] = jnp.dot(mid, w2_ref[...], preferred_element_type=F32)

    @pl.when(i >= n_used)
    def _():
        o_ref[...] = jnp.zeros_like(o_ref)


def _moe_ffn(h, src, blk_e, n_used, w1, w3, w2, *, blk):
    n_rows = src.shape[0]
    n_blk = n_rows // blk
    d = h.shape[1]
    hid = w1.shape[-1]
    grid_spec = pltpu.PrefetchScalarGridSpec(
        num_scalar_prefetch=3, grid=(n_blk,),
        in_specs=[pl.BlockSpec(memory_space=pl.ANY),
                  pl.BlockSpec((None, d, hid), lambda i, s, e, n: (e[i], 0, 0)),
                  pl.BlockSpec((None, d, hid), lambda i, s, e, n: (e[i], 0, 0)),
                  pl.BlockSpec((None, hid, d), lambda i, s, e, n: (e[i], 0, 0))],
        out_specs=pl.BlockSpec((blk, d), lambda i, s, e, n: (i, 0)),
        scratch_shapes=[pltpu.VMEM((2, blk, d), h.dtype), pltpu.SemaphoreType.DMA((2,))])
    return pl.pallas_call(
        functools.partial(_moe_ffn_body, blk=blk), grid_spec=grid_spec,
        out_shape=jax.ShapeDtypeStruct((n_rows, d), F32),
        compiler_params=_cparams(("arbitrary",)), name="moe_ffn")(src, blk_e, n_used, h, w1, w3, w2)


def _moe_combine_body(pos_ref, x_ref, gates_ref, mod_ref, yb_hbm, o_ref, ybuf, sems, *, tm, n_tiles):
    i = pl.program_id(0)
    slot = i % 2

    def start(tile, s):
        for kk in range(MOE_TOPK):
            _row_gather_start(pos_ref, tile * tm * MOE_TOPK + kk, tm, yb_hbm, ybuf.at[s], sems.at[s],
                              dst_base=kk * tm, idx_stride=MOE_TOPK)

    @pl.when(i == 0)
    def _():
        start(0, 0)

    @pl.when(i + 1 < n_tiles)
    def _():
        start(i + 1, 1 - slot)

    _row_gather_wait(MOE_TOPK * tm, yb_hbm, ybuf.at[slot], sems.at[slot])
    gates = gates_ref[...]
    y = gates[:, 0:1] * ybuf[slot, pl.ds(0, tm), :] + gates[:, 1:2] * ybuf[slot, pl.ds(tm, tm), :]
    o_ref[...] = x_ref[...] + mod_ref[pl.ds(5, 1), :] * y


def _moe_combine(x, gates, mods2, yb, pos, *, lt, n_lat):
    rows, d = x.shape
    tm = 128
    tpb = lt // tm
    lat_tiles = n_lat // tm
    n_tiles = rows // tm
    grid_spec = pltpu.PrefetchScalarGridSpec(
        num_scalar_prefetch=1, grid=(n_tiles,),
        in_specs=[pl.BlockSpec((tm, d), lambda i, p: (i, 0)),
                  pl.BlockSpec((tm, LANES), lambda i, p: (i, 0)),
                  pl.BlockSpec((None, 6, d),
                               lambda i, p: ((i // tpb) * 2 + ((i % tpb) >= lat_tiles).astype(jnp.int32), 0, 0)),
                  pl.BlockSpec(memory_space=pl.ANY)],
        out_specs=pl.BlockSpec((tm, d), lambda i, p: (i, 0)),
        scratch_shapes=[pltpu.VMEM((2, MOE_TOPK * tm, d), F32), pltpu.SemaphoreType.DMA((2,))])
    return pl.pallas_call(
        functools.partial(_moe_combine_body, tm=tm, n_tiles=n_tiles), grid_spec=grid_spec,
        out_shape=jax.ShapeDtypeStruct((rows, d), F32),
        compiler_params=_cparams(("arbitrary",)), name="moe_combine")(pos, x, gates, mods2, yb)


def _moe_layer(x, mods2, norm_g, router_w, router_b, w1, w3, w2, *, lt, n_lat):
    rows, d = x.shape
    n_exp = w1.shape[0]
    h2, ids, gates = _modnorm(x, norm_g, mods2, shift_idx=3, scale_idx=4, lt=lt, n_lat=n_lat, out_dtype=F32,
                              router_w=router_w, router_b=router_b)
    blk = MOE_BLOCK
    n_asg = rows * MOE_TOPK
    flat = ids[:, :MOE_TOPK].reshape(-1)
    order = jnp.argsort(flat)
    sorted_e = flat[order]
    counts = jnp.bincount(flat, length=n_exp)
    padded = (counts + blk - 1) // blk * blk
    pad_end = jnp.cumsum(padded)
    pad_start = pad_end - padded
    start = jnp.cumsum(counts) - counts
    dest = (pad_start[sorted_e] + jnp.arange(n_asg) - start[sorted_e]).astype(jnp.int32)
    n_blk = n_asg // blk + n_exp
    src = jnp.zeros((n_blk * blk,), jnp.int32).at[dest].set((order // MOE_TOPK).astype(jnp.int32))
    blk_e = jnp.minimum(jnp.searchsorted(pad_end, jnp.arange(n_blk) * blk, side="right"), n_exp - 1).astype(jnp.int32)
    n_used = (pad_end[-1:] // blk).astype(jnp.int32)
    pos = jnp.zeros((n_asg,), jnp.int32).at[order].set(dest)
    yb = _moe_ffn(h2, src, blk_e, n_used, w1, w3, w2, blk=blk)
    return _moe_combine(x, gates, mods2, yb, pos, lt=lt, n_lat=n_lat)


def _rope_tables(n_lat, n_ctx, rot_dim, width):
    rows = n_lat // GRID_W
    row = jnp.repeat(jnp.arange(rows, dtype=F32), GRID_W)
    col = jnp.tile(jnp.arange(GRID_W, dtype=F32), rows)
    axis_dim = rot_dim // 2
    inv_freq = ROPE_BASE ** (-jnp.arange(0, axis_dim, 2, dtype=F32) / axis_dim)
    a0 = row[:, None] * inv_freq
    a1 = col[:, None] * inv_freq
    cos = jnp.concatenate([jnp.cos(a0), jnp.cos(a0), jnp.cos(a1), jnp.cos(a1)], -1)
    sin = jnp.concatenate([-jnp.sin(a0), jnp.sin(a0), -jnp.sin(a1), jnp.sin(a1)], -1)
    cos = jnp.pad(cos, ((0, n_ctx), (0, width - rot_dim)), constant_values=1.0)
    sin = jnp.pad(sin, ((0, n_ctx), (0, width - rot_dim)))
    return cos, sin


def _rope(x, cos, sin, half):
    lane = lax.broadcasted_iota(jnp.int32, x.shape, 1)
    partner = jnp.where((lane % (2 * half)) < half, pltpu.roll(x, LANES - half, 1), pltpu.roll(x, half, 1))
    return x * cos + partner * sin


def _swa_prep_body(p_ref, cos_ref, sin_ref, qg_ref, kg_ref, q_ref, k_ref, v_ref, *, n_q, n_kv):
    hd = SWA_HEAD_DIM
    cos, sin = cos_ref[...], sin_ref[...]
    scale = hd ** -0.5

    def norm_rope(t, g):
        t = t * lax.rsqrt(jnp.mean(t * t, axis=-1, keepdims=True) + NORM_EPS) * g
        return _rope(t, cos, sin, hd // 4)

    for h in range(n_q):
        q_ref[:, h * hd:(h + 1) * hd] = (norm_rope(p_ref[:, h * hd:(h + 1) * hd], qg_ref[...]) * scale).astype(BF16)
    for h in range(n_kv):
        c0 = (n_q + h) * hd
        k_ref[:, h * hd:(h + 1) * hd] = norm_rope(p_ref[:, c0:c0 + hd], kg_ref[...]).astype(BF16)
    v_ref[...] = p_ref[:, (n_q + n_kv) * hd:].astype(BF16)


def _swa_attn_body(sink_ref, q_ref, kp_ref, kc_ref, kn_ref, kx_ref, vp_ref, vc_ref, vn_ref, vx_ref, o_ref, *,
                   n_lat, grp):
    kv = pl.program_id(1)
    j = pl.program_id(2)
    blk = SWA_BLOCK
    hd = SWA_HEAD_DIM
    n_ctx = kx_ref.shape[0]
    lat_blocks = n_lat // blk
    is_lat = j < lat_blocks
    qpos = j * blk + lax.broadcasted_iota(jnp.int32, (blk, 1), 0)
    biases = []
    for w in range(3):
        kpos = (j - 1 + w) * blk + lax.broadcasted_iota(jnp.int32, (1, blk), 1)
        valid = (jnp.abs(kpos - qpos) <= SWA_WINDOW) & (kpos >= 0) & (kpos < n_lat) & is_lat
        biases.append(jnp.where(valid, 0.0, -jnp.inf))
    biases.append(jnp.zeros((blk, n_ctx), F32))
    bias = jnp.concatenate(biases, axis=1)
    k_all = jnp.concatenate([kp_ref[...], kc_ref[...], kn_ref[...], kx_ref[...]], axis=0)
    v_all = jnp.concatenate([vp_ref[...], vc_ref[...], vn_ref[...], vx_ref[...]], axis=0)
    for g in range(grp):
        sink = sink_ref[kv * grp + g]
        s = _nt_dot(q_ref[:, g * hd:(g + 1) * hd], k_all) + bias
        m = jnp.maximum(jnp.max(s, axis=-1, keepdims=True), sink)
        p = jnp.exp(s - m)
        denom = jnp.sum(p, axis=-1, keepdims=True) + jnp.exp(sink - m)
        o = jnp.dot(p.astype(BF16), v_all, preferred_element_type=F32) / denom
        o_ref[:, g * hd:(g + 1) * hd] = o.astype(o_ref.dtype)


def _mixer_swa(h, w_in, q_g, k_g, sinks, *, batch, lt, n_lat):
    rows = h.shape[0]
    hd = SWA_HEAD_DIM
    n_q, n_kv = SWA_HEADS, SWA_KV_HEADS
    grp = n_q // n_kv
    n_ctx = lt - n_lat
    n_cols = (n_q + 2 * n_kv) * hd
    tm = _pick(lt, (768, 512, 384, 256, 128))
    tn = _pick(n_cols, (512, 256, 128))
    p = _matmul(h, w_in, col_blocks=[0], n_out=n_cols, tm=tm, tn=tn, name="swa_in")
    cos, sin = _rope_tables(n_lat, n_ctx, hd, hd)
    tp = _pick(math.gcd(lt, n_lat), (256, 128))
    tpb = lt // tp
    q, k, v = pl.pallas_call(
        functools.partial(_swa_prep_body, n_q=n_q, n_kv=n_kv), grid=(rows // tp,),
        in_specs=[pl.BlockSpec((tp, n_cols), lambda i: (i, 0)),
                  pl.BlockSpec((tp, hd), lambda i: (i % tpb, 0)),
                  pl.BlockSpec((tp, hd), lambda i: (i % tpb, 0)),
                  pl.BlockSpec((1, hd), lambda i: (0, 0)),
                  pl.BlockSpec((1, hd), lambda i: (0, 0))],
        out_specs=[pl.BlockSpec((tp, n_q * hd), lambda i: (i, 0)),
                   pl.BlockSpec((tp, n_kv * hd), lambda i: (i, 0)),
                   pl.BlockSpec((tp, n_kv * hd), lambda i: (i, 0))],
        out_shape=[jax.ShapeDtypeStruct((rows, n_q * hd), BF16),
                   jax.ShapeDtypeStruct((rows, n_kv * hd), BF16),
                   jax.ShapeDtypeStruct((rows, n_kv * hd), BF16)],
        compiler_params=_cparams(("parallel",)), name="swa_prep")(p, cos, sin, q_g.reshape(1, hd), k_g.reshape(1, hd))

    blk = SWA_BLOCK
    bpb = lt // blk
    lat_blocks = n_lat // blk

    def win(off):
        return lambda b, kvh, j: (b * bpb + jnp.clip(j + off, 0, lat_blocks - 1), kvh)

    ctx_spec = pl.BlockSpec((None, n_ctx, hd), lambda b, kvh, j: (b, n_lat // n_ctx, kvh))
    k3 = k.reshape(batch, lt, n_kv * hd)
    v3 = v.reshape(batch, lt, n_kv * hd)
    kv_spec = [pl.BlockSpec((blk, hd), win(-1)), pl.BlockSpec((blk, hd), win(0)), pl.BlockSpec((blk, hd), win(1))]
    return pl.pallas_call(
        functools.partial(_swa_attn_body, n_lat=n_lat, grp=grp), grid=(batch, n_kv, bpb),
        in_specs=[pl.BlockSpec(memory_space=pltpu.SMEM),
                  pl.BlockSpec((blk, grp * hd), lambda b, kvh, j: (b * bpb + j, kvh))]
        + kv_spec + [ctx_spec] + kv_spec + [ctx_spec],
        out_specs=pl.BlockSpec((blk, grp * hd), lambda b, kvh, j: (b * bpb + j, kvh)),
        out_shape=jax.ShapeDtypeStruct((rows, n_q * hd), BF16),
        compiler_params=_cparams(("parallel", "parallel", "arbitrary")), name="swa_attn")(
            sinks.astype(F32), q, k, k, k, k3, v, v, v, v3)


def _conv_silu_body(x_ref, prev_ref, next_ref, w_ref, b_ref, o_ref, *, lat_tiles, n_tiles):
    t = pl.program_id(1)
    x = x_ref[...]
    rows = x.shape[0]
    has_prev = jnp.logical_and(t != 0, t != lat_tiles).astype(F32)
    has_next = jnp.logical_and(t != lat_tiles - 1, t != n_tiles - 1).astype(F32)
    row = lax.broadcasted_iota(jnp.int32, x.shape, 0)
    x_m = jnp.where(row == 0, prev_ref[SUBLANES - 1:SUBLANES, :] * has_prev, pltpu.roll(x, 1, 0))
    x_p = jnp.where(row == rows - 1, next_ref[0:1, :] * has_next, pltpu.roll(x, rows - 1, 0))
    y = w_ref[0:1, :] * x_m + w_ref[1:2, :] * x + w_ref[2:3, :] * x_p + b_ref[...]
    o_ref[...] = _silu(y).astype(o_ref.dtype)


def _conv_silu(xbc, conv_w, conv_b, *, batch, lt, n_lat):
    cols = xbc.shape[-1]
    x3 = xbc.reshape(batch, lt, cols)
    tt = _pick(math.gcd(lt, n_lat), (256, 128))
    tc = _pick(cols, (2048, 1024, 512, 256, 128))
    n_tiles = lt // tt
    sub = tt // SUBLANES
    out = pl.pallas_call(
        functools.partial(_conv_silu_body, lat_tiles=n_lat // tt, n_tiles=n_tiles),
        grid=(batch, n_tiles, cols // tc),
        in_specs=[pl.BlockSpec((None, tt, tc), lambda b, t, j: (b, t, j)),
                  pl.BlockSpec((None, SUBLANES, tc), lambda b, t, j: (b, jnp.maximum(t * sub - 1, 0), j)),
                  pl.BlockSpec((None, SUBLANES, tc), lambda b, t, j: (b, jnp.minimum((t + 1) * sub, lt // SUBLANES - 1), j)),
                  pl.BlockSpec((3, tc), lambda b, t, j: (0, j)),
                  pl.BlockSpec((1, tc), lambda b, t, j: (0, j))],
        out_specs=pl.BlockSpec((None, tt, tc), lambda b, t, j: (b, t, j)),
        out_shape=jax.ShapeDtypeStruct((batch, lt, cols), BF16),
        compiler_params=_cparams(("parallel", "parallel", "parallel")), name="ssd_conv")(
            x3, x3, x3, conv_w, conv_b.reshape(1, cols))
    return out


def _ssd_pass_body(*refs, direction, hg, final):
    if final:
        (x_ref, b_ref, c_ref, dt_ref, dtt_ref, bias_ref, biast_ref, alog_ref, alogt_ref,
         y0_ref, z_ref, ng_ref, o_ref, state_ref, g_ref) = refs
    else:
        (x_ref, b_ref, c_ref, dt_ref, dtt_ref, bias_ref, biast_ref, alog_ref, alogt_ref,
         dsk_ref, o_ref, state_ref) = refs
    t_len = x_ref.shape[0]

    @pl.when(pl.program_id(2) == 0)
    def _():
        state_ref[...] = jnp.zeros_like(state_ref)

    bm = b_ref[...]
    cm = c_ref[...]
    v = _softplus(dt_ref[...] + bias_ref[...])
    dt = v
    v = v * (-jnp.exp(alog_ref[...]))
    v_t = _softplus(dtt_ref[...] + biast_ref[...]) * (-jnp.exp(alogt_ref[...]))
    row = lax.broadcasted_iota(jnp.int32, (t_len, t_len), 0)
    col = lax.broadcasted_iota(jnp.int32, (t_len, t_len), 1)
    tri = (col <= row) if direction == 0 else (col >= row)
    tri_f = tri.astype(F32)
    cum = jnp.dot(tri_f, v, precision=HIGHEST, preferred_element_type=F32)
    cum_t = lax.dot_general(v_t, tri_f, (((1,), (1,)), ((), ())), precision=HIGHEST,
                            preferred_element_type=F32)
    total = jnp.sum(v, axis=0, keepdims=True)
    exp_cum = jnp.exp(cum)
    to_end = jnp.exp(total - cum)
    exp_total = jnp.exp(total)
    cb = _nt_dot(cm, bm)
    b_t = bm.astype(F32).T.astype(BF16)
    left = lax.broadcasted_iota(jnp.int32, (t_len, LANES), 1) < SSD_HEAD_DIM
    left_row = lax.broadcasted_iota(jnp.int32, (1, LANES), 1) < SSD_HEAD_DIM

    def expand(t, h0):
        return jnp.where(left, t[:, h0:h0 + 1], t[:, h0 + 1:h0 + 2])

    for pr in range(hg // 2):
        h0 = 2 * pr
        sl = slice(pr * LANES, (pr + 1) * LANES)
        xp = x_ref[:, sl].astype(F32)
        xdt = xp * expand(dt, h0)
        xdt_b = xdt.astype(BF16)
        ys = []
        for hh in (h0, h0 + 1):
            seg = cum[:, hh:hh + 1] - cum_t[hh:hh + 1, :]
            dec = jnp.exp(jnp.where(tri, seg, -jnp.inf))
            ys.append(jnp.dot((cb * dec).astype(BF16), xdt_b, preferred_element_type=F32))
        y = jnp.where(left, ys[0], ys[1])
        st = state_ref[:, sl]
        y = y + jnp.dot(cm, st.astype(BF16), preferred_element_type=F32) * expand(exp_cum, h0)
        xw = (xdt * expand(to_end, h0)).astype(BF16)
        decay = jnp.where(left_row, exp_total[:, h0:h0 + 1], exp_total[:, h0 + 1:h0 + 2])
        state_ref[:, sl] = st * decay + jnp.dot(b_t, xw, preferred_element_type=F32)
        if final:
            z = z_ref[:, sl]
            g_ref[:, sl] = (y0_ref[:, sl] + y) * _silu(z)
        else:
            o_ref[:, sl] = dsk_ref[:, sl] * xp + y
    if final:
        g = g_ref[...]
        o_ref[...] = (g * lax.rsqrt(jnp.mean(g * g, axis=-1, keepdims=True) + NORM_EPS) * ng_ref[...]).astype(o_ref.dtype)


def _chunk_order(i, direction, lat_chunks, ctx_chunks):
    if direction == 0:
        return jnp.where(i < ctx_chunks, lat_chunks + i, i - ctx_chunks)
    return lat_chunks + ctx_chunks - 1 - i


def _ssd_pass(direction, xbc_s, dt_l, dt_t, bias, bias_t, alog, alog_t, extra, *, batch, lt, n_lat, d_inner, final):
    groups = SSD_GROUPS
    gw = d_inner // groups
    hg = gw // SSD_HEAD_DIM
    n_state = SSD_STATE
    t_len = SSD_CHUNK
    lat_chunks, ctx_chunks = n_lat // t_len, (lt - n_lat) // t_len
    n_chunks = lat_chunks + ctx_chunks
    xb = d_inner // n_state

    def cidx(i):
        return _chunk_order(i, direction, lat_chunks, ctx_chunks)

    in_specs = [
        pl.BlockSpec((None, t_len, gw), lambda b, g, i: (b, cidx(i), g)),
        pl.BlockSpec((None, t_len, n_state), lambda b, g, i: (b, cidx(i), xb + g)),
        pl.BlockSpec((None, t_len, n_state), lambda b, g, i: (b, cidx(i), xb + groups + g)),
        pl.BlockSpec((None, None, None, t_len, hg), lambda b, g, i: (direction, g, b, cidx(i), 0)),
        pl.BlockSpec((None, None, None, hg, t_len), lambda b, g, i: (direction, g, b, 0, cidx(i))),
        pl.BlockSpec((None, None, 1, hg), lambda b, g, i: (direction, g, 0, 0)),
        pl.BlockSpec((None, None, hg, 1), lambda b, g, i: (direction, g, 0, 0)),
        pl.BlockSpec((None, None, 1, hg), lambda b, g, i: (direction, g, 0, 0)),
        pl.BlockSpec((None, None, hg, 1), lambda b, g, i: (direction, g, 0, 0)),
    ]
    args = [xbc_s, xbc_s, xbc_s, dt_l, dt_t, bias, bias_t, alog, alog_t]
    row_spec = pl.BlockSpec((None, t_len, gw), lambda b, g, i: (b, cidx(i), g))
    vec_spec = pl.BlockSpec((1, gw), lambda b, g, i: (0, g))
    scratch = [pltpu.VMEM((n_state, gw), F32)]
    if final:
        y0, z, norm_g = extra
        in_specs += [row_spec, row_spec, vec_spec]
        args += [y0, z, norm_g]
        out_dtype = BF16
        scratch.append(pltpu.VMEM((t_len, gw), F32))
    else:
        (dsk,) = extra
        in_specs += [vec_spec]
        args += [dsk]
        out_dtype = F32
    return pl.pallas_call(
        functools.partial(_ssd_pass_body, direction=direction, hg=hg, final=final),
        grid=(batch, groups, n_chunks), in_specs=in_specs, out_specs=row_spec,
        out_shape=jax.ShapeDtypeStruct((batch, lt, d_inner), out_dtype), scratch_shapes=scratch,
        compiler_params=_cparams(("parallel", "parallel", "arbitrary")), name=f"ssd_pass{direction}")(*args)


def _mixer_ssd(h, w_in, conv_w, conv_b, dt_bias, a_log, d_skip, norm_g, *, batch, lt, n_lat):
    rows = h.shape[0]
    groups = SSD_GROUPS
    n_heads = dt_bias.shape[-1]
    d_inner = n_heads * SSD_HEAD_DIM
    hg = n_heads // groups
    gn = groups * SSD_STATE
    tm = _pick(lt, (768, 512, 384, 256, 128))
    tn = _pick(math.gcd(d_inner, 2 * gn), (512, 256, 128))
    z = _matmul(h, w_in, col_blocks=[0], n_out=d_inner, tm=tm, tn=tn, name="ssd_in_z")
    xbc = _matmul(h, w_in, col_blocks=[d_inner // tn], n_out=d_inner + 2 * gn, tm=tm, tn=tn, name="ssd_in_xbc")
    tdt = _pick(2 * n_heads, (256, 128, 64, 32))
    dt = _matmul(h, w_in, col_blocks=[(2 * d_inner + 2 * gn) // tdt], n_out=2 * n_heads, tm=tm, tn=tdt,
                 name="ssd_in_dt")
    xbc_s = _conv_silu(xbc, conv_w, conv_b, batch=batch, lt=lt, n_lat=n_lat)
    dt5 = dt.reshape(batch, lt, 2, groups, hg)
    dt_l = dt5.transpose(2, 3, 0, 1, 4)
    dt_t = dt5.transpose(2, 3, 0, 4, 1)
    bias = dt_bias.astype(F32).reshape(2, groups, 1, hg)
    bias_t = dt_bias.astype(F32).reshape(2, groups, hg, 1)
    alog = a_log.astype(F32).reshape(2, groups, 1, hg)
    alog_t = a_log.astype(F32).reshape(2, groups, hg, 1)
    dsk = jnp.repeat(d_skip.astype(F32), SSD_HEAD_DIM).reshape(1, d_inner)
    kw = dict(batch=batch, lt=lt, n_lat=n_lat, d_inner=d_inner)
    y0 = _ssd_pass(0, xbc_s, dt_l, dt_t, bias, bias_t, alog, alog_t, (dsk,), final=False, **kw)
    g = _ssd_pass(1, xbc_s, dt_l, dt_t, bias, bias_t, alog, alog_t,
                  (y0, z.reshape(batch, lt, d_inner), norm_g.reshape(1, d_inner)), final=True, **kw)
    return g.reshape(rows, d_inner)


def _s5_scan_body(u_ref, bbd_ref, cbd_ref, a_ref, y_ref, xs_ref, s_ref):
    d = pl.program_id(0)
    t_len, n_b, width = u_ref.shape
    ns = a_ref.shape[-1] // 2

    @pl.when(pl.program_id(2) == 0)
    def _():
        s_ref[...] = jnp.zeros_like(s_ref)

    u = u_ref[...].reshape(t_len * n_b, width).astype(BF16)
    xs_ref[...] = jnp.dot(u, bbd_ref[...], preferred_element_type=F32)
    ar = a_ref[:, :ns]
    ai = a_ref[:, ns:]

    def step(k, carry):
        sr, si = carry
        t = jnp.where(d == 0, k, t_len - 1 - k)
        r0 = pl.multiple_of(t * n_b, n_b)
        xr = xs_ref[pl.ds(r0, n_b), :ns]
        xi = xs_ref[pl.ds(r0, n_b), ns:]
        nr = ar * sr - ai * si + xr
        ni = ar * si + ai * sr + xi
        xs_ref[pl.ds(r0, n_b), :ns] = nr
        xs_ref[pl.ds(r0, n_b), ns:] = ni
        return nr, ni

    sr, si = lax.fori_loop(0, t_len, step, (s_ref[:, :ns], s_ref[:, ns:]))
    s_ref[:, :ns] = sr
    s_ref[:, ns:] = si
    y = jnp.dot(xs_ref[...].astype(BF16), cbd_ref[...], preferred_element_type=F32)
    y_ref[...] = y.reshape(t_len, n_b, width)


def _s5_merge_body(u_ref, y0_ref, y1_ref, dsk_ref, o_ref):
    y = dsk_ref[...] * u_ref[...] + y0_ref[...] + y1_ref[...]
    o_ref[...] = (0.5 * y * (1.0 + jnp.tanh(math.sqrt(2.0 / math.pi) * (y + 0.044715 * (y * y * y))))).astype(o_ref.dtype)


def _s5_params(lam_re, lam_im, log_dt, b_re, b_im, c_re, c_im, n_b):
    n_groups, n_state, gsz = b_re.shape
    sg = S5_SLAB_GROUPS
    n_slab = n_groups // sg
    eye = jnp.eye(sg, dtype=F32)
    bbds, a_s = [], []
    br, bi = b_re.astype(F32), b_im.astype(F32)
    for d in range(2):
        lr = lam_re[d].astype(F32)
        li = lam_im[d].astype(F32)
        step = jnp.exp(log_dt[d].astype(F32))[:, None]
        mag = jnp.exp(lr * step)
        ar, ai = mag * jnp.cos(li * step), mag * jnp.sin(li * step)
        den = lr * lr + li * li
        fr = ((ar - 1.0) * lr + ai * li) / den
        fi = (ai * lr - (ar - 1.0) * li) / den
        bbr = fr[..., None] * br - fi[..., None] * bi
        bbi = fr[..., None] * bi + fi[..., None] * br

        def bdiag_in(m):
            m = m.reshape(n_slab, sg, n_state, gsz)
            return jnp.einsum("sgpc,gh->sgchp", m, eye).reshape(n_slab, sg * gsz, sg * n_state)

        bbds.append(jnp.concatenate([bdiag_in(bbr), bdiag_in(bbi)], -1))
        a_cat = jnp.concatenate([ar.reshape(n_slab, sg * n_state), ai.reshape(n_slab, sg * n_state)], -1)
        a_s.append(jnp.broadcast_to(a_cat[:, None, :], (n_slab, n_b, 2 * sg * n_state)))

    def bdiag_out(m):
        m = m.reshape(n_slab, sg, gsz, n_state)
        return jnp.einsum("sgcp,gh->sgphc", m, eye).reshape(n_slab, sg * n_state, sg * gsz)

    cbd = jnp.concatenate([bdiag_out(c_re.astype(F32)), -bdiag_out(c_im.astype(F32))], 1)
    return jnp.stack(bbds).astype(BF16), cbd.astype(BF16), jnp.stack(a_s)


def _mixer_s5(h, w_in, lam_re, lam_im, log_dt, b_re, b_im, c_re, c_im, d_skip, *, batch, lt, n_lat):
    rows, d = h.shape
    width = w_in.shape[1]
    tm = _pick(lt, (768, 512, 384, 256, 128))
    tn = _pick(width, (512, 256, 128))
    tpb = lt // tm
    ncb = width // tn
    u = _matmul(h, w_in, col_blocks=[0], n_out=width, tm=tm, tn=tn, out_shape=(lt, batch * width),
                out_map=lambda i, j, k: (i % tpb, (i // tpb) * ncb + j), name="s5_in")
    bbd, cbd, a_bc = _s5_params(lam_re, lam_im, log_dt, b_re, b_im, c_re, c_im, batch)
    n_slab, sw = bbd.shape[1], bbd.shape[2]
    ns2 = bbd.shape[3]
    t_len = S5_CHUNK
    lat_chunks, ctx_chunks = n_lat // t_len, (lt - n_lat) // t_len
    n_chunks = lat_chunks + ctx_chunks

    def cidx(dd, i):
        fwd = jnp.where(i < ctx_chunks, lat_chunks + i, i - ctx_chunks)
        return jnp.where(dd == 0, fwd, n_chunks - 1 - i)

    u3 = u.reshape(lt, batch, width)
    ys = pl.pallas_call(
        _s5_scan_body, grid=(2, n_slab, n_chunks),
        in_specs=[pl.BlockSpec((t_len, batch, sw), lambda dd, s, i: (cidx(dd, i), 0, s)),
                  pl.BlockSpec((None, None, sw, ns2), lambda dd, s, i: (dd, s, 0, 0)),
                  pl.BlockSpec((None, ns2, sw), lambda dd, s, i: (s, 0, 0)),
                  pl.BlockSpec((None, None, batch, ns2), lambda dd, s, i: (dd, s, 0, 0))],
        out_specs=pl.BlockSpec((None, t_len, batch, sw), lambda dd, s, i: (dd, cidx(dd, i), 0, s)),
        out_shape=jax.ShapeDtypeStruct((2, lt, batch, width), F32),
        scratch_shapes=[pltpu.VMEM((t_len * batch, ns2), F32), pltpu.VMEM((batch, ns2), F32)],
        compiler_params=_cparams(("parallel", "parallel", "arbitrary")), name="s5_scan")(u3, bbd, cbd, a_bc)
    y2 = ys.reshape(2, lt, batch * width)
    tt = _pick(lt, (256, 128))
    ttb = lt // tt
    tc = _pick(width, (2048, 1024, 512, 256, 128))
    ncc = width // tc
    tm_spec = pl.BlockSpec((tt, tc), lambda b, t, j: (t, b * ncc + j))
    return pl.pallas_call(
        _s5_merge_body, grid=(batch, ttb, ncc),
        in_specs=[tm_spec,
                  pl.BlockSpec((None, tt, tc), lambda b, t, j: (0, t, b * ncc + j)),
                  pl.BlockSpec((None, tt, tc), lambda b, t, j: (1, t, b * ncc + j)),
                  pl.BlockSpec((1, tc), lambda b, t, j: (0, j))],
        out_specs=pl.BlockSpec((tt, tc), lambda b, t, j: (b * ttb + t, j)),
        out_shape=jax.ShapeDtypeStruct((rows, width), BF16),
        compiler_params=_cparams(("parallel", "parallel", "parallel")), name="s5_merge")(
            u, y2, y2, d_skip.astype(F32).reshape(1, width))


def _mla_norm_body(p_ref, qg_ref, kvg_ref, cq_ref, ckv_ref, kr_ref):
    def rms(t, g):
        return t * lax.rsqrt(jnp.mean(t * t, axis=-1, keepdims=True) + NORM_EPS) * g

    cq_ref[...] = rms(p_ref[:, :MLA_Q_RANK], qg_ref[...]).astype(cq_ref.dtype)
    ckv_ref[...] = rms(p_ref[:, MLA_Q_RANK:MLA_Q_RANK + MLA_KV_RANK], kvg_ref[...]).astype(ckv_ref.dtype)
    kr_ref[...] = p_ref[:, MLA_Q_RANK + MLA_KV_RANK:MLA_Q_RANK + MLA_KV_RANK + LANES]


def _mla_prep_body(q_ref, kv_ref, kr_ref, cos_ref, sin_ref, qg_ref, kg_ref, qo_ref, ko_ref, vo_ref, *, heads):
    dk = MLA_NOPE + MLA_ROPE
    hw = 2 * LANES
    scale = dk ** -0.5
    cos, sin = cos_ref[...], sin_ref[...]
    kr = kr_ref[...]
    kr_ss = jnp.sum(kr * kr, axis=-1, keepdims=True)
    qg_n, qg_r = qg_ref[:, :LANES], qg_ref[:, LANES:]
    kg_n, kg_r = kg_ref[:, :LANES], kg_ref[:, LANES:]
    for h in range(heads):
        c0 = h * hw
        qn = q_ref[:, c0:c0 + LANES]
        qr = q_ref[:, c0 + LANES:c0 + hw]
        rinv = lax.rsqrt((jnp.sum(qn * qn, axis=-1, keepdims=True) + jnp.sum(qr * qr, axis=-1, keepdims=True)) / dk
                         + NORM_EPS)
        qo_ref[:, c0:c0 + LANES] = (qn * rinv * qg_n * scale).astype(BF16)
        qo_ref[:, c0 + LANES:c0 + hw] = (_rope(qr * rinv * qg_r, cos, sin, MLA_ROPE // 4) * scale).astype(BF16)
        kn = kv_ref[:, c0:c0 + LANES]
        rinv = lax.rsqrt((jnp.sum(kn * kn, axis=-1, keepdims=True) + kr_ss) / dk + NORM_EPS)
        ko_ref[:, c0:c0 + LANES] = (kn * rinv * kg_n).astype(BF16)
        ko_ref[:, c0 + LANES:c0 + hw] = _rope(kr * rinv * kg_r, cos, sin, MLA_ROPE // 4).astype(BF16)
        vo_ref[:, h * LANES:(h + 1) * LANES] = kv_ref[:, c0 + LANES:c0 + hw].astype(BF16)


def _mla_attn_body(q_ref, k_ref, v_ref, o_ref):
    s = _nt_dot(q_ref[...], k_ref[...])
    m = jnp.max(s, axis=-1, keepdims=True)
    p = jnp.exp(s - m)
    denom = jnp.sum(p, axis=-1, keepdims=True)
    o_ref[...] = (jnp.dot(p.astype(BF16), v_ref[...], preferred_element_type=F32) / denom).astype(o_ref.dtype)


def _mixer_mla(h, w_in, q_a_g, kv_a_g, w_uq, w_ukv, q_g, k_g, *, batch, lt, n_lat):
    rows, d = h.shape
    heads = MLA_HEADS
    dk = MLA_NOPE + MLA_ROPE
    hw = 2 * LANES
    n_ctx = lt - n_lat
    n_in = MLA_Q_RANK + MLA_KV_RANK + MLA_ROPE
    n_in_pad = MLA_Q_RANK + MLA_KV_RANK + 2 * LANES
    w_in_p = jnp.pad(w_in, ((0, 0), (0, n_in_pad - n_in))).astype(BF16)
    w_uq_p = jnp.pad(w_uq.reshape(MLA_Q_RANK, heads, dk), ((0, 0), (0, 0), (0, hw - dk))).reshape(
        MLA_Q_RANK, heads * hw).astype(BF16)
    w_ukv_b = w_ukv.astype(BF16)
    qg_p = jnp.pad(q_g.astype(F32), (0, hw - dk)).reshape(1, hw)
    kg_p = jnp.pad(k_g.astype(F32), (0, hw - dk)).reshape(1, hw)
    tm = _pick(lt, (768, 512, 384, 256, 128))
    p = _matmul(h, w_in_p, col_blocks=[0], n_out=n_in_pad, tm=tm, tn=_pick(n_in_pad, (256, 128)), name="mla_in")
    tp = _pick(rows, (512, 256, 128))
    cq, ckv, kr = pl.pallas_call(
        _mla_norm_body, grid=(rows // tp,),
        in_specs=[pl.BlockSpec((tp, n_in_pad), lambda i: (i, 0)),
                  pl.BlockSpec((1, MLA_Q_RANK), lambda i: (0, 0)),
                  pl.BlockSpec((1, MLA_KV_RANK), lambda i: (0, 0))],
        out_specs=[pl.BlockSpec((tp, MLA_Q_RANK), lambda i: (i, 0)),
                   pl.BlockSpec((tp, MLA_KV_RANK), lambda i: (i, 0)),
                   pl.BlockSpec((tp, LANES), lambda i: (i, 0))],
        out_shape=[jax.ShapeDtypeStruct((rows, MLA_Q_RANK), BF16),
                   jax.ShapeDtypeStruct((rows, MLA_KV_RANK), BF16),
                   jax.ShapeDtypeStruct((rows, LANES), F32)],
        compiler_params=_cparams(("parallel",)), name="mla_norm")(
            p, q_a_g.reshape(1, MLA_Q_RANK), kv_a_g.reshape(1, MLA_KV_RANK))
    tn = _pick(heads * hw, (512, 256))
    q_full = _matmul(cq, w_uq_p, col_blocks=[0], n_out=heads * hw, tm=tm, tn=tn, name="mla_uq")
    kv = _matmul(ckv, w_ukv_b, col_blocks=[0], n_out=heads * hw, tm=tm, tn=tn, name="mla_ukv")
    cos, sin = _rope_tables(n_lat, n_ctx, MLA_ROPE, LANES)
    tr = _pick(math.gcd(lt, n_lat), (256, 128))
    trb = lt // tr
    hb = _pick(heads, (8, 4, 2, 1))
    q_cat, k_cat, v = pl.pallas_call(
        functools.partial(_mla_prep_body, heads=hb), grid=(rows // tr, heads // hb),
        in_specs=[pl.BlockSpec((tr, hb * hw), lambda i, j: (i, j)),
                  pl.BlockSpec((tr, hb * hw), lambda i, j: (i, j)),
                  pl.BlockSpec((tr, LANES), lambda i, j: (i, 0)),
                  pl.BlockSpec((tr, LANES), lambda i, j: (i % trb, 0)),
                  pl.BlockSpec((tr, LANES), lambda i, j: (i % trb, 0)),
                  pl.BlockSpec((1, hw), lambda i, j: (0, 0)),
                  pl.BlockSpec((1, hw), lambda i, j: (0, 0))],
        out_specs=[pl.BlockSpec((tr, hb * hw), lambda i, j: (i, j)),
                   pl.BlockSpec((tr, hb * hw), lambda i, j: (i, j)),
                   pl.BlockSpec((tr, hb * LANES), lambda i, j: (i, j))],
        out_shape=[jax.ShapeDtypeStruct((rows, heads * hw), BF16),
                   jax.ShapeDtypeStruct((rows, heads * hw), BF16),
                   jax.ShapeDtypeStruct((rows, heads * LANES), BF16)],
        compiler_params=_cparams(("parallel", "parallel")), name="mla_prep")(q_full, kv, kr, cos, sin, qg_p, kg_p)
    tq = _pick(math.gcd(lt, n_lat), (512, 256, 128))
    q3 = q_cat.reshape(batch, lt, heads * hw)
    k3 = k_cat.reshape(batch, lt, heads * hw)
    v3 = v.reshape(batch, lt, heads * LANES)
    o = pl.pallas_call(
        _mla_attn_body, grid=(batch, heads, lt // tq),
        in_specs=[pl.BlockSpec((None, tq, hw), lambda b, hh, i: (b, i, hh)),
                  pl.BlockSpec((None, lt, hw), lambda b, hh, i: (b, 0, hh)),
                  pl.BlockSpec((None, lt, LANES), lambda b, hh, i: (b, 0, hh))],
        out_specs=pl.BlockSpec((None, tq, LANES), lambda b, hh, i: (b, i, hh)),
        out_shape=jax.ShapeDtypeStruct((batch, lt, heads * LANES), BF16),
        compiler_params=_cparams(("parallel", "parallel", "arbitrary")), name="mla_attn")(q3, k3, v3)
    return o.reshape(rows, heads * LANES)


def kernel(x, c, ctx, c_ctx, mod_down, mod_up, mod_b, norm1_g, norm2_g, swa_w_in, swa_q_g, swa_k_g, swa_sinks, swa_w_out, ssd_w_in, ssd_conv_w, ssd_conv_b, ssd_dt_bias, ssd_a_log, ssd_d, ssd_norm_g, ssd_w_out, s5_w_in, s5_lam_re, s5_lam_im, s5_log_dt, s5_b_re, s5_b_im, s5_c_re, s5_c_im, s5_d, s5_w_glu, mla_w_in, mla_q_a_g, mla_kv_a_g, mla_w_uq, mla_w_ukv, mla_q_g, mla_k_g, mla_w_out, moe_w_group, moe_b_group, moe_w_expert, moe_b_expert, moe_w1, moe_w3, moe_w2):
    batch, n_lat, d = x.shape
    n_ctx = ctx.shape[1]
    lt = n_lat + n_ctx
    rows = batch * lt
    depth = mod_down.shape[0]
    dims = dict(batch=batch, lt=lt, n_lat=n_lat)
    tm = _pick(lt, (768, 512, 384, 256, 128))
    tn = _pick(d, (512, 256, 128))

    xs = jnp.concatenate([x, ctx], axis=1).reshape(rows, d)

    pad_rows = -(batch + 1) % SUBLANES
    cvecs = jnp.concatenate([c, c_ctx[None], jnp.zeros((pad_rows, d), F32)], axis=0)
    mod_all = _adaln(cvecs, mod_down, mod_up, mod_b).reshape(depth, batch + 1 + pad_rows, 6, d)

    n_moe_logits = MOE_GROUPS + moe_w_expert.shape[-1]
    router_w = jnp.pad(jnp.concatenate([moe_w_group, moe_w_expert], -1), ((0, 0), (0, 0), (0, LANES - n_moe_logits)))
    router_b = jnp.pad(jnp.concatenate([moe_b_group, moe_b_expert], -1), ((0, 0), (0, LANES - n_moe_logits)))

    def out_proj(o, w, res, mods4, name, epilogue="residual", cols=(0,)):
        tk = _pick(o.shape[1], (4096, 2048, 1024, 512, 256, 128))
        return _matmul(o, w.astype(BF16), col_blocks=list(cols), n_out=d, tm=tm, tn=tn, tk=tk, epilogue=epilogue,
                       res=res, mods=mods4, gate_idx=2, rows_per_batch=lt, n_lat=n_lat, name=name)

    for i in range(depth):
        kind, slot = i % 4, i // 4
        ml = mod_all[i, :batch]
        mc = jnp.broadcast_to(mod_all[i, batch][None], (batch, 6, d))
        mods4 = jnp.stack([ml, mc], axis=1)
        mods2 = mods4.reshape(batch * 2, 6, d)
        h = _modnorm(xs, norm1_g[i], mods2, shift_idx=0, scale_idx=1, lt=lt, n_lat=n_lat, out_dtype=BF16)
        if kind == 0:
            o = _mixer_swa(h, swa_w_in[slot].astype(BF16), swa_q_g[slot], swa_k_g[slot], swa_sinks[slot], **dims)
            xs = out_proj(o, swa_w_out[slot], xs, mods4, "swa_out")
        elif kind == 1:
            o = _mixer_ssd(h, ssd_w_in[slot].astype(BF16), ssd_conv_w[slot], ssd_conv_b[slot], ssd_dt_bias[slot],
                           ssd_a_log[slot], ssd_d[slot], ssd_norm_g[slot], **dims)
            xs = out_proj(o, ssd_w_out[slot], xs, mods4, "ssd_out")
        elif kind == 2:
            o = _mixer_s5(h, s5_w_in[slot].astype(BF16), s5_lam_re[slot], s5_lam_im[slot], s5_log_dt[slot],
                          s5_b_re[slot], s5_b_im[slot], s5_c_re[slot], s5_c_im[slot], s5_d[slot], **dims)
            xs = out_proj(o, s5_w_glu[slot], xs, mods4, "s5_glu", epilogue="glu_residual", cols=(0, d // tn))
        else:
            o = _mixer_mla(h, mla_w_in[slot], mla_q_a_g[slot], mla_kv_a_g[slot], mla_w_uq[slot], mla_w_ukv[slot],
                           mla_q_g[slot], mla_k_g[slot], **dims)
            xs = out_proj(o, mla_w_out[slot], xs, mods4, "mla_out")
        xs = _moe_layer(xs, mods2, norm2_g[i], router_w[i], router_b[i].reshape(1, LANES), moe_w1[i].astype(BF16),
                        moe_w3[i].astype(BF16), moe_w2[i].astype(BF16), lt=lt, n_lat=n_lat)
    return xs.reshape(batch, lt, d)[:, :n_lat]
```

```python
import functools
import math

import jax
import jax.numpy as jnp
from jax import lax
from jax.experimental import pallas as pl
from jax.experimental.pallas import tpu as pltpu

F32 = jnp.float32
BF16 = jnp.bfloat16
HIGHEST = lax.Precision.HIGHEST

GRID_W = 64
ROPE_BASE = 10000.0
NORM_EPS = 1e-6

SWA_HEADS = 32
SWA_KV_HEADS = 8
SWA_HEAD_DIM = 128
SWA_WINDOW = 128
SWA_BLOCK = 128

SSD_HEAD_DIM = 64
SSD_GROUPS = 8
SSD_STATE = 128
SSD_CHUNK = 128

S5_GROUP = 16
S5_STATE = 64
S5_CHUNK = 128
S5_SLAB_GROUPS = 16

MLA_HEADS = 32
MLA_Q_RANK = 1024
MLA_KV_RANK = 512
MLA_NOPE = 128
MLA_ROPE = 64
MLA_V = 128

MOE_GROUPS = 4
MOE_PER_GROUP = 8
MOE_TOPK = 2
MOE_BLOCK = 256

LANES = 128
SUBLANES = 8
VMEM_LIMIT = 56 * 1024 * 1024


def _cparams(sem, vmem=VMEM_LIMIT):
    return pltpu.CompilerParams(dimension_semantics=sem, vmem_limit_bytes=vmem)


def _pick(n, cands):
    for c in cands:
        if n % c == 0:
            return c
    raise ValueError(f"no tile in {cands} divides {n}")


def _sigmoid(x):
    return 1.0 / (1.0 + jnp.exp(-x))


def _silu(x):
    return x * _sigmoid(x)


def _softplus(x):
    return jnp.maximum(x, 0.0) + jnp.log1p(jnp.exp(-jnp.abs(x)))


def _nt_dot(a, b):
    return lax.dot_general(a, b, (((1,), (1,)), ((), ())), preferred_element_type=F32)


def _mm_body(*refs, n_w, nk, epilogue, tm, tiles_per_batch, n_lat, gate_idx):
    a_ref = refs[0]
    w_refs = refs[1:1 + n_w]
    pos = 1 + n_w
    res_ref = mod_ref = None
    if epilogue in ("residual", "glu_residual"):
        res_ref, mod_ref = refs[pos], refs[pos + 1]
        pos += 2
    o_ref = refs[pos]
    acc_refs = refs[pos + 1:]

    def finish(vals):
        val = vals[0] * _sigmoid(vals[1]) if epilogue in ("glu", "glu_residual") else vals[0]
        if res_ref is not None:
            row = (pl.program_id(0) % tiles_per_batch) * tm + lax.broadcasted_iota(jnp.int32, (tm, 1), 0)
            gate = jnp.where(row >= n_lat, mod_ref[1, pl.ds(gate_idx, 1), :], mod_ref[0, pl.ds(gate_idx, 1), :])
            val = res_ref[...] + gate * val
        o_ref[...] = val.astype(o_ref.dtype)

    a = a_ref[...]
    if nk == 1:
        finish([jnp.dot(a, w[...], preferred_element_type=F32) for w in w_refs])
        return
    k = pl.program_id(2)

    @pl.when(k == 0)
    def _():
        for acc in acc_refs:
            acc[...] = jnp.zeros_like(acc)

    for w, acc in zip(w_refs, acc_refs):
        acc[...] += jnp.dot(a, w[...], preferred_element_type=F32)

    @pl.when(k == nk - 1)
    def _():
        finish([acc[...] for acc in acc_refs])


def _matmul(a, w, *, batch, col_blocks, n_out, tm, tn, tk=None, out_dtype=F32, epilogue="store", res=None, mods=None,
            gate_idx=0, use_rows=None, n_lat=None, out_shape=None, out_map=None, name="matmul"):
    k_dim = a.shape[1]
    a3 = a.reshape(batch, a.shape[0] // batch, k_dim)
    use_rows = use_rows or a3.shape[1]
    tk = tk or k_dim
    nk = k_dim // tk
    n_w = len(col_blocks)
    tu = use_rows // tm
    grid = (batch * tu, n_out // tn, nk)
    in_specs = [pl.BlockSpec((None, tm, tk), lambda i, j, k: (i // tu, i % tu, k))]
    args = [a3]
    for off in col_blocks:
        in_specs.append(pl.BlockSpec((tk, tn), lambda i, j, k, off=off: (k, j + off)))
        args.append(w)
    if res is not None:
        in_specs.append(pl.BlockSpec((None, tm, tn), lambda i, j, k: (i // tu, i % tu, j)))
        in_specs.append(pl.BlockSpec((None, 2, 6, tn), lambda i, j, k: (i // tu, 0, 0, j)))
        args += [res.reshape(batch, res.shape[0] // batch, res.shape[1]), mods]
    out_shape = out_shape or (batch * use_rows, n_out)
    out_map = out_map or (lambda i, j, k: (i, j))
    scratch = [pltpu.VMEM((tm, tn), F32) for _ in range(n_w)] if nk > 1 else []
    body = functools.partial(_mm_body, n_w=n_w, nk=nk, epilogue=epilogue, tm=tm, tiles_per_batch=tu,
                             n_lat=n_lat, gate_idx=gate_idx)
    return pl.pallas_call(
        body, grid=grid, in_specs=in_specs, out_specs=pl.BlockSpec((tm, tn), out_map),
        out_shape=jax.ShapeDtypeStruct(out_shape, out_dtype), scratch_shapes=scratch,
        compiler_params=_cparams(("parallel", "parallel", "arbitrary")), name=name)(*args)


def _adaln_body(c_ref, wd_ref, wu_ref, b_ref, o_ref, t_ref):
    @pl.when(pl.program_id(1) == 0)
    def _():
        cv = c_ref[...]
        t_ref[...] = jnp.dot(_silu(cv), wd_ref[...], precision=HIGHEST, preferred_element_type=F32)

    o_ref[...] = jnp.dot(t_ref[...], wu_ref[...], precision=HIGHEST, preferred_element_type=F32) + b_ref[...]


def _adaln(cvecs, mod_down, mod_up, mod_b):
    depth, d, rank = mod_down.shape
    n6 = mod_up.shape[-1]
    rows = cvecs.shape[0]
    tn = _pick(n6, (2048, 1024, 512, 256, 128))
    return pl.pallas_call(
        _adaln_body, grid=(depth, n6 // tn),
        in_specs=[pl.BlockSpec((rows, d), lambda l, j: (0, 0)),
                  pl.BlockSpec((None, d, rank), lambda l, j: (l, 0, 0)),
                  pl.BlockSpec((None, rank, tn), lambda l, j: (l, 0, j)),
                  pl.BlockSpec((None, 1, tn), lambda l, j: (l, 0, j))],
        out_specs=pl.BlockSpec((None, rows, tn), lambda l, j: (l, 0, j)),
        out_shape=jax.ShapeDtypeStruct((depth, rows, n6), F32),
        scratch_shapes=[pltpu.VMEM((rows, rank), F32)],
        compiler_params=_cparams(("parallel", "arbitrary")), name="adaln")(
            cvecs, mod_down, mod_up, mod_b.reshape(depth, 1, n6))


def _modnorm_body(*refs, shift_idx, scale_idx, router):
    if router:
        x_ref, g_ref, mod_ref, wr_ref, br_ref, h_ref, ids_ref, gates_ref = refs
    else:
        x_ref, g_ref, mod_ref, h_ref = refs
    x = x_ref[...]
    xn = x * lax.rsqrt(jnp.mean(x * x, axis=-1, keepdims=True) + NORM_EPS) * g_ref[...]
    h = xn * (1.0 + mod_ref[pl.ds(scale_idx, 1), :]) + mod_ref[pl.ds(shift_idx, 1), :]
    if not router:
        h_ref[...] = h.astype(h_ref.dtype)
        return
    tm, d = h.shape
    n_chunks = d // (2 * LANES)

    def bf16_bits(t):
        u = lax.bitcast_convert_type(t, jnp.uint32)
        return u + jnp.uint32(0x7FFF) + ((u >> 16) & jnp.uint32(1))

    packed = (bf16_bits(h[:, :d // 2]) >> 16) | (bf16_bits(h[:, d // 2:]) & jnp.uint32(0xFFFF0000))
    for cc in range(n_chunks):
        h_ref[pl.ds(cc, tm, stride=n_chunks), :] = packed[:, cc * LANES:(cc + 1) * LANES]
    logits = jnp.dot(h, wr_ref[...], precision=HIGHEST, preferred_element_type=F32) + br_ref[...]
    lane = lax.broadcasted_iota(jnp.int32, logits.shape, 1)
    lane_f = lane.astype(F32)
    neg = -jnp.inf

    def first_lane(hit):
        return jnp.min(jnp.where(hit, lane_f, float(LANES)), axis=-1, keepdims=True).astype(jnp.int32)

    is_grp = lane < MOE_GROUPS
    lg = jnp.where(is_grp, logits, neg)
    mg = jnp.max(lg, axis=-1, keepdims=True)
    grp = first_lane(lg == mg)
    p_grp = 1.0 / jnp.sum(jnp.where(is_grp, jnp.exp(lg - mg), 0.0), axis=-1, keepdims=True)
    lo = MOE_GROUPS + grp * MOE_PER_GROUP
    le = jnp.where((lane >= lo) & (lane < lo + MOE_PER_GROUP), logits, neg)
    m1 = jnp.max(le, axis=-1, keepdims=True)
    i1 = first_lane(le == m1)
    le2 = jnp.where(lane == i1, neg, le)
    m2 = jnp.max(le2, axis=-1, keepdims=True)
    i2 = first_lane(le2 == m2)
    e2 = jnp.exp(m2 - m1)
    g1 = p_grp / (1.0 + e2)
    g2 = p_grp * e2 / (1.0 + e2)
    ids_ref[...] = jnp.where(lane == 0, i1 - MOE_GROUPS, jnp.where(lane == 1, i2 - MOE_GROUPS, 0))
    gates_ref[...] = jnp.where(lane == 0, g1, jnp.where(lane == 1, g2, 0.0))


def _modnorm(x, g, mods2, *, shift_idx, scale_idx, lt, n_lat, out_dtype, router_w=None, router_b=None):
    rows, d = x.shape
    tm = _pick(math.gcd(lt, n_lat), (256, 128))
    tpb = lt // tm
    lat_tiles = n_lat // tm
    router = router_w is not None
    in_specs = [pl.BlockSpec((tm, d), lambda i: (i, 0)),
                pl.BlockSpec((1, d), lambda i: (0, 0)),
                pl.BlockSpec((None, 6, d), lambda i: ((i // tpb) * 2 + ((i % tpb) >= lat_tiles).astype(jnp.int32), 0, 0))]
    args = [x, g.reshape(1, d), mods2]
    out_specs = [pl.BlockSpec((tm, d), lambda i: (i, 0))]
    out_shape = [jax.ShapeDtypeStruct((rows, d), out_dtype)]
    if router:
        n_chunks = d // (2 * LANES)
        out_specs = [pl.BlockSpec((tm * n_chunks, LANES), lambda i: (i, 0))]
        out_shape = [jax.ShapeDtypeStruct((rows * n_chunks, LANES), jnp.uint32)]
        in_specs += [pl.BlockSpec((d, LANES), lambda i: (0, 0)), pl.BlockSpec((1, LANES), lambda i: (0, 0))]
        args += [router_w, router_b]
        out_specs += [pl.BlockSpec((tm, LANES), lambda i: (i, 0))] * 2
        out_shape += [jax.ShapeDtypeStruct((rows, LANES), jnp.int32), jax.ShapeDtypeStruct((rows, LANES), F32)]
    body = functools.partial(_modnorm_body, shift_idx=shift_idx, scale_idx=scale_idx, router=router)
    out = pl.pallas_call(body, grid=(rows // tm,), in_specs=in_specs, out_specs=out_specs, out_shape=out_shape,
                         compiler_params=_cparams(("parallel",)), name="modnorm_router" if router else "modnorm")(*args)
    return out if router else out[0]


def _token_gather_start(idx_ref, base, n, src_hbm, dst, sem, *, rpt, dst_base=0, idx_stride=1):
    def body(r, c):
        tok = idx_ref[base + r * idx_stride]
        pltpu.make_async_copy(src_hbm.at[pl.ds(pl.multiple_of(tok * rpt, rpt), rpt)],
                              dst.at[pl.ds(pl.multiple_of((dst_base + r) * rpt, rpt), rpt)], sem).start()
        return c

    lax.fori_loop(0, n, body, 0)


def _token_gather_wait(n, src_hbm, dst, sem, *, rpt):
    def body(r, c):
        pltpu.make_async_copy(src_hbm.at[pl.ds(0, rpt)], dst.at[pl.ds(pl.multiple_of(r * rpt, rpt), rpt)], sem).wait()
        return c

    lax.fori_loop(0, n, body, 0)


def _moe_ffn_body(src_ref, blk_e_ref, n_used_ref, h_hbm, w1_ref, w3_ref, w2_ref, o_ref,
                  xbuf, xb, w1b, w3b, w2b, sems, *, blk):
    i = pl.program_id(0)
    slot = i % 2
    n_used = n_used_ref[0]
    d = xb.shape[1]
    n_in = d // (2 * LANES)
    n_out = d // LANES

    @pl.when(i == 0)
    def _():
        _token_gather_start(src_ref, 0, blk, h_hbm, xbuf.at[0], sems.at[0], rpt=n_in)

    @pl.when(i + 1 < n_used)
    def _():
        _token_gather_start(src_ref, (i + 1) * blk, blk, h_hbm, xbuf.at[1 - slot], sems.at[1 - slot], rpt=n_in)

    @pl.when(i < n_used)
    def _():
        @pl.when(jnp.logical_or(i == 0, blk_e_ref[i] != blk_e_ref[jnp.maximum(i - 1, 0)]))
        def _():
            w1b[...] = w1_ref[...].astype(BF16)
            w3b[...] = w3_ref[...].astype(BF16)
            w2b[...] = w2_ref[...].astype(BF16)

        _token_gather_wait(blk, h_hbm, xbuf.at[slot], sems.at[slot], rpt=n_in)
        xs = xbuf.at[slot]
        for cc in range(n_in):
            u = xs[pl.ds(cc, blk, stride=n_in), :]
            xb[:, cc * LANES:(cc + 1) * LANES] = lax.bitcast_convert_type(u << 16, F32).astype(BF16)
            xb[:, d // 2 + cc * LANES:d // 2 + (cc + 1) * LANES] = lax.bitcast_convert_type(
                u & jnp.uint32(0xFFFF0000), F32).astype(BF16)
        x = xb[...]
        a1 = jnp.dot(x, w1b[...], preferred_element_type=F32)
        a3 = jnp.dot(x, w3b[...], preferred_element_type=F32)
        mid = (_silu(a1) * a3).astype(BF16)
        y = jnp.dot(mid, w2b[...], preferred_element_type=F32)
        for cc in range(n_out):
            o_ref[pl.ds(cc, blk, stride=n_out), :] = y[:, cc * LANES:(cc + 1) * LANES]

    @pl.when(i >= n_used)
    def _():
        o_ref[...] = jnp.zeros_like(o_ref)


def _moe_ffn(hpk, src, blk_e, n_used, w1, w3, w2, layer, *, blk, d):
    n_rows = src.shape[0]
    n_blk = n_rows // blk
    hid = w1.shape[-1]
    n_in = d // (2 * LANES)
    n_out = d // LANES
    grid_spec = pltpu.PrefetchScalarGridSpec(
        num_scalar_prefetch=3, grid=(n_blk,),
        in_specs=[pl.BlockSpec(memory_space=pl.ANY),
                  pl.BlockSpec((None, None, d, hid), lambda i, s, e, n: (layer, e[i], 0, 0)),
                  pl.BlockSpec((None, None, d, hid), lambda i, s, e, n: (layer, e[i], 0, 0)),
                  pl.BlockSpec((None, None, hid, d), lambda i, s, e, n: (layer, e[i], 0, 0))],
        out_specs=pl.BlockSpec((blk * n_out, LANES), lambda i, s, e, n: (i, 0)),
        scratch_shapes=[pltpu.VMEM((2, blk * n_in, LANES), jnp.uint32), pltpu.VMEM((blk, d), BF16),
                        pltpu.VMEM((d, hid), BF16), pltpu.VMEM((d, hid), BF16), pltpu.VMEM((hid, d), BF16),
                        pltpu.SemaphoreType.DMA((2,))])
    return pl.pallas_call(
        functools.partial(_moe_ffn_body, blk=blk), grid_spec=grid_spec,
        out_shape=jax.ShapeDtypeStruct((n_rows * n_out, LANES), F32),
        compiler_params=_cparams(("arbitrary",)), name="moe_ffn")(src, blk_e, n_used, hpk, w1, w3, w2)


def _moe_combine_body(pos_ref, x_ref, gates_ref, mod_ref, yb_hbm, o_ref, ybuf, sems, *, tm, n_tiles):
    i = pl.program_id(0)
    slot = i % 2
    rpt = x_ref.shape[1] // LANES

    def start(tile, s):
        for kk in range(MOE_TOPK):
            _token_gather_start(pos_ref, tile * tm * MOE_TOPK + kk, tm, yb_hbm, ybuf.at[s], sems.at[s], rpt=rpt,
                                dst_base=kk * tm, idx_stride=MOE_TOPK)

    @pl.when(i == 0)
    def _():
        start(0, 0)

    @pl.when(i + 1 < n_tiles)
    def _():
        start(i + 1, 1 - slot)

    _token_gather_wait(MOE_TOPK * tm, yb_hbm, ybuf.at[slot], sems.at[slot], rpt=rpt)
    g0 = jnp.broadcast_to(gates_ref[:, 0:1], (tm, LANES))
    g1 = jnp.broadcast_to(gates_ref[:, 1:2], (tm, LANES))
    ys = ybuf.at[slot]
    for cc in range(rpt):
        sl = slice(cc * LANES, (cc + 1) * LANES)
        y = g0 * ys[pl.ds(cc, tm, stride=rpt), :] + g1 * ys[pl.ds(tm * rpt + cc, tm, stride=rpt), :]
        o_ref[:, sl] = x_ref[:, sl] + mod_ref[pl.ds(5, 1), sl] * y


def _moe_combine(x, gates, mods2, yb, pos, *, lt, n_lat):
    rows, d = x.shape
    tm = 128
    tpb = lt // tm
    lat_tiles = n_lat // tm
    n_tiles = rows // tm
    rpt = d // LANES
    grid_spec = pltpu.PrefetchScalarGridSpec(
        num_scalar_prefetch=1, grid=(n_tiles,),
        in_specs=[pl.BlockSpec((tm, d), lambda i, p: (i, 0)),
                  pl.BlockSpec((tm, LANES), lambda i, p: (i, 0)),
                  pl.BlockSpec((None, 6, d),
                               lambda i, p: ((i // tpb) * 2 + ((i % tpb) >= lat_tiles).astype(jnp.int32), 0, 0)),
                  pl.BlockSpec(memory_space=pl.ANY)],
        out_specs=pl.BlockSpec((tm, d), lambda i, p: (i, 0)),
        scratch_shapes=[pltpu.VMEM((2, MOE_TOPK * tm * rpt, LANES), F32), pltpu.SemaphoreType.DMA((2,))])
    return pl.pallas_call(
        functools.partial(_moe_combine_body, tm=tm, n_tiles=n_tiles), grid_spec=grid_spec,
        out_shape=jax.ShapeDtypeStruct((rows, d), F32),
        compiler_params=_cparams(("arbitrary",)), name="moe_combine")(pos, x, gates, mods2, yb)


def _moe_layer(x, mods2, norm_g, router_w, router_b, w1, w3, w2, layer, *, lt, n_lat):
    rows, d = x.shape
    n_exp = w1.shape[1]
    hpk, ids, gates = _modnorm(x, norm_g, mods2, shift_idx=3, scale_idx=4, lt=lt, n_lat=n_lat, out_dtype=F32,
                               router_w=router_w, router_b=router_b)
    blk = MOE_BLOCK
    n_asg = rows * MOE_TOPK
    n_blk = n_asg // blk + n_exp
    flat = ids[:, :MOE_TOPK].reshape(-1)
    order = jnp.argsort(flat).astype(jnp.int32)
    sorted_e = flat[order]
    counts = jnp.sum((flat[:, None] == jnp.arange(n_exp, dtype=jnp.int32)[None, :]).astype(jnp.int32), axis=0)
    padded = (counts + blk - 1) // blk * blk
    pad_end = jnp.cumsum(padded)
    pad_start = pad_end - padded
    start = jnp.cumsum(counts) - counts
    dest = (pad_start[sorted_e] + jnp.arange(n_asg, dtype=jnp.int32) - start[sorted_e]).astype(jnp.int32)
    blk_first = jnp.arange(n_blk, dtype=jnp.int32) * blk
    blk_e = jnp.minimum(jnp.sum((pad_end[None, :] <= blk_first[:, None]).astype(jnp.int32), axis=1), n_exp - 1)
    n_used = (pad_end[-1:] // blk).astype(jnp.int32)
    within = jnp.arange(n_blk * blk, dtype=jnp.int32) - jnp.repeat(pad_start[blk_e], blk)
    valid = within < jnp.repeat(counts[blk_e], blk)
    sorted_idx = jnp.clip(jnp.repeat(start[blk_e], blk) + within, 0, n_asg - 1)
    src = jnp.where(valid, order[sorted_idx] // MOE_TOPK, 0).astype(jnp.int32)
    pos = dest[jnp.argsort(order)]
    yb = _moe_ffn(hpk, src, blk_e.astype(jnp.int32), n_used, w1, w3, w2, layer, blk=blk, d=d)
    return _moe_combine(x, gates, mods2, yb, pos.astype(jnp.int32), lt=lt, n_lat=n_lat)


def _rope_tables(n_lat, n_ctx, rot_dim, width):
    rows = n_lat // GRID_W
    row = jnp.repeat(jnp.arange(rows, dtype=F32), GRID_W)
    col = jnp.tile(jnp.arange(GRID_W, dtype=F32), rows)
    axis_dim = rot_dim // 2
    inv_freq = ROPE_BASE ** (-jnp.arange(0, axis_dim, 2, dtype=F32) / axis_dim)
    a0 = row[:, None] * inv_freq
    a1 = col[:, None] * inv_freq
    cos = jnp.concatenate([jnp.cos(a0), jnp.cos(a0), jnp.cos(a1), jnp.cos(a1)], -1)
    sin = jnp.concatenate([-jnp.sin(a0), jnp.sin(a0), -jnp.sin(a1), jnp.sin(a1)], -1)
    cos = jnp.pad(cos, ((0, n_ctx), (0, width - rot_dim)), constant_values=1.0)
    sin = jnp.pad(sin, ((0, n_ctx), (0, width - rot_dim)))
    return cos, sin


def _rope(x, cos, sin, half):
    lane = lax.broadcasted_iota(jnp.int32, x.shape, 1)
    partner = jnp.where((lane % (2 * half)) < half, pltpu.roll(x, LANES - half, 1), pltpu.roll(x, half, 1))
    return x * cos + partner * sin


def _swa_prep_body(p_ref, cos_ref, sin_ref, qg_ref, kg_ref, q_ref, k_ref, v_ref, *, n_q, n_kv):
    hd = SWA_HEAD_DIM
    cos, sin = cos_ref[...], sin_ref[...]
    scale = hd ** -0.5

    def norm_rope(t, g):
        t = t * lax.rsqrt(jnp.mean(t * t, axis=-1, keepdims=True) + NORM_EPS) * g
        return _rope(t, cos, sin, hd // 4)

    for h in range(n_q):
        q_ref[:, h * hd:(h + 1) * hd] = (norm_rope(p_ref[:, h * hd:(h + 1) * hd], qg_ref[...]) * scale).astype(BF16)
    for h in range(n_kv):
        c0 = (n_q + h) * hd
        k_ref[:, h * hd:(h + 1) * hd] = norm_rope(p_ref[:, c0:c0 + hd], kg_ref[...]).astype(BF16)
    v_ref[...] = p_ref[:, (n_q + n_kv) * hd:].astype(BF16)


def _swa_attn_body(sink_ref, q_ref, kp_ref, kc_ref, kn_ref, kx_ref, vp_ref, vc_ref, vn_ref, vx_ref, o_ref, *,
                   n_lat, grp):
    kv = pl.program_id(1)
    j = pl.program_id(2)
    blk = SWA_BLOCK
    hd = SWA_HEAD_DIM
    n_ctx = kx_ref.shape[0]
    lat_blocks = n_lat // blk
    is_lat = j < lat_blocks
    qpos = j * blk + lax.broadcasted_iota(jnp.int32, (blk, 1), 0)
    biases = []
    for w in range(3):
        kpos = (j - 1 + w) * blk + lax.broadcasted_iota(jnp.int32, (1, blk), 1)
        valid = (jnp.abs(kpos - qpos) <= SWA_WINDOW) & (kpos >= 0) & (kpos < n_lat) & is_lat
        biases.append(jnp.where(valid, 0.0, -jnp.inf))
    biases.append(jnp.zeros((blk, n_ctx), F32))
    bias = jnp.concatenate(biases, axis=1)
    k_all = jnp.concatenate([kp_ref[...], kc_ref[...], kn_ref[...], kx_ref[...]], axis=0)
    v_all = jnp.concatenate([vp_ref[...], vc_ref[...], vn_ref[...], vx_ref[...]], axis=0)
    for g in range(grp):
        sink = sink_ref[kv * grp + g]
        s = _nt_dot(q_ref[:, g * hd:(g + 1) * hd], k_all) + bias
        m = jnp.maximum(jnp.max(s, axis=-1, keepdims=True), sink)
        p = jnp.exp(s - m)
        denom = jnp.sum(p, axis=-1, keepdims=True) + jnp.exp(sink - m)
        o = jnp.dot(p.astype(BF16), v_all, preferred_element_type=F32) / denom
        o_ref[:, g * hd:(g + 1) * hd] = o.astype(o_ref.dtype)


def _mixer_swa(h, w_in, q_g, k_g, sinks, *, batch, lt, n_lat):
    rows = h.shape[0]
    hd = SWA_HEAD_DIM
    n_q, n_kv = SWA_HEADS, SWA_KV_HEADS
    grp = n_q // n_kv
    n_ctx = lt - n_lat
    n_cols = (n_q + 2 * n_kv) * hd
    tm = _pick(lt, (768, 512, 384, 256, 128))
    tn = _pick(n_cols, (512, 256, 128))
    p = _matmul(h, w_in, batch=batch, col_blocks=[0], n_out=n_cols, tm=tm, tn=tn, name="swa_in")
    cos, sin = _rope_tables(n_lat, n_ctx, hd, hd)
    tp = _pick(math.gcd(lt, n_lat), (256, 128))
    tpb = lt // tp
    q, k, v = pl.pallas_call(
        functools.partial(_swa_prep_body, n_q=n_q, n_kv=n_kv), grid=(rows // tp,),
        in_specs=[pl.BlockSpec((tp, n_cols), lambda i: (i, 0)),
                  pl.BlockSpec((tp, hd), lambda i: (i % tpb, 0)),
                  pl.BlockSpec((tp, hd), lambda i: (i % tpb, 0)),
                  pl.BlockSpec((1, hd), lambda i: (0, 0)),
                  pl.BlockSpec((1, hd), lambda i: (0, 0))],
        out_specs=[pl.BlockSpec((tp, n_q * hd), lambda i: (i, 0)),
                   pl.BlockSpec((tp, n_kv * hd), lambda i: (i, 0)),
                   pl.BlockSpec((tp, n_kv * hd), lambda i: (i, 0))],
        out_shape=[jax.ShapeDtypeStruct((rows, n_q * hd), BF16),
                   jax.ShapeDtypeStruct((rows, n_kv * hd), BF16),
                   jax.ShapeDtypeStruct((rows, n_kv * hd), BF16)],
        compiler_params=_cparams(("parallel",)), name="swa_prep")(p, cos, sin, q_g.reshape(1, hd), k_g.reshape(1, hd))

    blk = SWA_BLOCK
    bpb = lt // blk
    lat_blocks = n_lat // blk

    def win(off):
        return lambda b, kvh, j: (b * bpb + jnp.clip(j + off, 0, lat_blocks - 1), kvh)

    ctx_spec = pl.BlockSpec((None, n_ctx, hd), lambda b, kvh, j: (b, n_lat // n_ctx, kvh))
    k3 = k.reshape(batch, lt, n_kv * hd)
    v3 = v.reshape(batch, lt, n_kv * hd)
    kv_spec = [pl.BlockSpec((blk, hd), win(-1)), pl.BlockSpec((blk, hd), win(0)), pl.BlockSpec((blk, hd), win(1))]
    return pl.pallas_call(
        functools.partial(_swa_attn_body, n_lat=n_lat, grp=grp), grid=(batch, n_kv, bpb),
        in_specs=[pl.BlockSpec(memory_space=pltpu.SMEM),
                  pl.BlockSpec((blk, grp * hd), lambda b, kvh, j: (b * bpb + j, kvh))]
        + kv_spec + [ctx_spec] + kv_spec + [ctx_spec],
        out_specs=pl.BlockSpec((blk, grp * hd), lambda b, kvh, j: (b * bpb + j, kvh)),
        out_shape=jax.ShapeDtypeStruct((rows, n_q * hd), BF16),
        compiler_params=_cparams(("parallel", "parallel", "arbitrary")), name="swa_attn")(
            sinks.astype(F32), q, k, k, k, k3, v, v, v, v3)


def _conv_silu_body(x_ref, prev_ref, next_ref, w_ref, b_ref, o_ref, *, lat_tiles, n_tiles):
    t = pl.program_id(1)
    x = x_ref[...]
    rows = x.shape[0]
    has_prev = jnp.logical_and(t != 0, t != lat_tiles).astype(F32)
    has_next = jnp.logical_and(t != lat_tiles - 1, t != n_tiles - 1).astype(F32)
    row = lax.broadcasted_iota(jnp.int32, x.shape, 0)
    x_m = jnp.where(row == 0, prev_ref[SUBLANES - 1:SUBLANES, :] * has_prev, pltpu.roll(x, 1, 0))
    x_p = jnp.where(row == rows - 1, next_ref[0:1, :] * has_next, pltpu.roll(x, rows - 1, 0))
    y = w_ref[0:1, :] * x_m + w_ref[1:2, :] * x + w_ref[2:3, :] * x_p + b_ref[...]
    o_ref[...] = _silu(y).astype(o_ref.dtype)


def _conv_silu(xbc, conv_w, conv_b, *, batch, lt, n_lat):
    cols = xbc.shape[-1]
    x3 = xbc.reshape(batch, lt, cols)
    tt = _pick(math.gcd(lt, n_lat), (256, 128))
    tc = _pick(cols, (2048, 1024, 512, 256, 128))
    n_tiles = lt // tt
    sub = tt // SUBLANES
    out = pl.pallas_call(
        functools.partial(_conv_silu_body, lat_tiles=n_lat // tt, n_tiles=n_tiles),
        grid=(batch, n_tiles, cols // tc),
        in_specs=[pl.BlockSpec((None, tt, tc), lambda b, t, j: (b, t, j)),
                  pl.BlockSpec((None, SUBLANES, tc), lambda b, t, j: (b, jnp.maximum(t * sub - 1, 0), j)),
                  pl.BlockSpec((None, SUBLANES, tc), lambda b, t, j: (b, jnp.minimum((t + 1) * sub, lt // SUBLANES - 1), j)),
                  pl.BlockSpec((3, tc), lambda b, t, j: (0, j)),
                  pl.BlockSpec((1, tc), lambda b, t, j: (0, j))],
        out_specs=pl.BlockSpec((None, tt, tc), lambda b, t, j: (b, t, j)),
        out_shape=jax.ShapeDtypeStruct((batch, lt, cols), BF16),
        compiler_params=_cparams(("parallel", "parallel", "parallel")), name="ssd_conv")(
            x3, x3, x3, conv_w, conv_b.reshape(1, cols))
    return out


def _ssd_pass_body(*refs, direction, hg, final):
    if final:
        (x_ref, b_ref, c_ref, dt_ref, dtt_ref, bias_ref, biast_ref, alog_ref, alogt_ref,
         y0_ref, z_ref, ng_ref, o_ref, state_ref, g_ref) = refs
    else:
        (x_ref, b_ref, c_ref, dt_ref, dtt_ref, bias_ref, biast_ref, alog_ref, alogt_ref,
         dsk_ref, o_ref, state_ref) = refs
    t_len = x_ref.shape[0]

    @pl.when(pl.program_id(2) == 0)
    def _():
        state_ref[...] = jnp.zeros_like(state_ref)

    bm = b_ref[...]
    cm = c_ref[...]
    v = _softplus(dt_ref[...] + bias_ref[...])
    dt = v
    v = v * (-jnp.exp(alog_ref[...]))
    v_t = _softplus(dtt_ref[...] + biast_ref[...]) * (-jnp.exp(alogt_ref[...]))
    row = lax.broadcasted_iota(jnp.int32, (t_len, t_len), 0)
    col = lax.broadcasted_iota(jnp.int32, (t_len, t_len), 1)
    tri = (col <= row) if direction == 0 else (col >= row)
    tri_f = tri.astype(F32)
    cum = jnp.dot(tri_f, v, precision=HIGHEST, preferred_element_type=F32)
    cum_t = lax.dot_general(v_t, tri_f, (((1,), (1,)), ((), ())), precision=HIGHEST,
                            preferred_element_type=F32)
    total = jnp.sum(v, axis=0, keepdims=True)
    exp_cum = jnp.exp(cum)
    to_end = jnp.exp(total - cum)
    exp_total = jnp.exp(total)
    cb = _nt_dot(cm, bm)
    b_t = bm.astype(F32).T.astype(BF16)
    left = lax.broadcasted_iota(jnp.int32, (t_len, LANES), 1) < SSD_HEAD_DIM
    left_row = lax.broadcasted_iota(jnp.int32, (1, LANES), 1) < SSD_HEAD_DIM

    def expand(t, h0):
        return jnp.where(left, t[:, h0:h0 + 1], t[:, h0 + 1:h0 + 2])

    for pr in range(hg // 2):
        h0 = 2 * pr
        sl = slice(pr * LANES, (pr + 1) * LANES)
        xp = x_ref[:, sl].astype(F32)
        xdt = xp * expand(dt, h0)
        xdt_b = xdt.astype(BF16)
        ys = []
        for hh in (h0, h0 + 1):
            seg = cum[:, hh:hh + 1] - cum_t[hh:hh + 1, :]
            dec = jnp.exp(jnp.where(tri, seg, -jnp.inf))
            ys.append(jnp.dot((cb * dec).astype(BF16), xdt_b, preferred_element_type=F32))
        y = jnp.where(left, ys[0], ys[1])
        st = state_ref[:, sl]
        y = y + jnp.dot(cm, st.astype(BF16), preferred_element_type=F32) * expand(exp_cum, h0)
        xw = (xdt * expand(to_end, h0)).astype(BF16)
        decay = jnp.where(left_row, exp_total[:, h0:h0 + 1], exp_total[:, h0 + 1:h0 + 2])
        state_ref[:, sl] = st * decay + jnp.dot(b_t, xw, preferred_element_type=F32)
        if final:
            z = z_ref[:, sl]
            g_ref[:, sl] = (y0_ref[:, sl] + y) * _silu(z)
        else:
            o_ref[:, sl] = dsk_ref[:, sl] * xp + y
    if final:
        g = g_ref[...]
        o_ref[...] = (g * lax.rsqrt(jnp.mean(g * g, axis=-1, keepdims=True) + NORM_EPS) * ng_ref[...]).astype(o_ref.dtype)


def _chunk_order(i, direction, lat_chunks, ctx_chunks):
    if direction == 0:
        return jnp.where(i < ctx_chunks, lat_chunks + i, i - ctx_chunks)
    return lat_chunks + ctx_chunks - 1 - i


def _ssd_pass(direction, xbc_s, dt_l, dt_t, bias, bias_t, alog, alog_t, extra, *, batch, lt, n_lat, d_inner, final):
    groups = SSD_GROUPS
    gw = d_inner // groups
    hg = gw // SSD_HEAD_DIM
    n_state = SSD_STATE
    t_len = SSD_CHUNK
    lat_chunks, ctx_chunks = n_lat // t_len, (lt - n_lat) // t_len
    n_chunks = lat_chunks + ctx_chunks
    xb = d_inner // n_state

    def cidx(i):
        return _chunk_order(i, direction, lat_chunks, ctx_chunks)

    in_specs = [
        pl.BlockSpec((None, t_len, gw), lambda b, g, i: (b, cidx(i), g)),
        pl.BlockSpec((None, t_len, n_state), lambda b, g, i: (b, cidx(i), xb + g)),
        pl.BlockSpec((None, t_len, n_state), lambda b, g, i: (b, cidx(i), xb + groups + g)),
        pl.BlockSpec((None, None, None, t_len, hg), lambda b, g, i: (direction, g, b, cidx(i), 0)),
        pl.BlockSpec((None, None, None, hg, t_len), lambda b, g, i: (direction, g, b, 0, cidx(i))),
        pl.BlockSpec((None, None, 1, hg), lambda b, g, i: (direction, g, 0, 0)),
        pl.BlockSpec((None, None, hg, 1), lambda b, g, i: (direction, g, 0, 0)),
        pl.BlockSpec((None, None, 1, hg), lambda b, g, i: (direction, g, 0, 0)),
        pl.BlockSpec((None, None, hg, 1), lambda b, g, i: (direction, g, 0, 0)),
    ]
    args = [xbc_s, xbc_s, xbc_s, dt_l, dt_t, bias, bias_t, alog, alog_t]
    row_spec = pl.BlockSpec((None, t_len, gw), lambda b, g, i: (b, cidx(i), g))
    vec_spec = pl.BlockSpec((1, gw), lambda b, g, i: (0, g))
    scratch = [pltpu.VMEM((n_state, gw), F32)]
    if final:
        y0, z, norm_g = extra
        in_specs += [row_spec, row_spec, vec_spec]
        args += [y0, z, norm_g]
        out_dtype = BF16
        scratch.append(pltpu.VMEM((t_len, gw), F32))
    else:
        (dsk,) = extra
        in_specs += [vec_spec]
        args += [dsk]
        out_dtype = F32
    return pl.pallas_call(
        functools.partial(_ssd_pass_body, direction=direction, hg=hg, final=final),
        grid=(batch, groups, n_chunks), in_specs=in_specs, out_specs=row_spec,
        out_shape=jax.ShapeDtypeStruct((batch, lt, d_inner), out_dtype), scratch_shapes=scratch,
        compiler_params=_cparams(("parallel", "parallel", "arbitrary")), name=f"ssd_pass{direction}")(*args)


def _mixer_ssd(h, w_in, conv_w, conv_b, dt_bias, a_log, d_skip, norm_g, *, batch, lt, n_lat):
    rows = h.shape[0]
    groups = SSD_GROUPS
    n_heads = dt_bias.shape[-1]
    d_inner = n_heads * SSD_HEAD_DIM
    hg = n_heads // groups
    gn = groups * SSD_STATE
    tm = _pick(lt, (768, 512, 384, 256, 128))
    tn = _pick(math.gcd(d_inner, 2 * gn), (512, 256, 128))
    z = _matmul(h, w_in, batch=batch, col_blocks=[0], n_out=d_inner, tm=tm, tn=tn, name="ssd_in_z")
    xbc = _matmul(h, w_in, batch=batch, col_blocks=[d_inner // tn], n_out=d_inner + 2 * gn, tm=tm, tn=tn,
                  name="ssd_in_xbc")
    tdt = _pick(2 * n_heads, (256, 128, 64, 32))
    dt = _matmul(h, w_in, batch=batch, col_blocks=[(2 * d_inner + 2 * gn) // tdt], n_out=2 * n_heads, tm=tm, tn=tdt,
                 name="ssd_in_dt")
    xbc_s = _conv_silu(xbc, conv_w, conv_b, batch=batch, lt=lt, n_lat=n_lat)
    dt5 = dt.reshape(batch, lt, 2, groups, hg)
    dt_l = dt5.transpose(2, 3, 0, 1, 4)
    dt_t = dt5.transpose(2, 3, 0, 4, 1)
    bias = dt_bias.astype(F32).reshape(2, groups, 1, hg)
    bias_t = dt_bias.astype(F32).reshape(2, groups, hg, 1)
    alog = a_log.astype(F32).reshape(2, groups, 1, hg)
    alog_t = a_log.astype(F32).reshape(2, groups, hg, 1)
    dsk = jnp.repeat(d_skip.astype(F32), SSD_HEAD_DIM).reshape(1, d_inner)
    kw = dict(batch=batch, lt=lt, n_lat=n_lat, d_inner=d_inner)
    y0 = _ssd_pass(0, xbc_s, dt_l, dt_t, bias, bias_t, alog, alog_t, (dsk,), final=False, **kw)
    g = _ssd_pass(1, xbc_s, dt_l, dt_t, bias, bias_t, alog, alog_t,
                  (y0, z.reshape(batch, lt, d_inner), norm_g.reshape(1, d_inner)), final=True, **kw)
    return g.reshape(rows, d_inner)


def _s5_scan_body(u_ref, bbd_ref, cbd_ref, a_ref, y_ref, xs_ref, s_ref):
    d = pl.program_id(0)
    t_len, n_b, width = u_ref.shape
    ns = a_ref.shape[-1] // 2

    @pl.when(pl.program_id(2) == 0)
    def _():
        s_ref[...] = jnp.zeros_like(s_ref)

    u = u_ref[...].reshape(t_len * n_b, width).astype(BF16)
    xs_ref[...] = jnp.dot(u, bbd_ref[...], preferred_element_type=F32)
    ar = a_ref[:, :ns]
    ai = a_ref[:, ns:]

    def step(k, carry):
        sr, si = carry
        t = jnp.where(d == 0, k, t_len - 1 - k)
        r0 = pl.multiple_of(t * n_b, n_b)
        xr = xs_ref[pl.ds(r0, n_b), :ns]
        xi = xs_ref[pl.ds(r0, n_b), ns:]
        nr = ar * sr - ai * si + xr
        ni = ar * si + ai * sr + xi
        xs_ref[pl.ds(r0, n_b), :ns] = nr
        xs_ref[pl.ds(r0, n_b), ns:] = ni
        return nr, ni

    sr, si = lax.fori_loop(0, t_len, step, (s_ref[:, :ns], s_ref[:, ns:]))
    s_ref[:, :ns] = sr
    s_ref[:, ns:] = si
    y = jnp.dot(xs_ref[...].astype(BF16), cbd_ref[...], preferred_element_type=F32)
    y_ref[...] = y.reshape(t_len, n_b, width)


def _s5_merge_body(u_ref, y0_ref, y1_ref, dsk_ref, o_ref, g_ref):
    tt, n_b, tc = u_ref.shape
    y = dsk_ref[...] * u_ref[...] + y0_ref[...] + y1_ref[...]
    g = 0.5 * y * (1.0 + jnp.tanh(math.sqrt(2.0 / math.pi) * (y + 0.044715 * (y * y * y))))
    g = g.reshape(tt * n_b, tc)
    for cc in range(tc // LANES):
        g_ref[cc] = g[:, cc * LANES:(cc + 1) * LANES]
    for b in range(n_b):
        for cc in range(tc // LANES):
            o_ref[b, :, cc * LANES:(cc + 1) * LANES] = g_ref[cc, pl.ds(b, tt, stride=n_b), :].astype(o_ref.dtype)


def _s5_params(lam_re, lam_im, log_dt, b_re, b_im, c_re, c_im, n_b):
    n_groups, n_state, gsz = b_re.shape
    sg = S5_SLAB_GROUPS
    n_slab = n_groups // sg
    eye = jnp.eye(sg, dtype=F32)
    bbds, a_s = [], []
    br, bi = b_re.astype(F32), b_im.astype(F32)
    for d in range(2):
        lr = lam_re[d].astype(F32)
        li = lam_im[d].astype(F32)
        step = jnp.exp(log_dt[d].astype(F32))[:, None]
        mag = jnp.exp(lr * step)
        ar, ai = mag * jnp.cos(li * step), mag * jnp.sin(li * step)
        den = lr * lr + li * li
        fr = ((ar - 1.0) * lr + ai * li) / den
        fi = (ai * lr - (ar - 1.0) * li) / den
        bbr = fr[..., None] * br - fi[..., None] * bi
        bbi = fr[..., None] * bi + fi[..., None] * br

        def bdiag_in(m):
            m = m.reshape(n_slab, sg, n_state, gsz)
            return jnp.einsum("sgpc,gh->sgchp", m, eye).reshape(n_slab, sg * gsz, sg * n_state)

        bbds.append(jnp.concatenate([bdiag_in(bbr), bdiag_in(bbi)], -1))
        a_cat = jnp.concatenate([ar.reshape(n_slab, sg * n_state), ai.reshape(n_slab, sg * n_state)], -1)
        a_s.append(jnp.broadcast_to(a_cat[:, None, :], (n_slab, n_b, 2 * sg * n_state)))

    def bdiag_out(m):
        m = m.reshape(n_slab, sg, gsz, n_state)
        return jnp.einsum("sgcp,gh->sgphc", m, eye).reshape(n_slab, sg * n_state, sg * gsz)

    cbd = jnp.concatenate([bdiag_out(c_re.astype(F32)), -bdiag_out(c_im.astype(F32))], 1)
    return jnp.stack(bbds).astype(BF16), cbd.astype(BF16), jnp.stack(a_s)


def _mixer_s5(h, w_in, lam_re, lam_im, log_dt, b_re, b_im, c_re, c_im, d_skip, *, batch, lt, n_lat):
    rows, d = h.shape
    width = w_in.shape[1]
    tm = _pick(lt, (768, 512, 384, 256, 128))
    tn = _pick(width, (512, 256, 128))
    tpb = lt // tm
    ncb = width // tn
    u = _matmul(h, w_in, batch=batch, col_blocks=[0], n_out=width, tm=tm, tn=tn, out_shape=(lt, batch * width),
                out_map=lambda i, j, k: (i % tpb, (i // tpb) * ncb + j), name="s5_in")
    bbd, cbd, a_bc = _s5_params(lam_re, lam_im, log_dt, b_re, b_im, c_re, c_im, batch)
    n_slab, sw = bbd.shape[1], bbd.shape[2]
    ns2 = bbd.shape[3]
    t_len = S5_CHUNK
    lat_chunks, ctx_chunks = n_lat // t_len, (lt - n_lat) // t_len
    n_chunks = lat_chunks + ctx_chunks

    def cidx(dd, i):
        fwd = jnp.where(i < ctx_chunks, lat_chunks + i, i - ctx_chunks)
        return jnp.where(dd == 0, fwd, n_chunks - 1 - i)

    u3 = u.reshape(lt, batch, width)
    ys = pl.pallas_call(
        _s5_scan_body, grid=(2, n_slab, n_chunks),
        in_specs=[pl.BlockSpec((t_len, batch, sw), lambda dd, s, i: (cidx(dd, i), 0, s)),
                  pl.BlockSpec((None, None, sw, ns2), lambda dd, s, i: (dd, s, 0, 0)),
                  pl.BlockSpec((None, ns2, sw), lambda dd, s, i: (s, 0, 0)),
                  pl.BlockSpec((None, None, batch, ns2), lambda dd, s, i: (dd, s, 0, 0))],
        out_specs=pl.BlockSpec((None, t_len, batch, sw), lambda dd, s, i: (dd, cidx(dd, i), 0, s)),
        out_shape=jax.ShapeDtypeStruct((2, lt, batch, width), F32),
        scratch_shapes=[pltpu.VMEM((t_len * batch, ns2), F32), pltpu.VMEM((batch, ns2), F32)],
        compiler_params=_cparams(("parallel", "parallel", "arbitrary")), name="s5_scan")(u3, bbd, cbd, a_bc)
    tt = _pick(lt, (128,))
    tc = _pick(width, (1024, 512, 256, 128))
    g = pl.pallas_call(
        _s5_merge_body, grid=(lt // tt, width // tc),
        in_specs=[pl.BlockSpec((tt, batch, tc), lambda t, j: (t, 0, j)),
                  pl.BlockSpec((None, tt, batch, tc), lambda t, j: (0, t, 0, j)),
                  pl.BlockSpec((None, tt, batch, tc), lambda t, j: (1, t, 0, j)),
                  pl.BlockSpec((1, 1, tc), lambda t, j: (0, 0, j))],
        out_specs=pl.BlockSpec((batch, tt, tc), lambda t, j: (0, t, j)),
        out_shape=jax.ShapeDtypeStruct((batch, lt, width), BF16),
        scratch_shapes=[pltpu.VMEM((tc // LANES, tt * batch, LANES), F32)],
        compiler_params=_cparams(("parallel", "parallel")), name="s5_merge")(
            u3, ys, ys, d_skip.astype(F32).reshape(1, 1, width))
    return g.reshape(rows, width)


def _mla_norm_body(p_ref, qg_ref, kvg_ref, cq_ref, ckv_ref, kr_ref):
    def rms(t, g):
        return t * lax.rsqrt(jnp.mean(t * t, axis=-1, keepdims=True) + NORM_EPS) * g

    cq_ref[...] = rms(p_ref[:, :MLA_Q_RANK], qg_ref[...]).astype(cq_ref.dtype)
    ckv_ref[...] = rms(p_ref[:, MLA_Q_RANK:MLA_Q_RANK + MLA_KV_RANK], kvg_ref[...]).astype(ckv_ref.dtype)
    kr_ref[...] = p_ref[:, MLA_Q_RANK + MLA_KV_RANK:MLA_Q_RANK + MLA_KV_RANK + LANES]


def _mla_prep_body(q_ref, kv_ref, kr_ref, cos_ref, sin_ref, qg_ref, kg_ref, qo_ref, ko_ref, vo_ref, *, heads):
    dk = MLA_NOPE + MLA_ROPE
    hw = 2 * LANES
    scale = dk ** -0.5
    cos, sin = cos_ref[...], sin_ref[...]
    kr = kr_ref[...]
    kr_ss = jnp.sum(kr * kr, axis=-1, keepdims=True)
    qg_n, qg_r = qg_ref[:, :LANES], qg_ref[:, LANES:]
    kg_n, kg_r = kg_ref[:, :LANES], kg_ref[:, LANES:]
    for h in range(heads):
        c0 = h * hw
        qn = q_ref[:, c0:c0 + LANES]
        qr = q_ref[:, c0 + LANES:c0 + hw]
        rinv = lax.rsqrt((jnp.sum(qn * qn, axis=-1, keepdims=True) + jnp.sum(qr * qr, axis=-1, keepdims=True)) / dk
                         + NORM_EPS)
        qo_ref[:, c0:c0 + LANES] = (qn * rinv * qg_n * scale).astype(BF16)
        qo_ref[:, c0 + LANES:c0 + hw] = (_rope(qr * rinv * qg_r, cos, sin, MLA_ROPE // 4) * scale).astype(BF16)
        kn = kv_ref[:, c0:c0 + LANES]
        rinv = lax.rsqrt((jnp.sum(kn * kn, axis=-1, keepdims=True) + kr_ss) / dk + NORM_EPS)
        ko_ref[:, c0:c0 + LANES] = (kn * rinv * kg_n).astype(BF16)
        ko_ref[:, c0 + LANES:c0 + hw] = _rope(kr * rinv * kg_r, cos, sin, MLA_ROPE // 4).astype(BF16)
        vo_ref[:, h * LANES:(h + 1) * LANES] = kv_ref[:, c0 + LANES:c0 + hw].astype(BF16)


def _mla_attn_body(q_ref, k_ref, v_ref, o_ref):
    s = _nt_dot(q_ref[...], k_ref[...])
    m = jnp.max(s, axis=-1, keepdims=True)
    p = jnp.exp(s - m)
    denom = jnp.sum(p, axis=-1, keepdims=True)
    o_ref[...] = (jnp.dot(p.astype(BF16), v_ref[...], preferred_element_type=F32) / denom).astype(o_ref.dtype)


def _mixer_mla(h, w_in, q_a_g, kv_a_g, w_uq, w_ukv, q_g, k_g, *, batch, lt, n_lat):
    rows, d = h.shape
    heads = MLA_HEADS
    dk = MLA_NOPE + MLA_ROPE
    hw = 2 * LANES
    n_ctx = lt - n_lat
    n_in = MLA_Q_RANK + MLA_KV_RANK + MLA_ROPE
    n_in_pad = MLA_Q_RANK + MLA_KV_RANK + 2 * LANES
    w_in_p = jnp.pad(w_in, ((0, 0), (0, n_in_pad - n_in))).astype(BF16)
    w_uq_p = jnp.pad(w_uq.reshape(MLA_Q_RANK, heads, dk), ((0, 0), (0, 0), (0, hw - dk))).reshape(
        MLA_Q_RANK, heads * hw).astype(BF16)
    w_ukv_b = w_ukv.astype(BF16)
    qg_p = jnp.pad(q_g.astype(F32), (0, hw - dk)).reshape(1, hw)
    kg_p = jnp.pad(k_g.astype(F32), (0, hw - dk)).reshape(1, hw)
    tm = _pick(lt, (768, 512, 384, 256, 128))
    p = _matmul(h, w_in_p, batch=batch, col_blocks=[0], n_out=n_in_pad, tm=tm, tn=_pick(n_in_pad, (256, 128)),
                name="mla_in")
    tp = _pick(rows, (512, 256, 128))
    cq, ckv, kr = pl.pallas_call(
        _mla_norm_body, grid=(rows // tp,),
        in_specs=[pl.BlockSpec((tp, n_in_pad), lambda i: (i, 0)),
                  pl.BlockSpec((1, MLA_Q_RANK), lambda i: (0, 0)),
                  pl.BlockSpec((1, MLA_KV_RANK), lambda i: (0, 0))],
        out_specs=[pl.BlockSpec((tp, MLA_Q_RANK), lambda i: (i, 0)),
                   pl.BlockSpec((tp, MLA_KV_RANK), lambda i: (i, 0)),
                   pl.BlockSpec((tp, LANES), lambda i: (i, 0))],
        out_shape=[jax.ShapeDtypeStruct((rows, MLA_Q_RANK), BF16),
                   jax.ShapeDtypeStruct((rows, MLA_KV_RANK), BF16),
                   jax.ShapeDtypeStruct((rows, LANES), F32)],
        compiler_params=_cparams(("parallel",)), name="mla_norm")(
            p, q_a_g.reshape(1, MLA_Q_RANK), kv_a_g.reshape(1, MLA_KV_RANK))
    tn = _pick(heads * hw, (512, 256))
    q_full = _matmul(cq, w_uq_p, batch=batch, col_blocks=[0], n_out=heads * hw, tm=tm, tn=tn, name="mla_uq")
    kv = _matmul(ckv, w_ukv_b, batch=batch, col_blocks=[0], n_out=heads * hw, tm=tm, tn=tn, name="mla_ukv")
    cos, sin = _rope_tables(n_lat, n_ctx, MLA_ROPE, LANES)
    tr = _pick(math.gcd(lt, n_lat), (256, 128))
    trb = lt // tr
    hb = _pick(heads, (8, 4, 2, 1))
    q_cat, k_cat, v = pl.pallas_call(
        functools.partial(_mla_prep_body, heads=hb), grid=(rows // tr, heads // hb),
        in_specs=[pl.BlockSpec((tr, hb * hw), lambda i, j: (i, j)),
                  pl.BlockSpec((tr, hb * hw), lambda i, j: (i, j)),
                  pl.BlockSpec((tr, LANES), lambda i, j: (i, 0)),
                  pl.BlockSpec((tr, LANES), lambda i, j: (i % trb, 0)),
                  pl.BlockSpec((tr, LANES), lambda i, j: (i % trb, 0)),
                  pl.BlockSpec((1, hw), lambda i, j: (0, 0)),
                  pl.BlockSpec((1, hw), lambda i, j: (0, 0))],
        out_specs=[pl.BlockSpec((tr, hb * hw), lambda i, j: (i, j)),
                   pl.BlockSpec((tr, hb * hw), lambda i, j: (i, j)),
                   pl.BlockSpec((tr, hb * LANES), lambda i, j: (i, j))],
        out_shape=[jax.ShapeDtypeStruct((rows, heads * hw), BF16),
                   jax.ShapeDtypeStruct((rows, heads * hw), BF16),
                   jax.ShapeDtypeStruct((rows, heads * LANES), BF16)],
        compiler_params=_cparams(("parallel", "parallel")), name="mla_prep")(q_full, kv, kr, cos, sin, qg_p, kg_p)
    tq = _pick(n_lat, (512, 256, 128))
    q3 = q_cat.reshape(batch, lt, heads * hw)
    k3 = k_cat.reshape(batch, lt, heads * hw)
    v3 = v.reshape(batch, lt, heads * LANES)
    o = pl.pallas_call(
        _mla_attn_body, grid=(batch, heads, n_lat // tq),
        in_specs=[pl.BlockSpec((None, tq, hw), lambda b, hh, i: (b, i, hh)),
                  pl.BlockSpec((None, lt, hw), lambda b, hh, i: (b, 0, hh)),
                  pl.BlockSpec((None, lt, LANES), lambda b, hh, i: (b, 0, hh))],
        out_specs=pl.BlockSpec((None, tq, LANES), lambda b, hh, i: (b, i, hh)),
        out_shape=jax.ShapeDtypeStruct((batch, n_lat, heads * LANES), BF16),
        compiler_params=_cparams(("parallel", "parallel", "arbitrary")), name="mla_attn")(q3, k3, v3)
    return o.reshape(batch * n_lat, heads * LANES)


def kernel(x, c, ctx, c_ctx, mod_down, mod_up, mod_b, norm1_g, norm2_g, swa_w_in, swa_q_g, swa_k_g, swa_sinks, swa_w_out, ssd_w_in, ssd_conv_w, ssd_conv_b, ssd_dt_bias, ssd_a_log, ssd_d, ssd_norm_g, ssd_w_out, s5_w_in, s5_lam_re, s5_lam_im, s5_log_dt, s5_b_re, s5_b_im, s5_c_re, s5_c_im, s5_d, s5_w_glu, mla_w_in, mla_q_a_g, mla_kv_a_g, mla_w_uq, mla_w_ukv, mla_q_g, mla_k_g, mla_w_out, moe_w_group, moe_b_group, moe_w_expert, moe_b_expert, moe_w1, moe_w3, moe_w2):
    batch, n_lat, d = x.shape
    n_ctx = ctx.shape[1]
    lt = n_lat + n_ctx
    rows = batch * lt
    depth = mod_down.shape[0]
    dims = dict(batch=batch, lt=lt, n_lat=n_lat)
    tm = _pick(lt, (768, 512, 384, 256, 128))
    tn = _pick(d, (512, 256, 128))

    xs = jnp.concatenate([x, ctx], axis=1).reshape(rows, d)

    pad_rows = -(batch + 1) % SUBLANES
    cvecs = jnp.concatenate([c, c_ctx[None], jnp.zeros((pad_rows, d), F32)], axis=0)
    mod_all = _adaln(cvecs, mod_down, mod_up, mod_b).reshape(depth, batch + 1 + pad_rows, 6, d)

    n_moe_logits = MOE_GROUPS + moe_w_expert.shape[-1]
    router_w = jnp.pad(jnp.concatenate([moe_w_group, moe_w_expert], -1), ((0, 0), (0, 0), (0, LANES - n_moe_logits)))
    router_b = jnp.pad(jnp.concatenate([moe_b_group, moe_b_expert], -1), ((0, 0), (0, LANES - n_moe_logits)))

    def out_proj(o, w, res, mods4, name, last, epilogue="residual", cols=(0,)):
        tk = _pick(o.shape[1], (4096, 2048, 1024, 512, 256, 128))
        use_rows = n_lat if last else lt
        tm_o = _pick(n_lat, (1024, 512, 256, 128)) if last else tm
        return _matmul(o, w.astype(BF16), batch=batch, col_blocks=list(cols), n_out=d, tm=tm_o, tn=tn, tk=tk,
                       epilogue=epilogue, res=res, mods=mods4, gate_idx=2, use_rows=use_rows, n_lat=n_lat, name=name)

    for i in range(depth):
        kind, slot = i % 4, i // 4
        last = i == depth - 1
        ml = mod_all[i, :batch]
        mc = jnp.broadcast_to(mod_all[i, batch][None], (batch, 6, d))
        mods4 = jnp.stack([ml, mc], axis=1)
        mods2 = mods4.reshape(batch * 2, 6, d)
        h = _modnorm(xs, norm1_g[i], mods2, shift_idx=0, scale_idx=1, lt=lt, n_lat=n_lat, out_dtype=BF16)
        if kind == 0:
            o = _mixer_swa(h, swa_w_in[slot].astype(BF16), swa_q_g[slot], swa_k_g[slot], swa_sinks[slot], **dims)
            xs = out_proj(o, swa_w_out[slot], xs, mods4, "swa_out", last)
        elif kind == 1:
            o = _mixer_ssd(h, ssd_w_in[slot].astype(BF16), ssd_conv_w[slot], ssd_conv_b[slot], ssd_dt_bias[slot],
                           ssd_a_log[slot], ssd_d[slot], ssd_norm_g[slot], **dims)
            xs = out_proj(o, ssd_w_out[slot], xs, mods4, "ssd_out", last)
        elif kind == 2:
            o = _mixer_s5(h, s5_w_in[slot].astype(BF16), s5_lam_re[slot], s5_lam_im[slot], s5_log_dt[slot],
                          s5_b_re[slot], s5_b_im[slot], s5_c_re[slot], s5_c_im[slot], s5_d[slot], **dims)
            xs = out_proj(o, s5_w_glu[slot], xs, mods4, "s5_glu", last, epilogue="glu_residual", cols=(0, d // tn))
        else:
            if not last:
                raise NotImplementedError("the MLA mixer is implemented for the last layer only (latent queries)")
            o = _mixer_mla(h, mla_w_in[slot], mla_q_a_g[slot], mla_kv_a_g[slot], mla_w_uq[slot], mla_w_ukv[slot],
                           mla_q_g[slot], mla_k_g[slot], **dims)
            xs = out_proj(o, mla_w_out[slot], xs, mods4, "mla_out", last)
        xs = _moe_layer(xs, mods2, norm2_g[i], router_w[i], router_b[i].reshape(1, LANES), moe_w1, moe_w3, moe_w2, i,
                        lt=n_lat if last else lt, n_lat=n_lat)
    return xs.reshape(batch, n_lat, d)
```

```python
import functools
import math

import jax
import jax.numpy as jnp
from jax import lax
from jax.experimental import pallas as pl
from jax.experimental.pallas import tpu as pltpu

F32 = jnp.float32
BF16 = jnp.bfloat16
HIGHEST = lax.Precision.HIGHEST

GRID_W = 64
ROPE_BASE = 10000.0
NORM_EPS = 1e-6

SWA_HEADS = 32
SWA_KV_HEADS = 8
SWA_HEAD_DIM = 128
SWA_WINDOW = 128
SWA_BLOCK = 128

SSD_HEAD_DIM = 64
SSD_GROUPS = 8
SSD_STATE = 128
SSD_CHUNK = 128

S5_GROUP = 16
S5_STATE = 64
S5_CHUNK = 256
S5_SLAB_GROUPS = 16

MLA_HEADS = 32
MLA_Q_RANK = 1024
MLA_KV_RANK = 512
MLA_NOPE = 128
MLA_ROPE = 64
MLA_V = 128

MOE_GROUPS = 4
MOE_PER_GROUP = 8
MOE_TOPK = 2
MOE_BLOCK = 256

LANES = 128
SUBLANES = 8
VMEM_LIMIT = 56 * 1024 * 1024


def _cparams(sem, vmem=VMEM_LIMIT):
    return pltpu.CompilerParams(dimension_semantics=sem, vmem_limit_bytes=vmem)


def _pick(n, cands):
    for c in cands:
        if n % c == 0:
            return c
    raise ValueError(f"no tile in {cands} divides {n}")


def _sigmoid(x):
    return 1.0 / (1.0 + jnp.exp(-x))


def _silu(x):
    return x * _sigmoid(x)


def _softplus(x):
    return jnp.maximum(x, 0.0) + jnp.log1p(jnp.exp(-jnp.abs(x)))


def _pack_bf16_pairs(t):
    n = t.shape[1] // 2

    def rne_bits(v):
        u = lax.bitcast_convert_type(v, jnp.uint32)
        return u + jnp.uint32(0x7FFF) + ((u >> 16) & jnp.uint32(1))

    return (rne_bits(t[:, :n]) >> 16) | (rne_bits(t[:, n:]) & jnp.uint32(0xFFFF0000))


def _unpack_bf16_pairs(u):
    lo = lax.bitcast_convert_type(u << 16, F32)
    hi = lax.bitcast_convert_type(u & jnp.uint32(0xFFFF0000), F32)
    return lo, hi


def _nt_dot(a, b):
    return lax.dot_general(a, b, (((1,), (1,)), ((), ())), preferred_element_type=F32)


def _mm_body(*refs, n_w, nk, epilogue, tm, tiles_per_batch, n_lat, gate_idx):
    a_ref = refs[0]
    w_refs = refs[1:1 + n_w]
    pos = 1 + n_w
    res_ref = mod_ref = None
    if epilogue in ("residual", "glu_residual"):
        res_ref, mod_ref = refs[pos], refs[pos + 1]
        pos += 2
    o_ref = refs[pos]
    acc_refs = refs[pos + 1:]

    def finish(vals):
        val = vals[0] * _sigmoid(vals[1]) if epilogue in ("glu", "glu_residual") else vals[0]
        if res_ref is not None:
            row = (pl.program_id(0) % tiles_per_batch) * tm + lax.broadcasted_iota(jnp.int32, (tm, 1), 0)
            gate = jnp.where(row >= n_lat, mod_ref[1, pl.ds(gate_idx, 1), :], mod_ref[0, pl.ds(gate_idx, 1), :])
            val = res_ref[...] + gate * val
        o_ref[...] = val.astype(o_ref.dtype)

    a = a_ref[...]
    if nk == 1:
        finish([jnp.dot(a, w[...], preferred_element_type=F32) for w in w_refs])
        return
    k = pl.program_id(2)

    @pl.when(k == 0)
    def _():
        for acc in acc_refs:
            acc[...] = jnp.zeros_like(acc)

    for w, acc in zip(w_refs, acc_refs):
        acc[...] += jnp.dot(a, w[...], preferred_element_type=F32)

    @pl.when(k == nk - 1)
    def _():
        finish([acc[...] for acc in acc_refs])


def _matmul(a, w, *, batch, col_blocks, n_out, tm, tn, tk=None, out_dtype=F32, epilogue="store", res=None, mods=None,
            gate_idx=0, use_rows=None, n_lat=None, out_shape=None, out_map=None, name="matmul"):
    k_dim = a.shape[1]
    a3 = a.reshape(batch, a.shape[0] // batch, k_dim)
    use_rows = use_rows or a3.shape[1]
    tk = tk or k_dim
    nk = k_dim // tk
    n_w = len(col_blocks)
    tu = use_rows // tm
    grid = (batch * tu, n_out // tn, nk)
    in_specs = [pl.BlockSpec((None, tm, tk), lambda i, j, k: (i // tu, i % tu, k))]
    args = [a3]
    for off in col_blocks:
        in_specs.append(pl.BlockSpec((tk, tn), lambda i, j, k, off=off: (k, j + off)))
        args.append(w)
    if res is not None:
        in_specs.append(pl.BlockSpec((None, tm, tn), lambda i, j, k: (i // tu, i % tu, j)))
        in_specs.append(pl.BlockSpec((None, 2, 6, tn), lambda i, j, k: (i // tu, 0, 0, j)))
        args += [res.reshape(batch, res.shape[0] // batch, res.shape[1]), mods]
    out_shape = out_shape or (batch * use_rows, n_out)
    out_map = out_map or (lambda i, j, k: (i, j))
    scratch = [pltpu.VMEM((tm, tn), F32) for _ in range(n_w)] if nk > 1 else []
    body = functools.partial(_mm_body, n_w=n_w, nk=nk, epilogue=epilogue, tm=tm, tiles_per_batch=tu,
                             n_lat=n_lat, gate_idx=gate_idx)
    return pl.pallas_call(
        body, grid=grid, in_specs=in_specs, out_specs=pl.BlockSpec((tm, tn), out_map),
        out_shape=jax.ShapeDtypeStruct(out_shape, out_dtype), scratch_shapes=scratch,
        compiler_params=_cparams(("parallel", "parallel", "arbitrary")), name=name)(*args)


def _adaln_body(c_ref, wd_ref, wu_ref, b_ref, o_ref, t_ref):
    @pl.when(pl.program_id(1) == 0)
    def _():
        cv = c_ref[...]
        t_ref[...] = jnp.dot(_silu(cv), wd_ref[...], precision=HIGHEST, preferred_element_type=F32)

    o_ref[...] = jnp.dot(t_ref[...], wu_ref[...], precision=HIGHEST, preferred_element_type=F32) + b_ref[...]


def _adaln(cvecs, mod_down, mod_up, mod_b):
    depth, d, rank = mod_down.shape
    n6 = mod_up.shape[-1]
    rows = cvecs.shape[0]
    tn = _pick(n6, (2048, 1024, 512, 256, 128))
    return pl.pallas_call(
        _adaln_body, grid=(depth, n6 // tn),
        in_specs=[pl.BlockSpec((rows, d), lambda l, j: (0, 0)),
                  pl.BlockSpec((None, d, rank), lambda l, j: (l, 0, 0)),
                  pl.BlockSpec((None, rank, tn), lambda l, j: (l, 0, j)),
                  pl.BlockSpec((None, 1, tn), lambda l, j: (l, 0, j))],
        out_specs=pl.BlockSpec((None, rows, tn), lambda l, j: (l, 0, j)),
        out_shape=jax.ShapeDtypeStruct((depth, rows, n6), F32),
        scratch_shapes=[pltpu.VMEM((rows, rank), F32)],
        compiler_params=_cparams(("parallel", "arbitrary")), name="adaln")(
            cvecs, mod_down, mod_up, mod_b.reshape(depth, 1, n6))


def _modnorm_body(*refs, shift_idx, scale_idx, router):
    if router:
        x_ref, g_ref, mod_ref, wr_ref, br_ref, h_ref, ids_ref, gates_ref = refs
    else:
        x_ref, g_ref, mod_ref, h_ref = refs
    x = x_ref[...]
    xn = x * lax.rsqrt(jnp.mean(x * x, axis=-1, keepdims=True) + NORM_EPS) * g_ref[...]
    h = xn * (1.0 + mod_ref[pl.ds(scale_idx, 1), :]) + mod_ref[pl.ds(shift_idx, 1), :]
    if not router:
        h_ref[...] = h.astype(h_ref.dtype)
        return
    h_ref[...] = _pack_bf16_pairs(h)
    logits = jnp.dot(h, wr_ref[...], precision=HIGHEST, preferred_element_type=F32) + br_ref[...]
    lane = lax.broadcasted_iota(jnp.int32, logits.shape, 1)
    lane_f = lane.astype(F32)
    neg = -jnp.inf

    def first_lane(hit):
        return jnp.min(jnp.where(hit, lane_f, float(LANES)), axis=-1, keepdims=True).astype(jnp.int32)

    is_grp = lane < MOE_GROUPS
    lg = jnp.where(is_grp, logits, neg)
    mg = jnp.max(lg, axis=-1, keepdims=True)
    grp = first_lane(lg == mg)
    p_grp = 1.0 / jnp.sum(jnp.where(is_grp, jnp.exp(lg - mg), 0.0), axis=-1, keepdims=True)
    lo = MOE_GROUPS + grp * MOE_PER_GROUP
    le = jnp.where((lane >= lo) & (lane < lo + MOE_PER_GROUP), logits, neg)
    m1 = jnp.max(le, axis=-1, keepdims=True)
    i1 = first_lane(le == m1)
    le2 = jnp.where(lane == i1, neg, le)
    m2 = jnp.max(le2, axis=-1, keepdims=True)
    i2 = first_lane(le2 == m2)
    e2 = jnp.exp(m2 - m1)
    g1 = p_grp / (1.0 + e2)
    g2 = p_grp * e2 / (1.0 + e2)
    ids_ref[...] = jnp.where(lane == 0, i1 - MOE_GROUPS, jnp.where(lane == 1, i2 - MOE_GROUPS, 0))
    gates_ref[...] = jnp.where(lane == 0, g1, jnp.where(lane == 1, g2, 0.0))


def _modnorm(x, g, mods2, *, shift_idx, scale_idx, lt, n_lat, out_dtype, router_w=None, router_b=None):
    rows, d = x.shape
    tm = _pick(math.gcd(lt, n_lat), (256, 128))
    tpb = lt // tm
    lat_tiles = n_lat // tm
    router = router_w is not None
    in_specs = [pl.BlockSpec((tm, d), lambda i: (i, 0)),
                pl.BlockSpec((1, d), lambda i: (0, 0)),
                pl.BlockSpec((None, 6, d), lambda i: ((i // tpb) * 2 + ((i % tpb) >= lat_tiles).astype(jnp.int32), 0, 0))]
    args = [x, g.reshape(1, d), mods2]
    out_specs = [pl.BlockSpec((tm, d), lambda i: (i, 0))]
    out_shape = [jax.ShapeDtypeStruct((rows, d), out_dtype)]
    if router:
        out_specs = [pl.BlockSpec((tm, d // 2), lambda i: (i, 0))]
        out_shape = [jax.ShapeDtypeStruct((rows, d // 2), jnp.uint32)]
        in_specs += [pl.BlockSpec((d, LANES), lambda i: (0, 0)), pl.BlockSpec((1, LANES), lambda i: (0, 0))]
        args += [router_w, router_b]
        out_specs += [pl.BlockSpec((tm, LANES), lambda i: (i, 0))] * 2
        out_shape += [jax.ShapeDtypeStruct((rows, LANES), jnp.int32), jax.ShapeDtypeStruct((rows, LANES), F32)]
    body = functools.partial(_modnorm_body, shift_idx=shift_idx, scale_idx=scale_idx, router=router)
    out = pl.pallas_call(body, grid=(rows // tm,), in_specs=in_specs, out_specs=out_specs, out_shape=out_shape,
                         compiler_params=_cparams(("parallel",)), name="modnorm_router" if router else "modnorm")(*args)
    return out if router else out[0]


def _row_gather_start(idx_ref, base, n, src_hbm, dst, sem, dst_base=0, idx_stride=1):
    def body(r, c):
        row = idx_ref[base + r * idx_stride]
        pltpu.make_async_copy(src_hbm.at[pl.ds(row, 1)], dst.at[pl.ds(dst_base + r, 1)], sem).start()
        return c

    lax.fori_loop(0, n, body, 0, unroll=8)


def _row_gather_wait(n, src_hbm, dst, sem):
    def body(r, c):
        pltpu.make_async_copy(src_hbm.at[pl.ds(0, 1)], dst.at[pl.ds(r, 1)], sem).wait()
        return c

    lax.fori_loop(0, n, body, 0, unroll=8)


def _moe_ffn_body(src_ref, blk_e_ref, n_used_ref, h_hbm, w1_ref, w3_ref, w2_ref, o_ref,
                  xbuf, xb, w1b, w3b, w2b, sems, *, blk):
    i = pl.program_id(0)
    slot = i % 2
    n_used = n_used_ref[0]
    half = xb.shape[1] // 2

    @pl.when(i == 0)
    def _():
        _row_gather_start(src_ref, 0, blk, h_hbm, xbuf.at[0], sems.at[0])

    @pl.when(i + 1 < n_used)
    def _():
        _row_gather_start(src_ref, (i + 1) * blk, blk, h_hbm, xbuf.at[1 - slot], sems.at[1 - slot])

    @pl.when(i < n_used)
    def _():
        @pl.when(jnp.logical_or(i == 0, blk_e_ref[i] != blk_e_ref[jnp.maximum(i - 1, 0)]))
        def _():
            w1b[...] = w1_ref[...].astype(BF16)
            w3b[...] = w3_ref[...].astype(BF16)
            w2b[...] = w2_ref[...].astype(BF16)

        _row_gather_wait(blk, h_hbm, xbuf.at[slot], sems.at[slot])
        lo, hi = _unpack_bf16_pairs(xbuf[slot])
        xb[:, :half] = lo.astype(BF16)
        xb[:, half:] = hi.astype(BF16)
        x = xb[...]
        a1 = jnp.dot(x, w1b[...], preferred_element_type=F32)
        a3 = jnp.dot(x, w3b[...], preferred_element_type=F32)
        mid = (_silu(a1) * a3).astype(BF16)
        o_ref[...] = _pack_bf16_pairs(jnp.dot(mid, w2b[...], preferred_element_type=F32))

    @pl.when(i >= n_used)
    def _():
        o_ref[...] = jnp.zeros_like(o_ref)


def _moe_ffn(hpk, src, blk_e, n_used, w1, w3, w2, layer, *, blk):
    n_rows = src.shape[0]
    n_blk = n_rows // blk
    half = hpk.shape[1]
    d = 2 * half
    hid = w1.shape[-1]
    grid_spec = pltpu.PrefetchScalarGridSpec(
        num_scalar_prefetch=3, grid=(n_blk,),
        in_specs=[pl.BlockSpec(memory_space=pl.ANY),
                  pl.BlockSpec((None, None, d, hid), lambda i, s, e, n: (layer, e[i], 0, 0)),
                  pl.BlockSpec((None, None, d, hid), lambda i, s, e, n: (layer, e[i], 0, 0)),
                  pl.BlockSpec((None, None, hid, d), lambda i, s, e, n: (layer, e[i], 0, 0))],
        out_specs=pl.BlockSpec((blk, half), lambda i, s, e, n: (i, 0)),
        scratch_shapes=[pltpu.VMEM((2, blk, half), jnp.uint32), pltpu.VMEM((blk, d), BF16),
                        pltpu.VMEM((d, hid), BF16), pltpu.VMEM((d, hid), BF16), pltpu.VMEM((hid, d), BF16),
                        pltpu.SemaphoreType.DMA((2,))])
    return pl.pallas_call(
        functools.partial(_moe_ffn_body, blk=blk), grid_spec=grid_spec,
        out_shape=jax.ShapeDtypeStruct((n_rows, half), jnp.uint32),
        compiler_params=_cparams(("arbitrary",)), name="moe_ffn")(src, blk_e, n_used, hpk, w1, w3, w2)


def _moe_combine_body(pos_ref, x_ref, gates_ref, mod_ref, yb_hbm, o_ref, ybuf, sems, *, tm, n_tiles):
    i = pl.program_id(0)
    slot = i % 2
    half = x_ref.shape[1] // 2

    def start(tile, s):
        for kk in range(MOE_TOPK):
            _row_gather_start(pos_ref, tile * tm * MOE_TOPK + kk, tm, yb_hbm, ybuf.at[s], sems.at[s],
                              dst_base=kk * tm, idx_stride=MOE_TOPK)

    @pl.when(i == 0)
    def _():
        start(0, 0)

    @pl.when(i + 1 < n_tiles)
    def _():
        start(i + 1, 1 - slot)

    _row_gather_wait(MOE_TOPK * tm, yb_hbm, ybuf.at[slot], sems.at[slot])
    ys = ybuf.at[slot]
    rb, cb = 32, min(4 * LANES, half)
    for r in range(tm // rb):
        rows = pl.ds(r * rb, rb)
        g0 = jnp.broadcast_to(gates_ref[rows, 0:1], (rb, cb))
        g1 = jnp.broadcast_to(gates_ref[rows, 1:2], (rb, cb))
        for c in range(half // cb):
            lo0, hi0 = _unpack_bf16_pairs(ys[rows, c * cb:(c + 1) * cb])
            lo1, hi1 = _unpack_bf16_pairs(ys[pl.ds(tm + r * rb, rb), c * cb:(c + 1) * cb])
            for off, y in ((c * cb, g0 * lo0 + g1 * lo1), (half + c * cb, g0 * hi0 + g1 * hi1)):
                o_ref[rows, off:off + cb] = x_ref[rows, off:off + cb] + mod_ref[pl.ds(5, 1), off:off + cb] * y


def _moe_combine(x, gates, mods2, yb, pos, *, lt, n_lat):
    rows, d = x.shape
    tm = _pick(math.gcd(lt, n_lat), (256, 128))
    tpb = lt // tm
    lat_tiles = n_lat // tm
    n_tiles = rows // tm
    grid_spec = pltpu.PrefetchScalarGridSpec(
        num_scalar_prefetch=1, grid=(n_tiles,),
        in_specs=[pl.BlockSpec((tm, d), lambda i, p: (i, 0)),
                  pl.BlockSpec((tm, LANES), lambda i, p: (i, 0)),
                  pl.BlockSpec((None, 6, d),
                               lambda i, p: ((i // tpb) * 2 + ((i % tpb) >= lat_tiles).astype(jnp.int32), 0, 0)),
                  pl.BlockSpec(memory_space=pl.ANY)],
        out_specs=pl.BlockSpec((tm, d), lambda i, p: (i, 0)),
        scratch_shapes=[pltpu.VMEM((2, MOE_TOPK * tm, d // 2), jnp.uint32), pltpu.SemaphoreType.DMA((2,))])
    return pl.pallas_call(
        functools.partial(_moe_combine_body, tm=tm, n_tiles=n_tiles), grid_spec=grid_spec,
        out_shape=jax.ShapeDtypeStruct((rows, d), F32),
        compiler_params=_cparams(("arbitrary",)), name="moe_combine")(pos, x, gates, mods2, yb)


def _moe_layer(x, mods2, norm_g, router_w, router_b, w1, w3, w2, layer, *, lt, n_lat):
    rows, d = x.shape
    n_exp = w1.shape[1]
    hpk, ids, gates = _modnorm(x, norm_g, mods2, shift_idx=3, scale_idx=4, lt=lt, n_lat=n_lat, out_dtype=F32,
                               router_w=router_w, router_b=router_b)
    blk = MOE_BLOCK
    n_asg = rows * MOE_TOPK
    n_blk = n_asg // blk + n_exp
    flat = ids[:, :MOE_TOPK].reshape(-1)
    order = jnp.argsort(flat).astype(jnp.int32)
    sorted_e = flat[order]
    counts = jnp.sum((flat[:, None] == jnp.arange(n_exp, dtype=jnp.int32)[None, :]).astype(jnp.int32), axis=0)
    padded = (counts + blk - 1) // blk * blk
    pad_end = jnp.cumsum(padded)
    pad_start = pad_end - padded
    start = jnp.cumsum(counts) - counts
    dest = (pad_start[sorted_e] + jnp.arange(n_asg, dtype=jnp.int32) - start[sorted_e]).astype(jnp.int32)
    blk_first = jnp.arange(n_blk, dtype=jnp.int32) * blk
    blk_e = jnp.minimum(jnp.sum((pad_end[None, :] <= blk_first[:, None]).astype(jnp.int32), axis=1), n_exp - 1)
    n_used = (pad_end[-1:] // blk).astype(jnp.int32)
    within = jnp.arange(n_blk * blk, dtype=jnp.int32) - jnp.repeat(pad_start[blk_e], blk)
    valid = within < jnp.repeat(counts[blk_e], blk)
    sorted_idx = jnp.clip(jnp.repeat(start[blk_e], blk) + within, 0, n_asg - 1)
    src = jnp.where(valid, order[sorted_idx] // MOE_TOPK, 0).astype(jnp.int32)
    pos = dest[jnp.argsort(order)]
    yb = _moe_ffn(hpk, src, blk_e.astype(jnp.int32), n_used, w1, w3, w2, layer, blk=blk)
    return _moe_combine(x, gates, mods2, yb, pos.astype(jnp.int32), lt=lt, n_lat=n_lat)


def _rope_tables(n_lat, n_ctx, rot_dim, width):
    rows = n_lat // GRID_W
    row = jnp.repeat(jnp.arange(rows, dtype=F32), GRID_W)
    col = jnp.tile(jnp.arange(GRID_W, dtype=F32), rows)
    axis_dim = rot_dim // 2
    inv_freq = ROPE_BASE ** (-jnp.arange(0, axis_dim, 2, dtype=F32) / axis_dim)
    a0 = row[:, None] * inv_freq
    a1 = col[:, None] * inv_freq
    cos = jnp.concatenate([jnp.cos(a0), jnp.cos(a0), jnp.cos(a1), jnp.cos(a1)], -1)
    sin = jnp.concatenate([-jnp.sin(a0), jnp.sin(a0), -jnp.sin(a1), jnp.sin(a1)], -1)
    cos = jnp.pad(cos, ((0, n_ctx), (0, width - rot_dim)), constant_values=1.0)
    sin = jnp.pad(sin, ((0, n_ctx), (0, width - rot_dim)))
    return cos, sin


def _rope(x, cos, sin, half):
    lane = lax.broadcasted_iota(jnp.int32, x.shape, 1)
    partner = jnp.where((lane % (2 * half)) < half, pltpu.roll(x, LANES - half, 1), pltpu.roll(x, half, 1))
    return x * cos + partner * sin


def _swa_prep_body(p_ref, cos_ref, sin_ref, qg_ref, kg_ref, q_ref, k_ref, v_ref, *, n_q, n_kv):
    hd = SWA_HEAD_DIM
    cos, sin = cos_ref[...], sin_ref[...]
    scale = hd ** -0.5

    def norm_rope(t, g):
        t = t * lax.rsqrt(jnp.mean(t * t, axis=-1, keepdims=True) + NORM_EPS) * g
        return _rope(t, cos, sin, hd // 4)

    for h in range(n_q):
        q_ref[:, h * hd:(h + 1) * hd] = (norm_rope(p_ref[:, h * hd:(h + 1) * hd], qg_ref[...]) * scale).astype(BF16)
    for h in range(n_kv):
        c0 = (n_q + h) * hd
        k_ref[:, h * hd:(h + 1) * hd] = norm_rope(p_ref[:, c0:c0 + hd], kg_ref[...]).astype(BF16)
    v_ref[...] = p_ref[:, (n_q + n_kv) * hd:].astype(BF16)


def _swa_attn_body(sink_ref, q_ref, *refs, n_lat, grp, nq):
    nw = nq + 2
    k_refs, v_refs, o_ref = refs[:nw + 1], refs[nw + 1:2 * nw + 2], refs[2 * nw + 2]
    kv = pl.program_id(1)
    first = pl.program_id(2) * nq
    blk = SWA_BLOCK
    hd = SWA_HEAD_DIM
    n_ctx = k_refs[nw].shape[0]
    is_lat = first < n_lat // blk
    qpos = first * blk + lax.broadcasted_iota(jnp.int32, (nq * blk, 1), 0)
    biases = []
    for w in range(nw):
        kpos = (first - 1 + w) * blk + lax.broadcasted_iota(jnp.int32, (1, blk), 1)
        valid = (jnp.abs(kpos - qpos) <= SWA_WINDOW) & (kpos >= 0) & (kpos < n_lat) & is_lat
        biases.append(jnp.where(valid, 0.0, -jnp.inf))
    biases.append(jnp.zeros((nq * blk, n_ctx), F32))
    bias = jnp.concatenate(biases, axis=1)
    k_all = jnp.concatenate([r[...] for r in k_refs], axis=0)
    v_all = jnp.concatenate([r[...] for r in v_refs], axis=0)
    for g in range(grp):
        sink = sink_ref[kv * grp + g]
        s = _nt_dot(q_ref[:, g * hd:(g + 1) * hd], k_all) + bias
        m = jnp.maximum(jnp.max(s, axis=-1, keepdims=True), sink)
        p = jnp.exp(s - m)
        denom = jnp.sum(p, axis=-1, keepdims=True) + jnp.exp(sink - m)
        o = jnp.dot(p.astype(BF16), v_all, preferred_element_type=F32) / denom
        o_ref[:, g * hd:(g + 1) * hd] = o.astype(o_ref.dtype)


def _mixer_swa(h, w_in, q_g, k_g, sinks, *, batch, lt, n_lat):
    rows = h.shape[0]
    hd = SWA_HEAD_DIM
    n_q, n_kv = SWA_HEADS, SWA_KV_HEADS
    grp = n_q // n_kv
    n_ctx = lt - n_lat
    n_cols = (n_q + 2 * n_kv) * hd
    tm = _pick(lt, (768, 512, 384, 256, 128))
    tn = _pick(n_cols, (512, 256, 128))
    p = _matmul(h, w_in, batch=batch, col_blocks=[0], n_out=n_cols, tm=tm, tn=tn, name="swa_in")
    cos, sin = _rope_tables(n_lat, n_ctx, hd, hd)
    tp = _pick(math.gcd(lt, n_lat), (256, 128))
    tpb = lt // tp
    q, k, v = pl.pallas_call(
        functools.partial(_swa_prep_body, n_q=n_q, n_kv=n_kv), grid=(rows // tp,),
        in_specs=[pl.BlockSpec((tp, n_cols), lambda i: (i, 0)),
                  pl.BlockSpec((tp, hd), lambda i: (i % tpb, 0)),
                  pl.BlockSpec((tp, hd), lambda i: (i % tpb, 0)),
                  pl.BlockSpec((1, hd), lambda i: (0, 0)),
                  pl.BlockSpec((1, hd), lambda i: (0, 0))],
        out_specs=[pl.BlockSpec((tp, n_q * hd), lambda i: (i, 0)),
                   pl.BlockSpec((tp, n_kv * hd), lambda i: (i, 0)),
                   pl.BlockSpec((tp, n_kv * hd), lambda i: (i, 0))],
        out_shape=[jax.ShapeDtypeStruct((rows, n_q * hd), BF16),
                   jax.ShapeDtypeStruct((rows, n_kv * hd), BF16),
                   jax.ShapeDtypeStruct((rows, n_kv * hd), BF16)],
        compiler_params=_cparams(("parallel",)), name="swa_prep")(p, cos, sin, q_g.reshape(1, hd), k_g.reshape(1, hd))

    blk = SWA_BLOCK
    bpb = lt // blk
    lat_blocks = n_lat // blk

    nq = 2
    steps = bpb // nq

    def win(off):
        return lambda b, kvh, j: (b * bpb + jnp.clip(j * nq + off, 0, lat_blocks - 1), kvh)

    ctx_spec = pl.BlockSpec((None, n_ctx, hd), lambda b, kvh, j: (b, n_lat // n_ctx, kvh))
    k3 = k.reshape(batch, lt, n_kv * hd)
    v3 = v.reshape(batch, lt, n_kv * hd)
    kv_spec = [pl.BlockSpec((blk, hd), win(off)) for off in range(-1, nq + 1)]
    q_spec = pl.BlockSpec((nq * blk, grp * hd), lambda b, kvh, j: (b * steps + j, kvh))
    n_win = len(kv_spec)
    return pl.pallas_call(
        functools.partial(_swa_attn_body, n_lat=n_lat, grp=grp, nq=nq), grid=(batch, n_kv, steps),
        in_specs=[pl.BlockSpec(memory_space=pltpu.SMEM), q_spec] + kv_spec + [ctx_spec] + kv_spec + [ctx_spec],
        out_specs=q_spec,
        out_shape=jax.ShapeDtypeStruct((rows, n_q * hd), BF16),
        compiler_params=_cparams(("parallel", "parallel", "arbitrary")), name="swa_attn")(
            sinks.astype(F32), q, *([k] * n_win), k3, *([v] * n_win), v3)


def _conv_silu_body(x_ref, prev_ref, next_ref, w_ref, b_ref, o_ref, *, lat_tiles, n_tiles):
    t = pl.program_id(1)
    x = x_ref[...]
    rows = x.shape[0]
    has_prev = jnp.logical_and(t != 0, t != lat_tiles).astype(F32)
    has_next = jnp.logical_and(t != lat_tiles - 1, t != n_tiles - 1).astype(F32)
    row = lax.broadcasted_iota(jnp.int32, x.shape, 0)
    x_m = jnp.where(row == 0, prev_ref[SUBLANES - 1:SUBLANES, :] * has_prev, pltpu.roll(x, 1, 0))
    x_p = jnp.where(row == rows - 1, next_ref[0:1, :] * has_next, pltpu.roll(x, rows - 1, 0))
    y = w_ref[0:1, :] * x_m + w_ref[1:2, :] * x + w_ref[2:3, :] * x_p + b_ref[...]
    o_ref[...] = _silu(y).astype(o_ref.dtype)


def _conv_silu(xbc, conv_w, conv_b, *, batch, lt, n_lat):
    cols = xbc.shape[-1]
    x3 = xbc.reshape(batch, lt, cols)
    tt = _pick(math.gcd(lt, n_lat), (256, 128))
    tc = _pick(cols, (2048, 1024, 512, 256, 128))
    n_tiles = lt // tt
    sub = tt // SUBLANES
    out = pl.pallas_call(
        functools.partial(_conv_silu_body, lat_tiles=n_lat // tt, n_tiles=n_tiles),
        grid=(batch, n_tiles, cols // tc),
        in_specs=[pl.BlockSpec((None, tt, tc), lambda b, t, j: (b, t, j)),
                  pl.BlockSpec((None, SUBLANES, tc), lambda b, t, j: (b, jnp.maximum(t * sub - 1, 0), j)),
                  pl.BlockSpec((None, SUBLANES, tc), lambda b, t, j: (b, jnp.minimum((t + 1) * sub, lt // SUBLANES - 1), j)),
                  pl.BlockSpec((3, tc), lambda b, t, j: (0, j)),
                  pl.BlockSpec((1, tc), lambda b, t, j: (0, j))],
        out_specs=pl.BlockSpec((None, tt, tc), lambda b, t, j: (b, t, j)),
        out_shape=jax.ShapeDtypeStruct((batch, lt, cols), BF16),
        compiler_params=_cparams(("parallel", "parallel", "parallel")), name="ssd_conv")(
            x3, x3, x3, conv_w, conv_b.reshape(1, cols))
    return out


def _ssd_pass_body(*refs, direction, hg, final):
    if final:
        x_ref, b_ref, c_ref, dtt_ref, biast_ref, alogt_ref, y0_ref, z_ref, ng_ref, o_ref, state_ref, g_ref = refs
    else:
        x_ref, b_ref, c_ref, dtt_ref, biast_ref, alogt_ref, dsk_ref, o_ref, state_ref = refs
    t_len = x_ref.shape[0]
    n_state = b_ref.shape[1]

    @pl.when(pl.program_id(2) == 0)
    def _():
        state_ref[...] = jnp.zeros_like(state_ref)

    bm = b_ref[...]
    cm = c_ref[...]
    dt_t = _softplus(dtt_ref[...] + biast_ref[...])
    v_t = dt_t * (-jnp.exp(alogt_ref[...]))
    row = lax.broadcasted_iota(jnp.int32, (t_len, t_len), 0)
    col = lax.broadcasted_iota(jnp.int32, (t_len, t_len), 1)
    tri = (col <= row) if direction == 0 else (col >= row)
    cum_t = lax.dot_general(v_t, tri.astype(F32), (((1,), (1,)), ((), ())), precision=HIGHEST,
                            preferred_element_type=F32)
    total_t = jnp.sum(v_t, axis=1, keepdims=True)
    w_t = dt_t * jnp.exp(total_t - cum_t)
    cb = _nt_dot(cm, bm)
    b_t = bm.astype(F32).T
    left = lax.broadcasted_iota(jnp.int32, (t_len, LANES), 1) < SSD_HEAD_DIM
    left_n = lax.broadcasted_iota(jnp.int32, (n_state, LANES), 1) < SSD_HEAD_DIM

    def split3(t):
        hi = t.astype(BF16).astype(F32)
        mid = (t - hi).astype(BF16).astype(F32)
        return hi, mid, (t - hi - mid).astype(BF16).astype(F32)

    def spread(width):
        hrow = lax.broadcasted_iota(jnp.int32, (hg, hg * width), 0)
        hcol = lax.broadcasted_iota(jnp.int32, (hg, hg * width), 1)
        return jnp.logical_and(hcol >= hrow * width, hcol < (hrow + 1) * width).astype(F32)

    def tn_dot(a_rows, w_rows):
        pad = LANES - a_rows.shape[0]
        a = jnp.concatenate([a_rows, jnp.zeros((pad, a_rows.shape[1]), F32)], axis=0)
        w = jnp.concatenate([w_rows, jnp.zeros((pad, w_rows.shape[1]), F32)], axis=0)
        a_cols = jnp.concatenate([a[:, j * LANES:(j + 1) * LANES].T for j in range(a.shape[1] // LANES)], axis=0)
        return jnp.dot(a_cols.astype(BF16), w.astype(BF16), preferred_element_type=F32)

    e_t = spread(t_len)
    cum_pieces = split3(cum_t)
    seg_all = tn_dot(
        jnp.concatenate(list(cum_pieces) + [jnp.ones((3 * hg, t_len), F32)], axis=0),
        jnp.concatenate([e_t] * 3 + [-jnp.tile(p, (1, hg)) * e_t for p in cum_pieces], axis=0))
    e_x = spread(SSD_HEAD_DIM)
    cols = jnp.concatenate([jnp.exp(cum_t), jnp.broadcast_to(jnp.exp(total_t), (hg, LANES))], axis=1)
    cols_x = tn_dot(jnp.concatenate(split3(cols), axis=0), jnp.concatenate([e_x] * 3, axis=0))
    exp_cum_x = cols_x[:t_len]
    exp_total_x = cols_x[t_len:t_len + 1]

    for pr in range(hg // 2):
        h0 = 2 * pr
        sl = slice(pr * LANES, (pr + 1) * LANES)
        xb = x_ref[:, sl]
        ys, ss = [], []
        for hh in (h0, h0 + 1):
            dec = jnp.exp(jnp.where(tri, seg_all[:, hh * t_len:(hh + 1) * t_len], -jnp.inf))
            ys.append(jnp.dot((cb * dec * dt_t[hh:hh + 1, :]).astype(BF16), xb, preferred_element_type=F32))
            ss.append(jnp.dot((b_t * w_t[hh:hh + 1, :]).astype(BF16), xb, preferred_element_type=F32))
        st = state_ref[:, sl]
        y = jnp.where(left, ys[0], ys[1]) + jnp.dot(cm, st.astype(BF16), preferred_element_type=F32) * exp_cum_x[:, sl]
        state_ref[:, sl] = st * exp_total_x[:, sl] + jnp.where(left_n, ss[0], ss[1])
        if final:
            z = z_ref[:, sl]
            g_ref[:, sl] = (y0_ref[:, sl] + y) * _silu(z)
        else:
            o_ref[:, sl] = dsk_ref[:, sl] * xb.astype(F32) + y
    if final:
        g = g_ref[...]
        o_ref[...] = (g * lax.rsqrt(jnp.mean(g * g, axis=-1, keepdims=True) + NORM_EPS) * ng_ref[...]).astype(o_ref.dtype)


def _chunk_order(i, direction, lat_chunks, ctx_chunks):
    if direction == 0:
        return jnp.where(i < ctx_chunks, lat_chunks + i, i - ctx_chunks)
    return lat_chunks + ctx_chunks - 1 - i


def _ssd_pass(direction, xbc_s, dt_t, bias_t, alog_t, extra, *, batch, lt, n_lat, d_inner, final):
    groups = SSD_GROUPS
    gw = d_inner // groups
    hg = gw // SSD_HEAD_DIM
    n_state = SSD_STATE
    t_len = SSD_CHUNK
    lat_chunks, ctx_chunks = n_lat // t_len, (lt - n_lat) // t_len
    n_chunks = lat_chunks + ctx_chunks
    xb = d_inner // n_state

    def cidx(i):
        return _chunk_order(i, direction, lat_chunks, ctx_chunks)

    in_specs = [
        pl.BlockSpec((None, t_len, gw), lambda b, g, i: (b, cidx(i), g)),
        pl.BlockSpec((None, t_len, n_state), lambda b, g, i: (b, cidx(i), xb + g)),
        pl.BlockSpec((None, t_len, n_state), lambda b, g, i: (b, cidx(i), xb + groups + g)),
        pl.BlockSpec((None, None, None, hg, t_len), lambda b, g, i: (direction, g, b, 0, cidx(i))),
        pl.BlockSpec((None, None, hg, 1), lambda b, g, i: (direction, g, 0, 0)),
        pl.BlockSpec((None, None, hg, 1), lambda b, g, i: (direction, g, 0, 0)),
    ]
    args = [xbc_s, xbc_s, xbc_s, dt_t, bias_t, alog_t]
    row_spec = pl.BlockSpec((None, t_len, gw), lambda b, g, i: (b, cidx(i), g))
    vec_spec = pl.BlockSpec((1, gw), lambda b, g, i: (0, g))
    scratch = [pltpu.VMEM((n_state, gw), F32)]
    if final:
        y0, z, norm_g = extra
        in_specs += [row_spec, row_spec, vec_spec]
        args += [y0, z, norm_g]
        out_dtype = BF16
        scratch.append(pltpu.VMEM((t_len, gw), F32))
    else:
        (dsk,) = extra
        in_specs += [vec_spec]
        args += [dsk]
        out_dtype = F32
    return pl.pallas_call(
        functools.partial(_ssd_pass_body, direction=direction, hg=hg, final=final),
        grid=(batch, groups, n_chunks), in_specs=in_specs, out_specs=row_spec,
        out_shape=jax.ShapeDtypeStruct((batch, lt, d_inner), out_dtype), scratch_shapes=scratch,
        compiler_params=_cparams(("parallel", "parallel", "arbitrary")), name=f"ssd_pass{direction}")(*args)


def _mixer_ssd(h, w_in, conv_w, conv_b, dt_bias, a_log, d_skip, norm_g, *, batch, lt, n_lat):
    rows = h.shape[0]
    groups = SSD_GROUPS
    n_heads = dt_bias.shape[-1]
    d_inner = n_heads * SSD_HEAD_DIM
    hg = n_heads // groups
    gn = groups * SSD_STATE
    tm = _pick(lt, (768, 512, 384, 256, 128))
    tn = _pick(math.gcd(d_inner, 2 * gn), (512, 256, 128))
    z = _matmul(h, w_in, batch=batch, col_blocks=[0], n_out=d_inner, tm=tm, tn=tn, name="ssd_in_z")
    xbc = _matmul(h, w_in, batch=batch, col_blocks=[d_inner // tn], n_out=d_inner + 2 * gn, tm=tm, tn=tn,
                  name="ssd_in_xbc")
    tdt = _pick(2 * n_heads, (256, 128, 64, 32))
    dt = _matmul(h, w_in, batch=batch, col_blocks=[(2 * d_inner + 2 * gn) // tdt], n_out=2 * n_heads, tm=tm, tn=tdt,
                 name="ssd_in_dt")
    xbc_s = _conv_silu(xbc, conv_w, conv_b, batch=batch, lt=lt, n_lat=n_lat)
    dt_t = dt.reshape(batch, lt, 2, groups, hg).transpose(2, 3, 0, 4, 1)
    bias_t = dt_bias.astype(F32).reshape(2, groups, hg, 1)
    alog_t = a_log.astype(F32).reshape(2, groups, hg, 1)
    dsk = jnp.repeat(d_skip.astype(F32), SSD_HEAD_DIM).reshape(1, d_inner)
    kw = dict(batch=batch, lt=lt, n_lat=n_lat, d_inner=d_inner)
    y0 = _ssd_pass(0, xbc_s, dt_t, bias_t, alog_t, (dsk,), final=False, **kw)
    g = _ssd_pass(1, xbc_s, dt_t, bias_t, alog_t,
                  (y0, z.reshape(batch, lt, d_inner), norm_g.reshape(1, d_inner)), final=True, **kw)
    return g.reshape(rows, d_inner)


def _s5_scan_body(u_ref, bbd_ref, cbd_ref, a_ref, y_ref, xs_ref, s_ref):
    d = pl.program_id(0)
    t_len, n_b, width = u_ref.shape
    ns = a_ref.shape[-1] // 2

    @pl.when(pl.program_id(2) == 0)
    def _():
        s_ref[...] = jnp.zeros_like(s_ref)

    tb = 32
    for r in range(t_len // tb):
        u = u_ref[r * tb:(r + 1) * tb].reshape(tb * n_b, width).astype(BF16)
        xs_ref[r * tb * n_b:(r + 1) * tb * n_b, :] = jnp.dot(u, bbd_ref[...], preferred_element_type=F32)
    ar = a_ref[:, :ns]
    ai = a_ref[:, ns:]

    def step(k, carry):
        sr, si = carry
        t = jnp.where(d == 0, k, t_len - 1 - k)
        r0 = pl.multiple_of(t * n_b, n_b)
        xr = xs_ref[pl.ds(r0, n_b), :ns]
        xi = xs_ref[pl.ds(r0, n_b), ns:]
        nr = ar * sr - ai * si + xr
        ni = ar * si + ai * sr + xi
        xs_ref[pl.ds(r0, n_b), :ns] = nr
        xs_ref[pl.ds(r0, n_b), ns:] = ni
        return nr, ni

    sr, si = lax.fori_loop(0, t_len, step, (s_ref[:, :ns], s_ref[:, ns:]), unroll=2)
    s_ref[:, :ns] = sr
    s_ref[:, ns:] = si
    for r in range(t_len // tb):
        y = jnp.dot(xs_ref[r * tb * n_b:(r + 1) * tb * n_b, :].astype(BF16), cbd_ref[...], preferred_element_type=F32)
        y_ref[r * tb:(r + 1) * tb] = y.reshape(tb, n_b, width)


def _s5_merge_body(u_ref, y0_ref, y1_ref, dsk_ref, o_ref, g_ref):
    tt, n_b, tc = u_ref.shape
    y = dsk_ref[...] * u_ref[...] + y0_ref[...] + y1_ref[...]
    g = 0.5 * y * (1.0 + jnp.tanh(math.sqrt(2.0 / math.pi) * (y + 0.044715 * (y * y * y))))
    g = g.reshape(tt * n_b, tc)
    for cc in range(tc // LANES):
        g_ref[cc] = g[:, cc * LANES:(cc + 1) * LANES]
    for b in range(n_b):
        for cc in range(tc // LANES):
            o_ref[b, :, cc * LANES:(cc + 1) * LANES] = g_ref[cc, pl.ds(b, tt, stride=n_b), :].astype(o_ref.dtype)


def _s5_params(lam_re, lam_im, log_dt, b_re, b_im, c_re, c_im, n_b):
    n_groups, n_state, gsz = b_re.shape
    sg = S5_SLAB_GROUPS
    n_slab = n_groups // sg
    eye = jnp.eye(sg, dtype=F32)
    bbds, a_s = [], []
    br, bi = b_re.astype(F32), b_im.astype(F32)
    for d in range(2):
        lr = lam_re[d].astype(F32)
        li = lam_im[d].astype(F32)
        step = jnp.exp(log_dt[d].astype(F32))[:, None]
        mag = jnp.exp(lr * step)
        ar, ai = mag * jnp.cos(li * step), mag * jnp.sin(li * step)
        den = lr * lr + li * li
        fr = ((ar - 1.0) * lr + ai * li) / den
        fi = (ai * lr - (ar - 1.0) * li) / den
        bbr = fr[..., None] * br - fi[..., None] * bi
        bbi = fr[..., None] * bi + fi[..., None] * br

        def bdiag_in(m):
            m = m.reshape(n_slab, sg, n_state, gsz)
            return jnp.einsum("sgpc,gh->sgchp", m, eye).reshape(n_slab, sg * gsz, sg * n_state)

        bbds.append(jnp.concatenate([bdiag_in(bbr), bdiag_in(bbi)], -1))
        a_cat = jnp.concatenate([ar.reshape(n_slab, sg * n_state), ai.reshape(n_slab, sg * n_state)], -1)
        a_s.append(jnp.broadcast_to(a_cat[:, None, :], (n_slab, n_b, 2 * sg * n_state)))

    def bdiag_out(m):
        m = m.reshape(n_slab, sg, gsz, n_state)
        return jnp.einsum("sgcp,gh->sgphc", m, eye).reshape(n_slab, sg * n_state, sg * gsz)

    cbd = jnp.concatenate([bdiag_out(c_re.astype(F32)), -bdiag_out(c_im.astype(F32))], 1)
    return jnp.stack(bbds).astype(BF16), cbd.astype(BF16), jnp.stack(a_s)


def _mixer_s5(h, w_in, lam_re, lam_im, log_dt, b_re, b_im, c_re, c_im, d_skip, *, batch, lt, n_lat):
    rows, d = h.shape
    width = w_in.shape[1]
    tm = _pick(lt, (768, 512, 384, 256, 128))
    tn = _pick(width, (512, 256, 128))
    tpb = lt // tm
    ncb = width // tn
    u = _matmul(h, w_in, batch=batch, col_blocks=[0], n_out=width, tm=tm, tn=tn, out_shape=(lt, batch * width),
                out_map=lambda i, j, k: (i % tpb, (i // tpb) * ncb + j), name="s5_in")
    bbd, cbd, a_bc = _s5_params(lam_re, lam_im, log_dt, b_re, b_im, c_re, c_im, batch)
    n_slab, sw = bbd.shape[1], bbd.shape[2]
    ns2 = bbd.shape[3]
    t_len = S5_CHUNK
    lat_chunks, ctx_chunks = n_lat // t_len, (lt - n_lat) // t_len
    n_chunks = lat_chunks + ctx_chunks

    def cidx(dd, i):
        fwd = jnp.where(i < ctx_chunks, lat_chunks + i, i - ctx_chunks)
        return jnp.where(dd == 0, fwd, n_chunks - 1 - i)

    u3 = u.reshape(lt, batch, width)
    ys = pl.pallas_call(
        _s5_scan_body, grid=(2, n_slab, n_chunks),
        in_specs=[pl.BlockSpec((t_len, batch, sw), lambda dd, s, i: (cidx(dd, i), 0, s)),
                  pl.BlockSpec((None, None, sw, ns2), lambda dd, s, i: (dd, s, 0, 0)),
                  pl.BlockSpec((None, ns2, sw), lambda dd, s, i: (s, 0, 0)),
                  pl.BlockSpec((None, None, batch, ns2), lambda dd, s, i: (dd, s, 0, 0))],
        out_specs=pl.BlockSpec((None, t_len, batch, sw), lambda dd, s, i: (dd, cidx(dd, i), 0, s)),
        out_shape=jax.ShapeDtypeStruct((2, lt, batch, width), F32),
        scratch_shapes=[pltpu.VMEM((t_len * batch, ns2), F32), pltpu.VMEM((batch, ns2), F32)],
        compiler_params=_cparams(("parallel", "parallel", "arbitrary")), name="s5_scan")(u3, bbd, cbd, a_bc)
    tt = _pick(lt, (128,))
    tc = _pick(width, (1024, 512, 256, 128))
    g = pl.pallas_call(
        _s5_merge_body, grid=(lt // tt, width // tc),
        in_specs=[pl.BlockSpec((tt, batch, tc), lambda t, j: (t, 0, j)),
                  pl.BlockSpec((None, tt, batch, tc), lambda t, j: (0, t, 0, j)),
                  pl.BlockSpec((None, tt, batch, tc), lambda t, j: (1, t, 0, j)),
                  pl.BlockSpec((1, 1, tc), lambda t, j: (0, 0, j))],
        out_specs=pl.BlockSpec((batch, tt, tc), lambda t, j: (0, t, j)),
        out_shape=jax.ShapeDtypeStruct((batch, lt, width), BF16),
        scratch_shapes=[pltpu.VMEM((tc // LANES, tt * batch, LANES), F32)],
        compiler_params=_cparams(("parallel", "parallel")), name="s5_merge")(
            u3, ys, ys, d_skip.astype(F32).reshape(1, 1, width))
    return g.reshape(rows, width)


def _mla_norm_body(p_ref, qg_ref, kvg_ref, cq_ref, ckv_ref, kr_ref):
    def rms(t, g):
        return t * lax.rsqrt(jnp.mean(t * t, axis=-1, keepdims=True) + NORM_EPS) * g

    cq_ref[...] = rms(p_ref[:, :MLA_Q_RANK], qg_ref[...]).astype(cq_ref.dtype)
    ckv_ref[...] = rms(p_ref[:, MLA_Q_RANK:MLA_Q_RANK + MLA_KV_RANK], kvg_ref[...]).astype(ckv_ref.dtype)
    kr_ref[...] = p_ref[:, MLA_Q_RANK + MLA_KV_RANK:MLA_Q_RANK + MLA_KV_RANK + LANES]


def _mla_prep_body(q_ref, kv_ref, kr_ref, cos_ref, sin_ref, qg_ref, kg_ref, qo_ref, ko_ref, vo_ref, *, heads):
    dk = MLA_NOPE + MLA_ROPE
    hw = 2 * LANES
    scale = dk ** -0.5
    cos, sin = cos_ref[...], sin_ref[...]
    kr = kr_ref[...]
    kr_ss = jnp.sum(kr * kr, axis=-1, keepdims=True)
    qg_n, qg_r = qg_ref[:, :LANES], qg_ref[:, LANES:]
    kg_n, kg_r = kg_ref[:, :LANES], kg_ref[:, LANES:]
    for h in range(heads):
        c0 = h * hw
        qn = q_ref[:, c0:c0 + LANES]
        qr = q_ref[:, c0 + LANES:c0 + hw]
        rinv = lax.rsqrt((jnp.sum(qn * qn, axis=-1, keepdims=True) + jnp.sum(qr * qr, axis=-1, keepdims=True)) / dk
                         + NORM_EPS)
        qo_ref[:, c0:c0 + LANES] = (qn * rinv * qg_n * scale).astype(BF16)
        qo_ref[:, c0 + LANES:c0 + hw] = (_rope(qr * rinv * qg_r, cos, sin, MLA_ROPE // 4) * scale).astype(BF16)
        kn = kv_ref[:, c0:c0 + LANES]
        rinv = lax.rsqrt((jnp.sum(kn * kn, axis=-1, keepdims=True) + kr_ss) / dk + NORM_EPS)
        ko_ref[:, c0:c0 + LANES] = (kn * rinv * kg_n).astype(BF16)
        ko_ref[:, c0 + LANES:c0 + hw] = _rope(kr * rinv * kg_r, cos, sin, MLA_ROPE // 4).astype(BF16)
        vo_ref[:, h * LANES:(h + 1) * LANES] = kv_ref[:, c0 + LANES:c0 + hw].astype(BF16)


def _mla_attn_body(q_ref, k_ref, v_ref, o_ref, *, sub):
    k = k_ref[...]
    v = v_ref[...]
    for qi in range(q_ref.shape[0] // sub):
        sl = slice(qi * sub, (qi + 1) * sub)
        s = _nt_dot(q_ref[sl, :], k)
        m = jnp.max(s, axis=-1, keepdims=True)
        p = jnp.exp(s - m)
        denom = jnp.sum(p, axis=-1, keepdims=True)
        o_ref[sl, :] = (jnp.dot(p.astype(BF16), v, preferred_element_type=F32) / denom).astype(o_ref.dtype)


def _mixer_mla(h, w_in, q_a_g, kv_a_g, w_uq, w_ukv, q_g, k_g, *, batch, lt, n_lat):
    rows, d = h.shape
    heads = MLA_HEADS
    dk = MLA_NOPE + MLA_ROPE
    hw = 2 * LANES
    n_ctx = lt - n_lat
    n_in = MLA_Q_RANK + MLA_KV_RANK + MLA_ROPE
    n_in_pad = MLA_Q_RANK + MLA_KV_RANK + 2 * LANES
    w_in_p = jnp.pad(w_in, ((0, 0), (0, n_in_pad - n_in))).astype(BF16)
    w_uq_p = jnp.pad(w_uq.reshape(MLA_Q_RANK, heads, dk), ((0, 0), (0, 0), (0, hw - dk))).reshape(
        MLA_Q_RANK, heads * hw).astype(BF16)
    w_ukv_b = w_ukv.astype(BF16)
    qg_p = jnp.pad(q_g.astype(F32), (0, hw - dk)).reshape(1, hw)
    kg_p = jnp.pad(k_g.astype(F32), (0, hw - dk)).reshape(1, hw)
    tm = _pick(lt, (768, 512, 384, 256, 128))
    p = _matmul(h, w_in_p, batch=batch, col_blocks=[0], n_out=n_in_pad, tm=tm, tn=_pick(n_in_pad, (256, 128)),
                name="mla_in")
    tp = _pick(rows, (512, 256, 128))
    cq, ckv, kr = pl.pallas_call(
        _mla_norm_body, grid=(rows // tp,),
        in_specs=[pl.BlockSpec((tp, n_in_pad), lambda i: (i, 0)),
                  pl.BlockSpec((1, MLA_Q_RANK), lambda i: (0, 0)),
                  pl.BlockSpec((1, MLA_KV_RANK), lambda i: (0, 0))],
        out_specs=[pl.BlockSpec((tp, MLA_Q_RANK), lambda i: (i, 0)),
                   pl.BlockSpec((tp, MLA_KV_RANK), lambda i: (i, 0)),
                   pl.BlockSpec((tp, LANES), lambda i: (i, 0))],
        out_shape=[jax.ShapeDtypeStruct((rows, MLA_Q_RANK), BF16),
                   jax.ShapeDtypeStruct((rows, MLA_KV_RANK), BF16),
                   jax.ShapeDtypeStruct((rows, LANES), F32)],
        compiler_params=_cparams(("parallel",)), name="mla_norm")(
            p, q_a_g.reshape(1, MLA_Q_RANK), kv_a_g.reshape(1, MLA_KV_RANK))
    tn = _pick(heads * hw, (512, 256))
    q_full = _matmul(cq, w_uq_p, batch=batch, col_blocks=[0], n_out=heads * hw, tm=tm, tn=tn, name="mla_uq")
    kv = _matmul(ckv, w_ukv_b, batch=batch, col_blocks=[0], n_out=heads * hw, tm=tm, tn=tn, name="mla_ukv")
    cos, sin = _rope_tables(n_lat, n_ctx, MLA_ROPE, LANES)
    tr = _pick(math.gcd(lt, n_lat), (256, 128))
    trb = lt // tr
    hb = _pick(heads, (8, 4, 2, 1))
    q_cat, k_cat, v = pl.pallas_call(
        functools.partial(_mla_prep_body, heads=hb), grid=(rows // tr, heads // hb),
        in_specs=[pl.BlockSpec((tr, hb * hw), lambda i, j: (i, j)),
                  pl.BlockSpec((tr, hb * hw), lambda i, j: (i, j)),
                  pl.BlockSpec((tr, LANES), lambda i, j: (i, 0)),
                  pl.BlockSpec((tr, LANES), lambda i, j: (i % trb, 0)),
                  pl.BlockSpec((tr, LANES), lambda i, j: (i % trb, 0)),
                  pl.BlockSpec((1, hw), lambda i, j: (0, 0)),
                  pl.BlockSpec((1, hw), lambda i, j: (0, 0))],
        out_specs=[pl.BlockSpec((tr, hb * hw), lambda i, j: (i, j)),
                   pl.BlockSpec((tr, hb * hw), lambda i, j: (i, j)),
                   pl.BlockSpec((tr, hb * LANES), lambda i, j: (i, j))],
        out_shape=[jax.ShapeDtypeStruct((rows, heads * hw), BF16),
                   jax.ShapeDtypeStruct((rows, heads * hw), BF16),
                   jax.ShapeDtypeStruct((rows, heads * LANES), BF16)],
        compiler_params=_cparams(("parallel", "parallel")), name="mla_prep")(q_full, kv, kr, cos, sin, qg_p, kg_p)
    tq = _pick(n_lat, (1024, 512, 256, 128))
    sub = min(tq, 256)
    q3 = q_cat.reshape(batch, lt, heads * hw)
    k3 = k_cat.reshape(batch, lt, heads * hw)
    v3 = v.reshape(batch, lt, heads * LANES)
    o = pl.pallas_call(
        functools.partial(_mla_attn_body, sub=sub), grid=(batch, heads, n_lat // tq),
        in_specs=[pl.BlockSpec((None, tq, hw), lambda b, hh, i: (b, i, hh)),
                  pl.BlockSpec((None, lt, hw), lambda b, hh, i: (b, 0, hh)),
                  pl.BlockSpec((None, lt, LANES), lambda b, hh, i: (b, 0, hh))],
        out_specs=pl.BlockSpec((None, tq, LANES), lambda b, hh, i: (b, i, hh)),
        out_shape=jax.ShapeDtypeStruct((batch, n_lat, heads * LANES), BF16),
        compiler_params=_cparams(("parallel", "parallel", "arbitrary")), name="mla_attn")(q3, k3, v3)
    return o.reshape(batch * n_lat, heads * LANES)


def kernel(x, c, ctx, c_ctx, mod_down, mod_up, mod_b, norm1_g, norm2_g, swa_w_in, swa_q_g, swa_k_g, swa_sinks, swa_w_out, ssd_w_in, ssd_conv_w, ssd_conv_b, ssd_dt_bias, ssd_a_log, ssd_d, ssd_norm_g, ssd_w_out, s5_w_in, s5_lam_re, s5_lam_im, s5_log_dt, s5_b_re, s5_b_im, s5_c_re, s5_c_im, s5_d, s5_w_glu, mla_w_in, mla_q_a_g, mla_kv_a_g, mla_w_uq, mla_w_ukv, mla_q_g, mla_k_g, mla_w_out, moe_w_group, moe_b_group, moe_w_expert, moe_b_expert, moe_w1, moe_w3, moe_w2):
    batch, n_lat, d = x.shape
    n_ctx = ctx.shape[1]
    lt = n_lat + n_ctx
    rows = batch * lt
    depth = mod_down.shape[0]
    dims = dict(batch=batch, lt=lt, n_lat=n_lat)
    tm = _pick(lt, (768, 512, 384, 256, 128))
    tn = _pick(d, (512, 256, 128))

    xs = jnp.concatenate([x, ctx], axis=1).reshape(rows, d)

    pad_rows = -(batch + 1) % SUBLANES
    cvecs = jnp.concatenate([c, c_ctx[None], jnp.zeros((pad_rows, d), F32)], axis=0)
    mod_all = _adaln(cvecs, mod_down, mod_up, mod_b).reshape(depth, batch + 1 + pad_rows, 6, d)

    n_moe_logits = MOE_GROUPS + moe_w_expert.shape[-1]
    router_w = jnp.pad(jnp.concatenate([moe_w_group, moe_w_expert], -1), ((0, 0), (0, 0), (0, LANES - n_moe_logits)))
    router_b = jnp.pad(jnp.concatenate([moe_b_group, moe_b_expert], -1), ((0, 0), (0, LANES - n_moe_logits)))

    def out_proj(o, w, res, mods4, name, last, epilogue="residual", cols=(0,)):
        tk = _pick(o.shape[1], (4096, 2048, 1024, 512, 256, 128))
        use_rows = n_lat if last else lt
        tm_o = _pick(n_lat, (1024, 512, 256, 128)) if last else tm
        return _matmul(o, w.astype(BF16), batch=batch, col_blocks=list(cols), n_out=d, tm=tm_o, tn=tn, tk=tk,
                       epilogue=epilogue, res=res, mods=mods4, gate_idx=2, use_rows=use_rows, n_lat=n_lat, name=name)

    for i in range(depth):
        kind, slot = i % 4, i // 4
        last = i == depth - 1
        ml = mod_all[i, :batch]
        mc = jnp.broadcast_to(mod_all[i, batch][None], (batch, 6, d))
        mods4 = jnp.stack([ml, mc], axis=1)
        mods2 = mods4.reshape(batch * 2, 6, d)
        h = _modnorm(xs, norm1_g[i], mods2, shift_idx=0, scale_idx=1, lt=lt, n_lat=n_lat, out_dtype=BF16)
        if kind == 0:
            o = _mixer_swa(h, swa_w_in[slot].astype(BF16), swa_q_g[slot], swa_k_g[slot], swa_sinks[slot], **dims)
            xs = out_proj(o, swa_w_out[slot], xs, mods4, "swa_out", last)
        elif kind == 1:
            o = _mixer_ssd(h, ssd_w_in[slot].astype(BF16), ssd_conv_w[slot], ssd_conv_b[slot], ssd_dt_bias[slot],
                           ssd_a_log[slot], ssd_d[slot], ssd_norm_g[slot], **dims)
            xs = out_proj(o, ssd_w_out[slot], xs, mods4, "ssd_out", last)
        elif kind == 2:
            o = _mixer_s5(h, s5_w_in[slot].astype(BF16), s5_lam_re[slot], s5_lam_im[slot], s5_log_dt[slot],
                          s5_b_re[slot], s5_b_im[slot], s5_c_re[slot], s5_c_im[slot], s5_d[slot], **dims)
            xs = out_proj(o, s5_w_glu[slot], xs, mods4, "s5_glu", last, epilogue="glu_residual", cols=(0, d // tn))
        else:
            if not last:
                raise NotImplementedError("the MLA mixer is implemented for the last layer only (latent queries)")
            o = _mixer_mla(h, mla_w_in[slot], mla_q_a_g[slot], mla_kv_a_g[slot], mla_w_uq[slot], mla_w_ukv[slot],
                           mla_q_g[slot], mla_k_g[slot], **dims)
            xs = out_proj(o, mla_w_out[slot], xs, mods4, "mla_out", last)
        xs = _moe_layer(xs, mods2, norm2_g[i], router_w[i], router_b[i].reshape(1, LANES), moe_w1, moe_w3, moe_w2, i,
                        lt=n_lat if last else lt, n_lat=n_lat)
    return xs.reshape(batch, n_lat, d)
```

```python
import functools
import math

import jax
import jax.numpy as jnp
from jax import lax
from jax.experimental import pallas as pl
from jax.experimental.pallas import tpu as pltpu

F32 = jnp.float32
BF16 = jnp.bfloat16
HIGHEST = lax.Precision.HIGHEST

GRID_W = 64
ROPE_BASE = 10000.0
NORM_EPS = 1e-6

SWA_HEADS = 32
SWA_KV_HEADS = 8
SWA_HEAD_DIM = 128
SWA_WINDOW = 128
SWA_BLOCK = 128

SSD_HEAD_DIM = 64
SSD_GROUPS = 8
SSD_STATE = 128
SSD_CHUNK = 128

S5_GROUP = 16
S5_STATE = 64
S5_CHUNK = 256
S5_SLAB_GROUPS = 16

MLA_HEADS = 32
MLA_Q_RANK = 1024
MLA_KV_RANK = 512
MLA_NOPE = 128
MLA_ROPE = 64
MLA_V = 128

MOE_GROUPS = 4
MOE_PER_GROUP = 8
MOE_TOPK = 2
MOE_BLOCK = 256

LANES = 128
SUBLANES = 8
VMEM_LIMIT = 56 * 1024 * 1024


def _cparams(sem, vmem=VMEM_LIMIT):
    return pltpu.CompilerParams(dimension_semantics=sem, vmem_limit_bytes=vmem)


def _pick(n, cands):
    for c in cands:
        if n % c == 0:
            return c
    raise ValueError(f"no tile in {cands} divides {n}")


def _sigmoid(x):
    return 1.0 / (1.0 + jnp.exp(-x))


def _silu(x):
    return x * _sigmoid(x)


def _softplus(x):
    return jnp.maximum(x, 0.0) + jnp.log1p(jnp.exp(-jnp.abs(x)))


def _pack_bf16_pairs(t):
    n = t.shape[1] // 2

    def rne_bits(v):
        u = lax.bitcast_convert_type(v, jnp.uint32)
        return u + jnp.uint32(0x7FFF) + ((u >> 16) & jnp.uint32(1))

    return (rne_bits(t[:, :n]) >> 16) | (rne_bits(t[:, n:]) & jnp.uint32(0xFFFF0000))


def _unpack_bf16_pairs(u):
    lo = lax.bitcast_convert_type(u << 16, F32)
    hi = lax.bitcast_convert_type(u & jnp.uint32(0xFFFF0000), F32)
    return lo, hi


def _nt_dot(a, b):
    return lax.dot_general(a, b, (((1,), (1,)), ((), ())), preferred_element_type=F32)


def _mm_body(*refs, n_w, nk, epilogue, tm, tiles_per_batch, n_lat, gate_idx):
    a_ref = refs[0]
    w_refs = refs[1:1 + n_w]
    pos = 1 + n_w
    res_ref = mod_ref = None
    if epilogue in ("residual", "glu_residual"):
        res_ref, mod_ref = refs[pos], refs[pos + 1]
        pos += 2
    o_ref = refs[pos]
    acc_refs = refs[pos + 1:]

    def finish(vals):
        val = vals[0] * _sigmoid(vals[1]) if epilogue in ("glu", "glu_residual") else vals[0]
        if res_ref is not None:
            row = (pl.program_id(0) % tiles_per_batch) * tm + lax.broadcasted_iota(jnp.int32, (tm, 1), 0)
            gate = jnp.where(row >= n_lat, mod_ref[1, pl.ds(gate_idx, 1), :], mod_ref[0, pl.ds(gate_idx, 1), :])
            val = res_ref[...] + gate * val
        o_ref[...] = val.astype(o_ref.dtype)

    a = a_ref[...]
    if nk == 1:
        finish([jnp.dot(a, w[...], preferred_element_type=F32) for w in w_refs])
        return
    k = pl.program_id(2)

    @pl.when(k == 0)
    def _():
        for acc in acc_refs:
            acc[...] = jnp.zeros_like(acc)

    for w, acc in zip(w_refs, acc_refs):
        acc[...] += jnp.dot(a, w[...], preferred_element_type=F32)

    @pl.when(k == nk - 1)
    def _():
        finish([acc[...] for acc in acc_refs])


def _matmul(a, w, *, batch, col_blocks, n_out, tm, tn, tk=None, out_dtype=F32, epilogue="store", res=None, mods=None,
            gate_idx=0, use_rows=None, n_lat=None, out_shape=None, out_map=None, name="matmul"):
    k_dim = a.shape[1]
    a3 = a.reshape(batch, a.shape[0] // batch, k_dim)
    use_rows = use_rows or a3.shape[1]
    tk = tk or k_dim
    nk = k_dim // tk
    n_w = len(col_blocks)
    tu = use_rows // tm
    grid = (batch * tu, n_out // tn, nk)
    in_specs = [pl.BlockSpec((None, tm, tk), lambda i, j, k: (i // tu, i % tu, k))]
    args = [a3]
    for off in col_blocks:
        in_specs.append(pl.BlockSpec((tk, tn), lambda i, j, k, off=off: (k, j + off)))
        args.append(w)
    if res is not None:
        in_specs.append(pl.BlockSpec((None, tm, tn), lambda i, j, k: (i // tu, i % tu, j)))
        in_specs.append(pl.BlockSpec((None, 2, 6, tn), lambda i, j, k: (i // tu, 0, 0, j)))
        args += [res.reshape(batch, res.shape[0] // batch, res.shape[1]), mods]
    out_shape = out_shape or (batch * use_rows, n_out)
    out_map = out_map or (lambda i, j, k: (i, j))
    scratch = [pltpu.VMEM((tm, tn), F32) for _ in range(n_w)] if nk > 1 else []
    body = functools.partial(_mm_body, n_w=n_w, nk=nk, epilogue=epilogue, tm=tm, tiles_per_batch=tu,
                             n_lat=n_lat, gate_idx=gate_idx)
    return pl.pallas_call(
        body, grid=grid, in_specs=in_specs, out_specs=pl.BlockSpec((tm, tn), out_map),
        out_shape=jax.ShapeDtypeStruct(out_shape, out_dtype), scratch_shapes=scratch,
        compiler_params=_cparams(("parallel", "parallel", "arbitrary")), name=name)(*args)


def _adaln_body(c_ref, wd_ref, wu_ref, b_ref, o_ref, t_ref):
    @pl.when(pl.program_id(1) == 0)
    def _():
        cv = c_ref[...]
        t_ref[...] = jnp.dot(_silu(cv), wd_ref[...], precision=HIGHEST, preferred_element_type=F32)

    o_ref[...] = jnp.dot(t_ref[...], wu_ref[...], precision=HIGHEST, preferred_element_type=F32) + b_ref[...]


def _adaln(cvecs, mod_down, mod_up, mod_b):
    depth, d, rank = mod_down.shape
    n6 = mod_up.shape[-1]
    rows = cvecs.shape[0]
    tn = _pick(n6, (2048, 1024, 512, 256, 128))
    return pl.pallas_call(
        _adaln_body, grid=(depth, n6 // tn),
        in_specs=[pl.BlockSpec((rows, d), lambda l, j: (0, 0)),
                  pl.BlockSpec((None, d, rank), lambda l, j: (l, 0, 0)),
                  pl.BlockSpec((None, rank, tn), lambda l, j: (l, 0, j)),
                  pl.BlockSpec((None, 1, tn), lambda l, j: (l, 0, j))],
        out_specs=pl.BlockSpec((None, rows, tn), lambda l, j: (l, 0, j)),
        out_shape=jax.ShapeDtypeStruct((depth, rows, n6), F32),
        scratch_shapes=[pltpu.VMEM((rows, rank), F32)],
        compiler_params=_cparams(("parallel", "arbitrary")), name="adaln")(
            cvecs, mod_down, mod_up, mod_b.reshape(depth, 1, n6))


def _modnorm_body(*refs, shift_idx, scale_idx, router):
    if router:
        x_ref, g_ref, mod_ref, wh_ref, wl_ref, br_ref, h_ref, ids_ref, gates_ref, hb_ref = refs
    else:
        x_ref, g_ref, mod_ref, h_ref = refs
    tm, d = x_ref.shape
    half = d // 2
    rb = 16
    cw = min(4 * LANES, half)

    def row_block(r, carry):
        rows = pl.ds(pl.multiple_of(r * rb, rb), rb)
        ss = jnp.zeros((rb, 1), F32)
        for c in range(d // cw):
            xc = x_ref[rows, c * cw:(c + 1) * cw]
            ss = ss + jnp.sum(xc * xc, axis=-1, keepdims=True)
        rinv = lax.rsqrt(ss / d + NORM_EPS)

        def h_chunk(c0):
            cols = slice(c0, c0 + cw)
            gs = g_ref[:, cols] * (1.0 + mod_ref[pl.ds(scale_idx, 1), cols])
            return x_ref[rows, cols] * rinv * gs + mod_ref[pl.ds(shift_idx, 1), cols]

        for c in range(half // cw):
            pieces = []
            for c0 in (c * cw, half + c * cw):
                h = h_chunk(c0)
                if not router:
                    h_ref[rows, c0:c0 + cw] = h.astype(h_ref.dtype)
                    continue
                h_b = h.astype(BF16)
                hb_ref[rows, c0:c0 + cw] = h_b
                pieces.append(lax.bitcast_convert_type(h_b.astype(F32), jnp.uint32))
            if router:
                h_ref[rows, c * cw:(c + 1) * cw] = (pieces[0] >> 16) | pieces[1]
        return carry

    lax.fori_loop(0, tm // rb, row_block, 0, unroll=2)
    if not router:
        return
    h_b = hb_ref[...]
    logits = (jnp.dot(h_b, wh_ref[...], preferred_element_type=F32)
              + jnp.dot(h_b, wl_ref[...], preferred_element_type=F32)) + br_ref[...]
    lane = lax.broadcasted_iota(jnp.int32, logits.shape, 1)
    lane_f = lane.astype(F32)
    neg = -jnp.inf

    def first_lane(hit):
        return jnp.min(jnp.where(hit, lane_f, float(LANES)), axis=-1, keepdims=True).astype(jnp.int32)

    is_grp = lane < MOE_GROUPS
    lg = jnp.where(is_grp, logits, neg)
    mg = jnp.max(lg, axis=-1, keepdims=True)
    grp = first_lane(lg == mg)
    p_grp = 1.0 / jnp.sum(jnp.where(is_grp, jnp.exp(lg - mg), 0.0), axis=-1, keepdims=True)
    lo = MOE_GROUPS + grp * MOE_PER_GROUP
    le = jnp.where((lane >= lo) & (lane < lo + MOE_PER_GROUP), logits, neg)
    m1 = jnp.max(le, axis=-1, keepdims=True)
    i1 = first_lane(le == m1)
    le2 = jnp.where(lane == i1, neg, le)
    m2 = jnp.max(le2, axis=-1, keepdims=True)
    i2 = first_lane(le2 == m2)
    e2 = jnp.exp(m2 - m1)
    g1 = p_grp / (1.0 + e2)
    g2 = p_grp * e2 / (1.0 + e2)
    ids_ref[...] = jnp.where(lane == 0, i1 - MOE_GROUPS, jnp.where(lane == 1, i2 - MOE_GROUPS, 0))
    gates_ref[...] = jnp.where(lane == 0, g1, jnp.where(lane == 1, g2, 0.0))


def _modnorm(x, g, mods2, *, shift_idx, scale_idx, lt, n_lat, out_dtype, router_w=None, router_b=None):
    rows, d = x.shape
    tm = _pick(math.gcd(lt, n_lat), (256, 128))
    tpb = lt // tm
    lat_tiles = n_lat // tm
    router = router_w is not None
    in_specs = [pl.BlockSpec((tm, d), lambda i: (i, 0)),
                pl.BlockSpec((1, d), lambda i: (0, 0)),
                pl.BlockSpec((None, 6, d), lambda i: ((i // tpb) * 2 + ((i % tpb) >= lat_tiles).astype(jnp.int32), 0, 0))]
    args = [x, g.reshape(1, d), mods2]
    out_specs = [pl.BlockSpec((tm, d), lambda i: (i, 0))]
    out_shape = [jax.ShapeDtypeStruct((rows, d), out_dtype)]
    if router:
        out_specs = [pl.BlockSpec((tm, d // 2), lambda i: (i, 0))]
        out_shape = [jax.ShapeDtypeStruct((rows, d // 2), jnp.uint32)]
        w_spec = pl.BlockSpec((d, LANES), lambda i: (0, 0))
        in_specs += [w_spec, w_spec, pl.BlockSpec((1, LANES), lambda i: (0, 0))]
        w_hi = router_w.astype(BF16)
        args += [w_hi, (router_w - w_hi.astype(F32)).astype(BF16), router_b]
        out_specs += [pl.BlockSpec((tm, LANES), lambda i: (i, 0))] * 2
        out_shape += [jax.ShapeDtypeStruct((rows, LANES), jnp.int32), jax.ShapeDtypeStruct((rows, LANES), F32)]
    scratch = [pltpu.VMEM((tm, d), BF16)] if router else []
    body = functools.partial(_modnorm_body, shift_idx=shift_idx, scale_idx=scale_idx, router=router)
    out = pl.pallas_call(body, grid=(rows // tm,), in_specs=in_specs, out_specs=out_specs, out_shape=out_shape,
                         scratch_shapes=scratch,
                         compiler_params=_cparams(("parallel",)), name="modnorm_router" if router else "modnorm")(*args)
    return out if router else out[0]


def _row_gather_start(idx_ref, base, n, src_hbm, dst, sem, dst_base=0, idx_stride=1):
    def body(r, c):
        row = idx_ref[base + r * idx_stride]
        pltpu.make_async_copy(src_hbm.at[pl.ds(row, 1)], dst.at[pl.ds(dst_base + r, 1)], sem).start()
        return c

    lax.fori_loop(0, n, body, 0, unroll=8)


def _row_gather_wait(n, src_hbm, dst, sem):
    def body(r, c):
        pltpu.make_async_copy(src_hbm.at[pl.ds(0, 1)], dst.at[pl.ds(r, 1)], sem).wait()
        return c

    lax.fori_loop(0, n, body, 0, unroll=8)


def _moe_ffn_body(src_ref, blk_e_ref, n_used_ref, h_hbm, w1_ref, w3_ref, w2_ref, o_ref,
                  xbuf, xb, w1b, w3b, w2b, sems, *, blk):
    i = pl.program_id(0)
    slot = i % 2
    n_used = n_used_ref[0]
    half = xb.shape[1] // 2

    @pl.when(i == 0)
    def _():
        _row_gather_start(src_ref, 0, blk, h_hbm, xbuf.at[0], sems.at[0])

    @pl.when(i + 1 < n_used)
    def _():
        _row_gather_start(src_ref, (i + 1) * blk, blk, h_hbm, xbuf.at[1 - slot], sems.at[1 - slot])

    @pl.when(i < n_used)
    def _():
        @pl.when(jnp.logical_or(i == 0, blk_e_ref[i] != blk_e_ref[jnp.maximum(i - 1, 0)]))
        def _():
            w1b[...] = w1_ref[...].astype(BF16)
            w3b[...] = w3_ref[...].astype(BF16)
            w2b[...] = w2_ref[...].astype(BF16)

        _row_gather_wait(blk, h_hbm, xbuf.at[slot], sems.at[slot])
        lo, hi = _unpack_bf16_pairs(xbuf[slot])
        xb[:, :half] = lo.astype(BF16)
        xb[:, half:] = hi.astype(BF16)
        x = xb[...]
        a1 = jnp.dot(x, w1b[...], preferred_element_type=F32)
        a3 = jnp.dot(x, w3b[...], preferred_element_type=F32)
        mid = (_silu(a1) * a3).astype(BF16)
        o_ref[...] = _pack_bf16_pairs(jnp.dot(mid, w2b[...], preferred_element_type=F32))

    @pl.when(i >= n_used)
    def _():
        o_ref[...] = jnp.zeros_like(o_ref)


def _moe_ffn(hpk, src, blk_e, n_used, w1, w3, w2, layer, *, blk):
    n_rows = src.shape[0]
    n_blk = n_rows // blk
    half = hpk.shape[1]
    d = 2 * half
    hid = w1.shape[-1]
    grid_spec = pltpu.PrefetchScalarGridSpec(
        num_scalar_prefetch=3, grid=(n_blk,),
        in_specs=[pl.BlockSpec(memory_space=pl.ANY),
                  pl.BlockSpec((None, None, d, hid), lambda i, s, e, n: (layer, e[i], 0, 0)),
                  pl.BlockSpec((None, None, d, hid), lambda i, s, e, n: (layer, e[i], 0, 0)),
                  pl.BlockSpec((None, None, hid, d), lambda i, s, e, n: (layer, e[i], 0, 0))],
        out_specs=pl.BlockSpec((blk, half), lambda i, s, e, n: (i, 0)),
        scratch_shapes=[pltpu.VMEM((2, blk, half), jnp.uint32), pltpu.VMEM((blk, d), BF16),
                        pltpu.VMEM((d, hid), BF16), pltpu.VMEM((d, hid), BF16), pltpu.VMEM((hid, d), BF16),
                        pltpu.SemaphoreType.DMA((2,))])
    return pl.pallas_call(
        functools.partial(_moe_ffn_body, blk=blk), grid_spec=grid_spec,
        out_shape=jax.ShapeDtypeStruct((n_rows, half), jnp.uint32),
        compiler_params=_cparams(("arbitrary",)), name="moe_ffn")(src, blk_e, n_used, hpk, w1, w3, w2)


def _moe_combine_body(pos_ref, x_ref, gates_ref, mod_ref, yb_hbm, o_ref, ybuf, sems, *, tm, n_tiles):
    i = pl.program_id(0)
    slot = i % 2
    half = x_ref.shape[1] // 2

    def start(tile, s):
        for kk in range(MOE_TOPK):
            _row_gather_start(pos_ref, tile * tm * MOE_TOPK + kk, tm, yb_hbm, ybuf.at[s], sems.at[s],
                              dst_base=kk * tm, idx_stride=MOE_TOPK)

    @pl.when(i == 0)
    def _():
        start(0, 0)

    @pl.when(i + 1 < n_tiles)
    def _():
        start(i + 1, 1 - slot)

    _row_gather_wait(MOE_TOPK * tm, yb_hbm, ybuf.at[slot], sems.at[slot])
    ys = ybuf.at[slot]
    rb, cb = 32, min(4 * LANES, half)
    for r in range(tm // rb):
        rows = pl.ds(r * rb, rb)
        g0 = jnp.broadcast_to(gates_ref[rows, 0:1], (rb, cb))
        g1 = jnp.broadcast_to(gates_ref[rows, 1:2], (rb, cb))
        for c in range(half // cb):
            lo0, hi0 = _unpack_bf16_pairs(ys[rows, c * cb:(c + 1) * cb])
            lo1, hi1 = _unpack_bf16_pairs(ys[pl.ds(tm + r * rb, rb), c * cb:(c + 1) * cb])
            for off, y in ((c * cb, g0 * lo0 + g1 * lo1), (half + c * cb, g0 * hi0 + g1 * hi1)):
                o_ref[rows, off:off + cb] = x_ref[rows, off:off + cb] + mod_ref[pl.ds(5, 1), off:off + cb] * y


def _moe_combine(x, gates, mods2, yb, pos, *, lt, n_lat):
    rows, d = x.shape
    tm = _pick(math.gcd(lt, n_lat), (256, 128))
    tpb = lt // tm
    lat_tiles = n_lat // tm
    n_tiles = rows // tm
    grid_spec = pltpu.PrefetchScalarGridSpec(
        num_scalar_prefetch=1, grid=(n_tiles,),
        in_specs=[pl.BlockSpec((tm, d), lambda i, p: (i, 0)),
                  pl.BlockSpec((tm, LANES), lambda i, p: (i, 0)),
                  pl.BlockSpec((None, 6, d),
                               lambda i, p: ((i // tpb) * 2 + ((i % tpb) >= lat_tiles).astype(jnp.int32), 0, 0)),
                  pl.BlockSpec(memory_space=pl.ANY)],
        out_specs=pl.BlockSpec((tm, d), lambda i, p: (i, 0)),
        scratch_shapes=[pltpu.VMEM((2, MOE_TOPK * tm, d // 2), jnp.uint32), pltpu.SemaphoreType.DMA((2,))])
    return pl.pallas_call(
        functools.partial(_moe_combine_body, tm=tm, n_tiles=n_tiles), grid_spec=grid_spec,
        out_shape=jax.ShapeDtypeStruct((rows, d), F32),
        compiler_params=_cparams(("arbitrary",)), name="moe_combine")(pos, x, gates, mods2, yb)


def _moe_layer(x, mods2, norm_g, router_w, router_b, w1, w3, w2, layer, *, lt, n_lat):
    rows, d = x.shape
    n_exp = w1.shape[1]
    hpk, ids, gates = _modnorm(x, norm_g, mods2, shift_idx=3, scale_idx=4, lt=lt, n_lat=n_lat, out_dtype=F32,
                               router_w=router_w, router_b=router_b)
    blk = MOE_BLOCK
    n_asg = rows * MOE_TOPK
    n_blk = n_asg // blk + n_exp
    flat = ids[:, :MOE_TOPK].reshape(-1)
    order = jnp.argsort(flat).astype(jnp.int32)
    sorted_e = flat[order]
    counts = jnp.sum((flat[:, None] == jnp.arange(n_exp, dtype=jnp.int32)[None, :]).astype(jnp.int32), axis=0)
    padded = (counts + blk - 1) // blk * blk
    pad_end = jnp.cumsum(padded)
    pad_start = pad_end - padded
    start = jnp.cumsum(counts) - counts
    dest = (pad_start[sorted_e] + jnp.arange(n_asg, dtype=jnp.int32) - start[sorted_e]).astype(jnp.int32)
    blk_first = jnp.arange(n_blk, dtype=jnp.int32) * blk
    blk_e = jnp.minimum(jnp.sum((pad_end[None, :] <= blk_first[:, None]).astype(jnp.int32), axis=1), n_exp - 1)
    n_used = (pad_end[-1:] // blk).astype(jnp.int32)
    within = jnp.arange(n_blk * blk, dtype=jnp.int32) - jnp.repeat(pad_start[blk_e], blk)
    valid = within < jnp.repeat(counts[blk_e], blk)
    sorted_idx = jnp.clip(jnp.repeat(start[blk_e], blk) + within, 0, n_asg - 1)
    src = jnp.where(valid, order[sorted_idx] // MOE_TOPK, 0).astype(jnp.int32)
    pos = dest[jnp.argsort(order)]
    yb = _moe_ffn(hpk, src, blk_e.astype(jnp.int32), n_used, w1, w3, w2, layer, blk=blk)
    return _moe_combine(x, gates, mods2, yb, pos.astype(jnp.int32), lt=lt, n_lat=n_lat)


def _rope_tables(n_lat, n_ctx, rot_dim, width):
    rows = n_lat // GRID_W
    row = jnp.repeat(jnp.arange(rows, dtype=F32), GRID_W)
    col = jnp.tile(jnp.arange(GRID_W, dtype=F32), rows)
    axis_dim = rot_dim // 2
    inv_freq = ROPE_BASE ** (-jnp.arange(0, axis_dim, 2, dtype=F32) / axis_dim)
    a0 = row[:, None] * inv_freq
    a1 = col[:, None] * inv_freq
    cos = jnp.concatenate([jnp.cos(a0), jnp.cos(a0), jnp.cos(a1), jnp.cos(a1)], -1)
    sin = jnp.concatenate([-jnp.sin(a0), jnp.sin(a0), -jnp.sin(a1), jnp.sin(a1)], -1)
    cos = jnp.pad(cos, ((0, n_ctx), (0, width - rot_dim)), constant_values=1.0)
    sin = jnp.pad(sin, ((0, n_ctx), (0, width - rot_dim)))
    return cos, sin


def _split2(t):
    hi = t.astype(BF16)
    return hi, (t - hi.astype(F32)).astype(BF16)


def _rowsum_lanes(t):
    ones = jnp.ones((LANES, LANES), BF16)
    hi, lo = _split2(t)
    return jnp.dot(hi, ones, preferred_element_type=F32) + jnp.dot(lo, ones, preferred_element_type=F32)


def _rope_perm(half):
    src = lax.broadcasted_iota(jnp.int32, (LANES, LANES), 0)
    dst = lax.broadcasted_iota(jnp.int32, (LANES, LANES), 1)
    partner = jnp.where((dst & (2 * half - 1)) < half, dst + half, dst - half)
    return (src == partner).astype(BF16)


def _rope(x, cos, sin, perm):
    hi, lo = _split2(x)
    partner = jnp.dot(hi, perm, preferred_element_type=F32) + jnp.dot(lo, perm, preferred_element_type=F32)
    return x * cos + partner * sin


def _swa_prep_body(p_ref, cos_ref, sin_ref, qg_ref, kg_ref, q_ref, k_ref, v_ref, *, n_q, n_kv):
    hd = SWA_HEAD_DIM
    cos, sin = cos_ref[...], sin_ref[...]
    scale = hd ** -0.5

    perm = _rope_perm(hd // 4)

    def norm_rope(t, g):
        t = t * lax.rsqrt(_rowsum_lanes(t * t) / hd + NORM_EPS) * g
        return _rope(t, cos, sin, perm)

    for h in range(n_q):
        q_ref[:, h * hd:(h + 1) * hd] = (norm_rope(p_ref[:, h * hd:(h + 1) * hd], qg_ref[...]) * scale).astype(BF16)
    for h in range(n_kv):
        c0 = (n_q + h) * hd
        k_ref[:, h * hd:(h + 1) * hd] = norm_rope(p_ref[:, c0:c0 + hd], kg_ref[...]).astype(BF16)
    v_ref[...] = p_ref[:, (n_q + n_kv) * hd:].astype(BF16)


def _swa_attn_body(sink_ref, q_ref, *refs, n_lat, grp, nq):
    nw = nq + 2
    k_refs, v_refs, o_ref = refs[:nw + 1], refs[nw + 1:2 * nw + 2], refs[2 * nw + 2]
    kv = pl.program_id(1)
    first = pl.program_id(2) * nq
    blk = SWA_BLOCK
    hd = SWA_HEAD_DIM
    n_ctx = k_refs[nw].shape[0]
    is_lat = first < n_lat // blk
    qpos = first * blk + lax.broadcasted_iota(jnp.int32, (nq * blk, 1), 0)
    biases = []
    for w in range(nw):
        kpos = (first - 1 + w) * blk + lax.broadcasted_iota(jnp.int32, (1, blk), 1)
        valid = (jnp.abs(kpos - qpos) <= SWA_WINDOW) & (kpos >= 0) & (kpos < n_lat) & is_lat
        biases.append(jnp.where(valid, 0.0, -jnp.inf))
    biases.append(jnp.zeros((nq * blk, n_ctx), F32))
    bias = jnp.concatenate(biases, axis=1)
    k_all = jnp.concatenate([r[...] for r in k_refs], axis=0)
    v_all = jnp.concatenate([r[...] for r in v_refs], axis=0)
    for g in range(grp):
        sink = sink_ref[kv * grp + g]
        s = _nt_dot(q_ref[:, g * hd:(g + 1) * hd], k_all) + bias
        m = jnp.maximum(jnp.max(s, axis=-1, keepdims=True), sink)
        p = jnp.exp(s - m)
        denom = jnp.sum(p, axis=-1, keepdims=True) + jnp.exp(sink - m)
        o = jnp.dot(p.astype(BF16), v_all, preferred_element_type=F32) / denom
        o_ref[:, g * hd:(g + 1) * hd] = o.astype(o_ref.dtype)


def _mixer_swa(h, w_in, q_g, k_g, sinks, *, batch, lt, n_lat):
    rows = h.shape[0]
    hd = SWA_HEAD_DIM
    n_q, n_kv = SWA_HEADS, SWA_KV_HEADS
    grp = n_q // n_kv
    n_ctx = lt - n_lat
    n_cols = (n_q + 2 * n_kv) * hd
    tm = _pick(lt, (768, 512, 384, 256, 128))
    tn = _pick(n_cols, (512, 256, 128))
    p = _matmul(h, w_in, batch=batch, col_blocks=[0], n_out=n_cols, tm=tm, tn=tn, name="swa_in")
    cos, sin = _rope_tables(n_lat, n_ctx, hd, hd)
    tp = _pick(math.gcd(lt, n_lat), (256, 128))
    tpb = lt // tp
    q, k, v = pl.pallas_call(
        functools.partial(_swa_prep_body, n_q=n_q, n_kv=n_kv), grid=(rows // tp,),
        in_specs=[pl.BlockSpec((tp, n_cols), lambda i: (i, 0)),
                  pl.BlockSpec((tp, hd), lambda i: (i % tpb, 0)),
                  pl.BlockSpec((tp, hd), lambda i: (i % tpb, 0)),
                  pl.BlockSpec((1, hd), lambda i: (0, 0)),
                  pl.BlockSpec((1, hd), lambda i: (0, 0))],
        out_specs=[pl.BlockSpec((tp, n_q * hd), lambda i: (i, 0)),
                   pl.BlockSpec((tp, n_kv * hd), lambda i: (i, 0)),
                   pl.BlockSpec((tp, n_kv * hd), lambda i: (i, 0))],
        out_shape=[jax.ShapeDtypeStruct((rows, n_q * hd), BF16),
                   jax.ShapeDtypeStruct((rows, n_kv * hd), BF16),
                   jax.ShapeDtypeStruct((rows, n_kv * hd), BF16)],
        compiler_params=_cparams(("parallel",)), name="swa_prep")(p, cos, sin, q_g.reshape(1, hd), k_g.reshape(1, hd))

    blk = SWA_BLOCK
    bpb = lt // blk
    lat_blocks = n_lat // blk

    nq = 2
    steps = bpb // nq

    def win(off):
        return lambda b, kvh, j: (b * bpb + jnp.clip(j * nq + off, 0, lat_blocks - 1), kvh)

    ctx_spec = pl.BlockSpec((None, n_ctx, hd), lambda b, kvh, j: (b, n_lat // n_ctx, kvh))
    k3 = k.reshape(batch, lt, n_kv * hd)
    v3 = v.reshape(batch, lt, n_kv * hd)
    kv_spec = [pl.BlockSpec((blk, hd), win(off)) for off in range(-1, nq + 1)]
    q_spec = pl.BlockSpec((nq * blk, grp * hd), lambda b, kvh, j: (b * steps + j, kvh))
    n_win = len(kv_spec)
    return pl.pallas_call(
        functools.partial(_swa_attn_body, n_lat=n_lat, grp=grp, nq=nq), grid=(batch, n_kv, steps),
        in_specs=[pl.BlockSpec(memory_space=pltpu.SMEM), q_spec] + kv_spec + [ctx_spec] + kv_spec + [ctx_spec],
        out_specs=q_spec,
        out_shape=jax.ShapeDtypeStruct((rows, n_q * hd), BF16),
        compiler_params=_cparams(("parallel", "parallel", "arbitrary")), name="swa_attn")(
            sinks.astype(F32), q, *([k] * n_win), k3, *([v] * n_win), v3)


def _conv_silu_body(x_ref, prev_ref, next_ref, w_ref, b_ref, o_ref, *, lat_tiles, n_tiles):
    t = pl.program_id(1)
    x = x_ref[...]
    rows = x.shape[0]
    has_prev = jnp.logical_and(t != 0, t != lat_tiles).astype(F32)
    has_next = jnp.logical_and(t != lat_tiles - 1, t != n_tiles - 1).astype(F32)
    row = lax.broadcasted_iota(jnp.int32, x.shape, 0)
    x_m = jnp.where(row == 0, prev_ref[SUBLANES - 1:SUBLANES, :] * has_prev, pltpu.roll(x, 1, 0))
    x_p = jnp.where(row == rows - 1, next_ref[0:1, :] * has_next, pltpu.roll(x, rows - 1, 0))
    y = w_ref[0:1, :] * x_m + w_ref[1:2, :] * x + w_ref[2:3, :] * x_p + b_ref[...]
    o_ref[...] = _silu(y).astype(o_ref.dtype)


def _conv_silu(xbc, conv_w, conv_b, *, batch, lt, n_lat):
    cols = xbc.shape[-1]
    x3 = xbc.reshape(batch, lt, cols)
    tt = _pick(math.gcd(lt, n_lat), (256, 128))
    tc = _pick(cols, (2048, 1024, 512, 256, 128))
    n_tiles = lt // tt
    sub = tt // SUBLANES
    out = pl.pallas_call(
        functools.partial(_conv_silu_body, lat_tiles=n_lat // tt, n_tiles=n_tiles),
        grid=(batch, n_tiles, cols // tc),
        in_specs=[pl.BlockSpec((None, tt, tc), lambda b, t, j: (b, t, j)),
                  pl.BlockSpec((None, SUBLANES, tc), lambda b, t, j: (b, jnp.maximum(t * sub - 1, 0), j)),
                  pl.BlockSpec((None, SUBLANES, tc), lambda b, t, j: (b, jnp.minimum((t + 1) * sub, lt // SUBLANES - 1), j)),
                  pl.BlockSpec((3, tc), lambda b, t, j: (0, j)),
                  pl.BlockSpec((1, tc), lambda b, t, j: (0, j))],
        out_specs=pl.BlockSpec((None, tt, tc), lambda b, t, j: (b, t, j)),
        out_shape=jax.ShapeDtypeStruct((batch, lt, cols), BF16),
        compiler_params=_cparams(("parallel", "parallel", "parallel")), name="ssd_conv")(
            x3, x3, x3, conv_w, conv_b.reshape(1, cols))
    return out


def _ssd_pass_body(*refs, direction, hg, final):
    if final:
        x_ref, b_ref, c_ref, dtt_ref, biast_ref, alogt_ref, y0_ref, z_ref, ng_ref, o_ref, state_ref, g_ref = refs
    else:
        x_ref, b_ref, c_ref, dtt_ref, biast_ref, alogt_ref, dsk_ref, o_ref, state_ref = refs
    t_len = x_ref.shape[0]
    n_state = b_ref.shape[1]

    @pl.when(pl.program_id(2) == 0)
    def _():
        state_ref[...] = jnp.zeros_like(state_ref)

    bm = b_ref[...]
    cm = c_ref[...]
    dt_t = _softplus(dtt_ref[...] + biast_ref[...])
    v_t = dt_t * (-jnp.exp(alogt_ref[...]))
    row = lax.broadcasted_iota(jnp.int32, (t_len, t_len), 0)
    col = lax.broadcasted_iota(jnp.int32, (t_len, t_len), 1)
    tri = (col <= row) if direction == 0 else (col >= row)
    cum_t = lax.dot_general(v_t, tri.astype(F32), (((1,), (1,)), ((), ())), precision=HIGHEST,
                            preferred_element_type=F32)
    total_t = jnp.sum(v_t, axis=1, keepdims=True)
    w_t = dt_t * jnp.exp(total_t - cum_t)
    cb = _nt_dot(cm, bm)
    b_t = bm.astype(F32).T
    left = lax.broadcasted_iota(jnp.int32, (t_len, LANES), 1) < SSD_HEAD_DIM
    left_n = lax.broadcasted_iota(jnp.int32, (n_state, LANES), 1) < SSD_HEAD_DIM

    def split3(t):
        hi = t.astype(BF16).astype(F32)
        mid = (t - hi).astype(BF16).astype(F32)
        return hi, mid, (t - hi - mid).astype(BF16).astype(F32)

    def spread(width):
        hrow = lax.broadcasted_iota(jnp.int32, (hg, hg * width), 0)
        hcol = lax.broadcasted_iota(jnp.int32, (hg, hg * width), 1)
        return jnp.logical_and(hcol >= hrow * width, hcol < (hrow + 1) * width).astype(F32)

    def tn_dot(a_rows, w_rows):
        pad = LANES - a_rows.shape[0]
        a = jnp.concatenate([a_rows, jnp.zeros((pad, a_rows.shape[1]), F32)], axis=0)
        w = jnp.concatenate([w_rows, jnp.zeros((pad, w_rows.shape[1]), F32)], axis=0)
        a_cols = jnp.concatenate([a[:, j * LANES:(j + 1) * LANES].T for j in range(a.shape[1] // LANES)], axis=0)
        return jnp.dot(a_cols.astype(BF16), w.astype(BF16), preferred_element_type=F32)

    e_t = spread(t_len)
    cum_pieces = split3(cum_t)
    seg_all = tn_dot(
        jnp.concatenate(list(cum_pieces) + [jnp.ones((3 * hg, t_len), F32)], axis=0),
        jnp.concatenate([e_t] * 3 + [-jnp.tile(p, (1, hg)) * e_t for p in cum_pieces], axis=0))
    e_x = spread(SSD_HEAD_DIM)
    cols = jnp.concatenate([jnp.exp(cum_t), jnp.broadcast_to(jnp.exp(total_t), (hg, LANES))], axis=1)
    cols_x = tn_dot(jnp.concatenate(split3(cols), axis=0), jnp.concatenate([e_x] * 3, axis=0))
    exp_cum_x = cols_x[:t_len]
    exp_total_x = cols_x[t_len:t_len + 1]

    for pr in range(hg // 2):
        h0 = 2 * pr
        sl = slice(pr * LANES, (pr + 1) * LANES)
        xb = x_ref[:, sl]
        ys, ss = [], []
        for hh in (h0, h0 + 1):
            dec = jnp.exp(jnp.where(tri, seg_all[:, hh * t_len:(hh + 1) * t_len], -jnp.inf))
            ys.append(jnp.dot((cb * dec * dt_t[hh:hh + 1, :]).astype(BF16), xb, preferred_element_type=F32))
            ss.append(jnp.dot((b_t * w_t[hh:hh + 1, :]).astype(BF16), xb, preferred_element_type=F32))
        st = state_ref[:, sl]
        y = jnp.where(left, ys[0], ys[1]) + jnp.dot(cm, st.astype(BF16), preferred_element_type=F32) * exp_cum_x[:, sl]
        state_ref[:, sl] = st * exp_total_x[:, sl] + jnp.where(left_n, ss[0], ss[1])
        if final:
            z = z_ref[:, sl]
            g_ref[:, sl] = (y0_ref[:, sl] + y) * _silu(z)
        else:
            o_ref[:, sl] = dsk_ref[:, sl] * xb.astype(F32) + y
    if final:
        g = g_ref[...]
        o_ref[...] = (g * lax.rsqrt(jnp.mean(g * g, axis=-1, keepdims=True) + NORM_EPS) * ng_ref[...]).astype(o_ref.dtype)


def _chunk_order(i, direction, lat_chunks, ctx_chunks):
    if direction == 0:
        return jnp.where(i < ctx_chunks, lat_chunks + i, i - ctx_chunks)
    return lat_chunks + ctx_chunks - 1 - i


def _ssd_pass(direction, xbc_s, dt_t, bias_t, alog_t, extra, *, batch, lt, n_lat, d_inner, final):
    groups = SSD_GROUPS
    gw = d_inner // groups
    hg = gw // SSD_HEAD_DIM
    n_state = SSD_STATE
    t_len = SSD_CHUNK
    lat_chunks, ctx_chunks = n_lat // t_len, (lt - n_lat) // t_len
    n_chunks = lat_chunks + ctx_chunks
    xb = d_inner // n_state

    def cidx(i):
        return _chunk_order(i, direction, lat_chunks, ctx_chunks)

    in_specs = [
        pl.BlockSpec((None, t_len, gw), lambda b, g, i: (b, cidx(i), g)),
        pl.BlockSpec((None, t_len, n_state), lambda b, g, i: (b, cidx(i), xb + g)),
        pl.BlockSpec((None, t_len, n_state), lambda b, g, i: (b, cidx(i), xb + groups + g)),
        pl.BlockSpec((None, None, None, hg, t_len), lambda b, g, i: (direction, g, b, 0, cidx(i))),
        pl.BlockSpec((None, None, hg, 1), lambda b, g, i: (direction, g, 0, 0)),
        pl.BlockSpec((None, None, hg, 1), lambda b, g, i: (direction, g, 0, 0)),
    ]
    args = [xbc_s, xbc_s, xbc_s, dt_t, bias_t, alog_t]
    row_spec = pl.BlockSpec((None, t_len, gw), lambda b, g, i: (b, cidx(i), g))
    vec_spec = pl.BlockSpec((1, gw), lambda b, g, i: (0, g))
    scratch = [pltpu.VMEM((n_state, gw), F32)]
    if final:
        y0, z, norm_g = extra
        in_specs += [row_spec, row_spec, vec_spec]
        args += [y0, z, norm_g]
        out_dtype = BF16
        scratch.append(pltpu.VMEM((t_len, gw), F32))
    else:
        (dsk,) = extra
        in_specs += [vec_spec]
        args += [dsk]
        out_dtype = F32
    return pl.pallas_call(
        functools.partial(_ssd_pass_body, direction=direction, hg=hg, final=final),
        grid=(batch, groups, n_chunks), in_specs=in_specs, out_specs=row_spec,
        out_shape=jax.ShapeDtypeStruct((batch, lt, d_inner), out_dtype), scratch_shapes=scratch,
        compiler_params=_cparams(("parallel", "parallel", "arbitrary")), name=f"ssd_pass{direction}")(*args)


def _mixer_ssd(h, w_in, conv_w, conv_b, dt_bias, a_log, d_skip, norm_g, *, batch, lt, n_lat):
    rows = h.shape[0]
    groups = SSD_GROUPS
    n_heads = dt_bias.shape[-1]
    d_inner = n_heads * SSD_HEAD_DIM
    hg = n_heads // groups
    gn = groups * SSD_STATE
    tm = _pick(lt, (768, 512, 384, 256, 128))
    tn = _pick(math.gcd(d_inner, 2 * gn), (512, 256, 128))
    z = _matmul(h, w_in, batch=batch, col_blocks=[0], n_out=d_inner, tm=tm, tn=tn, name="ssd_in_z")
    xbc = _matmul(h, w_in, batch=batch, col_blocks=[d_inner // tn], n_out=d_inner + 2 * gn, tm=tm, tn=tn,
                  name="ssd_in_xbc")
    tdt = _pick(2 * n_heads, (256, 128, 64, 32))
    dt = _matmul(h, w_in, batch=batch, col_blocks=[(2 * d_inner + 2 * gn) // tdt], n_out=2 * n_heads, tm=tm, tn=tdt,
                 name="ssd_in_dt")
    xbc_s = _conv_silu(xbc, conv_w, conv_b, batch=batch, lt=lt, n_lat=n_lat)
    dt_t = dt.reshape(batch, lt, 2, groups, hg).transpose(2, 3, 0, 4, 1)
    bias_t = dt_bias.astype(F32).reshape(2, groups, hg, 1)
    alog_t = a_log.astype(F32).reshape(2, groups, hg, 1)
    dsk = jnp.repeat(d_skip.astype(F32), SSD_HEAD_DIM).reshape(1, d_inner)
    kw = dict(batch=batch, lt=lt, n_lat=n_lat, d_inner=d_inner)
    y0 = _ssd_pass(0, xbc_s, dt_t, bias_t, alog_t, (dsk,), final=False, **kw)
    g = _ssd_pass(1, xbc_s, dt_t, bias_t, alog_t,
                  (y0, z.reshape(batch, lt, d_inner), norm_g.reshape(1, d_inner)), final=True, **kw)
    return g.reshape(rows, d_inner)


def _s5_scan_body(u_ref, bbd_ref, cbd_ref, a_ref, y_ref, xs_ref, s_ref):
    d = pl.program_id(0)
    t_len, n_b, width = u_ref.shape
    ns = a_ref.shape[-1] // 2

    @pl.when(pl.program_id(2) == 0)
    def _():
        s_ref[...] = jnp.zeros_like(s_ref)

    tb = 32
    for r in range(t_len // tb):
        u = u_ref[r * tb:(r + 1) * tb].reshape(tb * n_b, width).astype(BF16)
        xs_ref[r * tb * n_b:(r + 1) * tb * n_b, :] = jnp.dot(u, bbd_ref[...], preferred_element_type=F32)
    ar = a_ref[:, :ns]
    ai = a_ref[:, ns:]

    def step(k, carry):
        sr, si = carry
        t = jnp.where(d == 0, k, t_len - 1 - k)
        r0 = pl.multiple_of(t * n_b, n_b)
        xr = xs_ref[pl.ds(r0, n_b), :ns]
        xi = xs_ref[pl.ds(r0, n_b), ns:]
        nr = ar * sr - ai * si + xr
        ni = ar * si + ai * sr + xi
        xs_ref[pl.ds(r0, n_b), :ns] = nr
        xs_ref[pl.ds(r0, n_b), ns:] = ni
        return nr, ni

    sr, si = lax.fori_loop(0, t_len, step, (s_ref[:, :ns], s_ref[:, ns:]), unroll=2)
    s_ref[:, :ns] = sr
    s_ref[:, ns:] = si
    for r in range(t_len // tb):
        y = jnp.dot(xs_ref[r * tb * n_b:(r + 1) * tb * n_b, :].astype(BF16), cbd_ref[...], preferred_element_type=F32)
        y_ref[r * tb:(r + 1) * tb] = y.reshape(tb, n_b, width)


def _s5_merge_body(u_ref, y0_ref, y1_ref, dsk_ref, o_ref, g_ref):
    tt, n_b, tc = u_ref.shape
    y = dsk_ref[...] * u_ref[...] + y0_ref[...] + y1_ref[...]
    g = 0.5 * y * (1.0 + jnp.tanh(math.sqrt(2.0 / math.pi) * (y + 0.044715 * (y * y * y))))
    g = g.reshape(tt * n_b, tc)
    for cc in range(tc // LANES):
        g_ref[cc] = g[:, cc * LANES:(cc + 1) * LANES]
    for b in range(n_b):
        for cc in range(tc // LANES):
            o_ref[b, :, cc * LANES:(cc + 1) * LANES] = g_ref[cc, pl.ds(b, tt, stride=n_b), :].astype(o_ref.dtype)


def _s5_params(lam_re, lam_im, log_dt, b_re, b_im, c_re, c_im, n_b):
    n_groups, n_state, gsz = b_re.shape
    sg = S5_SLAB_GROUPS
    n_slab = n_groups // sg
    eye = jnp.eye(sg, dtype=F32)
    bbds, a_s = [], []
    br, bi = b_re.astype(F32), b_im.astype(F32)
    for d in range(2):
        lr = lam_re[d].astype(F32)
        li = lam_im[d].astype(F32)
        step = jnp.exp(log_dt[d].astype(F32))[:, None]
        mag = jnp.exp(lr * step)
        ar, ai = mag * jnp.cos(li * step), mag * jnp.sin(li * step)
        den = lr * lr + li * li
        fr = ((ar - 1.0) * lr + ai * li) / den
        fi = (ai * lr - (ar - 1.0) * li) / den
        bbr = fr[..., None] * br - fi[..., None] * bi
        bbi = fr[..., None] * bi + fi[..., None] * br

        def bdiag_in(m):
            m = m.reshape(n_slab, sg, n_state, gsz)
            return jnp.einsum("sgpc,gh->sgchp", m, eye).reshape(n_slab, sg * gsz, sg * n_state)

        bbds.append(jnp.concatenate([bdiag_in(bbr), bdiag_in(bbi)], -1))
        a_cat = jnp.concatenate([ar.reshape(n_slab, sg * n_state), ai.reshape(n_slab, sg * n_state)], -1)
        a_s.append(jnp.broadcast_to(a_cat[:, None, :], (n_slab, n_b, 2 * sg * n_state)))

    def bdiag_out(m):
        m = m.reshape(n_slab, sg, gsz, n_state)
        return jnp.einsum("sgcp,gh->sgphc", m, eye).reshape(n_slab, sg * n_state, sg * gsz)

    cbd = jnp.concatenate([bdiag_out(c_re.astype(F32)), -bdiag_out(c_im.astype(F32))], 1)
    return jnp.stack(bbds).astype(BF16), cbd.astype(BF16), jnp.stack(a_s)


def _mixer_s5(h, w_in, lam_re, lam_im, log_dt, b_re, b_im, c_re, c_im, d_skip, *, batch, lt, n_lat):
    rows, d = h.shape
    width = w_in.shape[1]
    tm = _pick(lt, (768, 512, 384, 256, 128))
    tn = _pick(width, (512, 256, 128))
    tpb = lt // tm
    ncb = width // tn
    u = _matmul(h, w_in, batch=batch, col_blocks=[0], n_out=width, tm=tm, tn=tn, out_shape=(lt, batch * width),
                out_map=lambda i, j, k: (i % tpb, (i // tpb) * ncb + j), name="s5_in")
    bbd, cbd, a_bc = _s5_params(lam_re, lam_im, log_dt, b_re, b_im, c_re, c_im, batch)
    n_slab, sw = bbd.shape[1], bbd.shape[2]
    ns2 = bbd.shape[3]
    t_len = S5_CHUNK
    lat_chunks, ctx_chunks = n_lat // t_len, (lt - n_lat) // t_len
    n_chunks = lat_chunks + ctx_chunks

    def cidx(dd, i):
        fwd = jnp.where(i < ctx_chunks, lat_chunks + i, i - ctx_chunks)
        return jnp.where(dd == 0, fwd, n_chunks - 1 - i)

    u3 = u.reshape(lt, batch, width)
    ys = pl.pallas_call(
        _s5_scan_body, grid=(2, n_slab, n_chunks),
        in_specs=[pl.BlockSpec((t_len, batch, sw), lambda dd, s, i: (cidx(dd, i), 0, s)),
                  pl.BlockSpec((None, None, sw, ns2), lambda dd, s, i: (dd, s, 0, 0)),
                  pl.BlockSpec((None, ns2, sw), lambda dd, s, i: (s, 0, 0)),
                  pl.BlockSpec((None, None, batch, ns2), lambda dd, s, i: (dd, s, 0, 0))],
        out_specs=pl.BlockSpec((None, t_len, batch, sw), lambda dd, s, i: (dd, cidx(dd, i), 0, s)),
        out_shape=jax.ShapeDtypeStruct((2, lt, batch, width), F32),
        scratch_shapes=[pltpu.VMEM((t_len * batch, ns2), F32), pltpu.VMEM((batch, ns2), F32)],
        compiler_params=_cparams(("parallel", "parallel", "arbitrary")), name="s5_scan")(u3, bbd, cbd, a_bc)
    tt = _pick(lt, (128,))
    tc = _pick(width, (1024, 512, 256, 128))
    g = pl.pallas_call(
        _s5_merge_body, grid=(lt // tt, width // tc),
        in_specs=[pl.BlockSpec((tt, batch, tc), lambda t, j: (t, 0, j)),
                  pl.BlockSpec((None, tt, batch, tc), lambda t, j: (0, t, 0, j)),
                  pl.BlockSpec((None, tt, batch, tc), lambda t, j: (1, t, 0, j)),
                  pl.BlockSpec((1, 1, tc), lambda t, j: (0, 0, j))],
        out_specs=pl.BlockSpec((batch, tt, tc), lambda t, j: (0, t, j)),
        out_shape=jax.ShapeDtypeStruct((batch, lt, width), BF16),
        scratch_shapes=[pltpu.VMEM((tc // LANES, tt * batch, LANES), F32)],
        compiler_params=_cparams(("parallel", "parallel")), name="s5_merge")(
            u3, ys, ys, d_skip.astype(F32).reshape(1, 1, width))
    return g.reshape(rows, width)


def _mla_norm_body(p_ref, qg_ref, kvg_ref, cq_ref, ckv_ref, kr_ref):
    def rms(t, g):
        return t * lax.rsqrt(jnp.mean(t * t, axis=-1, keepdims=True) + NORM_EPS) * g

    cq_ref[...] = rms(p_ref[:, :MLA_Q_RANK], qg_ref[...]).astype(cq_ref.dtype)
    ckv_ref[...] = rms(p_ref[:, MLA_Q_RANK:MLA_Q_RANK + MLA_KV_RANK], kvg_ref[...]).astype(ckv_ref.dtype)
    kr_ref[...] = p_ref[:, MLA_Q_RANK + MLA_KV_RANK:MLA_Q_RANK + MLA_KV_RANK + LANES]


def _mla_prep_body(q_ref, kv_ref, kr_ref, cos_ref, sin_ref, qg_ref, kg_ref, qo_ref, ko_ref, vo_ref, *, heads):
    dk = MLA_NOPE + MLA_ROPE
    hw = 2 * LANES
    scale = dk ** -0.5
    cos, sin = cos_ref[...], sin_ref[...]
    perm = _rope_perm(MLA_ROPE // 4)
    qg_n, qg_r = qg_ref[:, :LANES], qg_ref[:, LANES:]
    kg_n, kg_r = kg_ref[:, :LANES], kg_ref[:, LANES:]
    kr = kr_ref[...]
    kr_sq = kr * kr
    kr_rot = _rope(kr * kg_r, cos, sin, perm)
    for h in range(heads):
        c0 = h * hw
        qn = q_ref[:, c0:c0 + LANES]
        qr = q_ref[:, c0 + LANES:c0 + hw]
        rinv = lax.rsqrt(_rowsum_lanes(qn * qn + qr * qr) / dk + NORM_EPS)
        qo_ref[:, c0:c0 + LANES] = (qn * rinv * qg_n * scale).astype(BF16)
        qo_ref[:, c0 + LANES:c0 + hw] = (_rope(qr * rinv * qg_r, cos, sin, perm) * scale).astype(BF16)
        kn = kv_ref[:, c0:c0 + LANES]
        rinv = lax.rsqrt(_rowsum_lanes(kn * kn + kr_sq) / dk + NORM_EPS)
        ko_ref[:, c0:c0 + LANES] = (kn * rinv * kg_n).astype(BF16)
        ko_ref[:, c0 + LANES:c0 + hw] = (kr_rot * rinv).astype(BF16)
        vo_ref[:, h * LANES:(h + 1) * LANES] = kv_ref[:, c0 + LANES:c0 + hw].astype(BF16)


def _mla_attn_body(q_ref, k_ref, v_ref, o_ref, *, sub):
    k = k_ref[...]
    v = v_ref[...]
    for qi in range(q_ref.shape[0] // sub):
        sl = slice(qi * sub, (qi + 1) * sub)
        s = _nt_dot(q_ref[sl, :], k)
        m = jnp.max(s, axis=-1, keepdims=True)
        p = jnp.exp(s - m)
        denom = jnp.sum(p, axis=-1, keepdims=True)
        o_ref[sl, :] = (jnp.dot(p.astype(BF16), v, preferred_element_type=F32) / denom).astype(o_ref.dtype)


def _mixer_mla(h, w_in, q_a_g, kv_a_g, w_uq, w_ukv, q_g, k_g, *, batch, lt, n_lat):
    rows, d = h.shape
    heads = MLA_HEADS
    dk = MLA_NOPE + MLA_ROPE
    hw = 2 * LANES
    n_ctx = lt - n_lat
    n_in = MLA_Q_RANK + MLA_KV_RANK + MLA_ROPE
    n_in_pad = MLA_Q_RANK + MLA_KV_RANK + 2 * LANES
    w_in_p = jnp.pad(w_in, ((0, 0), (0, n_in_pad - n_in))).astype(BF16)
    w_uq_p = jnp.pad(w_uq.reshape(MLA_Q_RANK, heads, dk), ((0, 0), (0, 0), (0, hw - dk))).reshape(
        MLA_Q_RANK, heads * hw).astype(BF16)
    w_ukv_b = w_ukv.astype(BF16)
    qg_p = jnp.pad(q_g.astype(F32), (0, hw - dk)).reshape(1, hw)
    kg_p = jnp.pad(k_g.astype(F32), (0, hw - dk)).reshape(1, hw)
    tm = _pick(lt, (768, 512, 384, 256, 128))
    p = _matmul(h, w_in_p, batch=batch, col_blocks=[0], n_out=n_in_pad, tm=tm, tn=_pick(n_in_pad, (256, 128)),
                name="mla_in")
    tp = _pick(rows, (512, 256, 128))
    cq, ckv, kr = pl.pallas_call(
        _mla_norm_body, grid=(rows // tp,),
        in_specs=[pl.BlockSpec((tp, n_in_pad), lambda i: (i, 0)),
                  pl.BlockSpec((1, MLA_Q_RANK), lambda i: (0, 0)),
                  pl.BlockSpec((1, MLA_KV_RANK), lambda i: (0, 0))],
        out_specs=[pl.BlockSpec((tp, MLA_Q_RANK), lambda i: (i, 0)),
                   pl.BlockSpec((tp, MLA_KV_RANK), lambda i: (i, 0)),
                   pl.BlockSpec((tp, LANES), lambda i: (i, 0))],
        out_shape=[jax.ShapeDtypeStruct((rows, MLA_Q_RANK), BF16),
                   jax.ShapeDtypeStruct((rows, MLA_KV_RANK), BF16),
                   jax.ShapeDtypeStruct((rows, LANES), F32)],
        compiler_params=_cparams(("parallel",)), name="mla_norm")(
            p, q_a_g.reshape(1, MLA_Q_RANK), kv_a_g.reshape(1, MLA_KV_RANK))
    tn = _pick(heads * hw, (512, 256))
    q_full = _matmul(cq, w_uq_p, batch=batch, col_blocks=[0], n_out=heads * hw, tm=tm, tn=tn, name="mla_uq")
    kv = _matmul(ckv, w_ukv_b, batch=batch, col_blocks=[0], n_out=heads * hw, tm=tm, tn=tn, name="mla_ukv")
    cos, sin = _rope_tables(n_lat, n_ctx, MLA_ROPE, LANES)
    tr = _pick(math.gcd(lt, n_lat), (256, 128))
    trb = lt // tr
    hb = _pick(heads, (8, 4, 2, 1))
    q_cat, k_cat, v = pl.pallas_call(
        functools.partial(_mla_prep_body, heads=hb), grid=(rows // tr, heads // hb),
        in_specs=[pl.BlockSpec((tr, hb * hw), lambda i, j: (i, j)),
                  pl.BlockSpec((tr, hb * hw), lambda i, j: (i, j)),
                  pl.BlockSpec((tr, LANES), lambda i, j: (i, 0)),
                  pl.BlockSpec((tr, LANES), lambda i, j: (i % trb, 0)),
                  pl.BlockSpec((tr, LANES), lambda i, j: (i % trb, 0)),
                  pl.BlockSpec((1, hw), lambda i, j: (0, 0)),
                  pl.BlockSpec((1, hw), lambda i, j: (0, 0))],
        out_specs=[pl.BlockSpec((tr, hb * hw), lambda i, j: (i, j)),
                   pl.BlockSpec((tr, hb * hw), lambda i, j: (i, j)),
                   pl.BlockSpec((tr, hb * LANES), lambda i, j: (i, j))],
        out_shape=[jax.ShapeDtypeStruct((rows, heads * hw), BF16),
                   jax.ShapeDtypeStruct((rows, heads * hw), BF16),
                   jax.ShapeDtypeStruct((rows, heads * LANES), BF16)],
        compiler_params=_cparams(("parallel", "parallel")), name="mla_prep")(q_full, kv, kr, cos, sin, qg_p, kg_p)
    tq = _pick(n_lat, (1024, 512, 256, 128))
    sub = min(tq, 256)
    q3 = q_cat.reshape(batch, lt, heads * hw)
    k3 = k_cat.reshape(batch, lt, heads * hw)
    v3 = v.reshape(batch, lt, heads * LANES)
    o = pl.pallas_call(
        functools.partial(_mla_attn_body, sub=sub), grid=(batch, heads, n_lat // tq),
        in_specs=[pl.BlockSpec((None, tq, hw), lambda b, hh, i: (b, i, hh)),
                  pl.BlockSpec((None, lt, hw), lambda b, hh, i: (b, 0, hh)),
                  pl.BlockSpec((None, lt, LANES), lambda b, hh, i: (b, 0, hh))],
        out_specs=pl.BlockSpec((None, tq, LANES), lambda b, hh, i: (b, i, hh)),
        out_shape=jax.ShapeDtypeStruct((batch, n_lat, heads * LANES), BF16),
        compiler_params=_cparams(("parallel", "parallel", "arbitrary")), name="mla_attn")(q3, k3, v3)
    return o.reshape(batch * n_lat, heads * LANES)


def kernel(x, c, ctx, c_ctx, mod_down, mod_up, mod_b, norm1_g, norm2_g, swa_w_in, swa_q_g, swa_k_g, swa_sinks, swa_w_out, ssd_w_in, ssd_conv_w, ssd_conv_b, ssd_dt_bias, ssd_a_log, ssd_d, ssd_norm_g, ssd_w_out, s5_w_in, s5_lam_re, s5_lam_im, s5_log_dt, s5_b_re, s5_b_im, s5_c_re, s5_c_im, s5_d, s5_w_glu, mla_w_in, mla_q_a_g, mla_kv_a_g, mla_w_uq, mla_w_ukv, mla_q_g, mla_k_g, mla_w_out, moe_w_group, moe_b_group, moe_w_expert, moe_b_expert, moe_w1, moe_w3, moe_w2):
    batch, n_lat, d = x.shape
    n_ctx = ctx.shape[1]
    lt = n_lat + n_ctx
    rows = batch * lt
    depth = mod_down.shape[0]
    dims = dict(batch=batch, lt=lt, n_lat=n_lat)
    tm = _pick(lt, (768, 512, 384, 256, 128))
    tn = _pick(d, (512, 256, 128))

    xs = jnp.concatenate([x, ctx], axis=1).reshape(rows, d)

    pad_rows = -(batch + 1) % SUBLANES
    cvecs = jnp.concatenate([c, c_ctx[None], jnp.zeros((pad_rows, d), F32)], axis=0)
    mod_all = _adaln(cvecs, mod_down, mod_up, mod_b).reshape(depth, batch + 1 + pad_rows, 6, d)

    n_moe_logits = MOE_GROUPS + moe_w_expert.shape[-1]
    router_w = jnp.pad(jnp.concatenate([moe_w_group, moe_w_expert], -1), ((0, 0), (0, 0), (0, LANES - n_moe_logits)))
    router_b = jnp.pad(jnp.concatenate([moe_b_group, moe_b_expert], -1), ((0, 0), (0, LANES - n_moe_logits)))

    def out_proj(o, w, res, mods4, name, last, epilogue="residual", cols=(0,)):
        tk = _pick(o.shape[1], (8192, 4096, 2048, 1024, 512, 256, 128))
        use_rows = n_lat if last else lt
        tm_o = _pick(n_lat, (1024, 512, 256, 128)) if last else tm
        return _matmul(o, w.astype(BF16), batch=batch, col_blocks=list(cols), n_out=d, tm=tm_o, tn=tn, tk=tk,
                       epilogue=epilogue, res=res, mods=mods4, gate_idx=2, use_rows=use_rows, n_lat=n_lat, name=name)

    for i in range(depth):
        kind, slot = i % 4, i // 4
        last = i == depth - 1
        ml = mod_all[i, :batch]
        mc = jnp.broadcast_to(mod_all[i, batch][None], (batch, 6, d))
        mods4 = jnp.stack([ml, mc], axis=1)
        mods2 = mods4.reshape(batch * 2, 6, d)
        h = _modnorm(xs, norm1_g[i], mods2, shift_idx=0, scale_idx=1, lt=lt, n_lat=n_lat, out_dtype=BF16)
        if kind == 0:
            o = _mixer_swa(h, swa_w_in[slot].astype(BF16), swa_q_g[slot], swa_k_g[slot], swa_sinks[slot], **dims)
            xs = out_proj(o, swa_w_out[slot], xs, mods4, "swa_out", last)
        elif kind == 1:
            o = _mixer_ssd(h, ssd_w_in[slot].astype(BF16), ssd_conv_w[slot], ssd_conv_b[slot], ssd_dt_bias[slot],
                           ssd_a_log[slot], ssd_d[slot], ssd_norm_g[slot], **dims)
            xs = out_proj(o, ssd_w_out[slot], xs, mods4, "ssd_out", last)
        elif kind == 2:
            o = _mixer_s5(h, s5_w_in[slot].astype(BF16), s5_lam_re[slot], s5_lam_im[slot], s5_log_dt[slot],
                          s5_b_re[slot], s5_b_im[slot], s5_c_re[slot], s5_c_im[slot], s5_d[slot], **dims)
            xs = out_proj(o, s5_w_glu[slot], xs, mods4, "s5_glu", last, epilogue="glu_residual", cols=(0, d // tn))
        else:
            if not last:
                raise NotImplementedError("the MLA mixer is implemented for the last layer only (latent queries)")
            o = _mixer_mla(h, mla_w_in[slot], mla_q_a_g[slot], mla_kv_a_g[slot], mla_w_uq[slot], mla_w_ukv[slot],
                           mla_q_g[slot], mla_k_g[slot], **dims)
            xs = out_proj(o, mla_w_out[slot], xs, mods4, "mla_out", last)
        xs = _moe_layer(xs, mods2, norm2_g[i], router_w[i], router_b[i].reshape(1, LANES), moe_w1, moe_w3, moe_w2, i,
                        lt=n_lat if last else lt, n_lat=n_lat)
    return xs.reshape(batch, n_lat, d)
```

```python
import functools
import math

import jax
import jax.numpy as jnp
from jax import lax
from jax.experimental import pallas as pl
from jax.experimental.pallas import tpu as pltpu

F32 = jnp.float32
BF16 = jnp.bfloat16
HIGHEST = lax.Precision.HIGHEST

GRID_W = 64
ROPE_BASE = 10000.0
NORM_EPS = 1e-6

SWA_HEADS = 32
SWA_KV_HEADS = 8
SWA_HEAD_DIM = 128
SWA_WINDOW = 128
SWA_BLOCK = 128

SSD_HEAD_DIM = 64
SSD_GROUPS = 8
SSD_STATE = 128
SSD_CHUNK = 128

S5_GROUP = 16
S5_STATE = 64
S5_CHUNK = 256
S5_SLAB_GROUPS = 16

MLA_HEADS = 32
MLA_Q_RANK = 1024
MLA_KV_RANK = 512
MLA_NOPE = 128
MLA_ROPE = 64
MLA_V = 128

MOE_GROUPS = 4
MOE_PER_GROUP = 8
MOE_TOPK = 2
MOE_BLOCK = 256

LANES = 128
SUBLANES = 8
VMEM_LIMIT = 56 * 1024 * 1024


def _cparams(sem, vmem=VMEM_LIMIT):
    return pltpu.CompilerParams(dimension_semantics=sem, vmem_limit_bytes=vmem)


def _pick(n, cands):
    for c in cands:
        if n % c == 0:
            return c
    raise ValueError(f"no tile in {cands} divides {n}")


def _sigmoid(x):
    return 1.0 / (1.0 + jnp.exp(-x))


def _silu(x):
    return x * _sigmoid(x)


def _softplus(x):
    return jnp.maximum(x, 0.0) + jnp.log1p(jnp.exp(-jnp.abs(x)))


def _pack_bf16_pairs(t):
    n = t.shape[1] // 2

    def rne_bits(v):
        u = lax.bitcast_convert_type(v, jnp.uint32)
        return u + jnp.uint32(0x7FFF) + ((u >> 16) & jnp.uint32(1))

    return (rne_bits(t[:, :n]) >> 16) | (rne_bits(t[:, n:]) & jnp.uint32(0xFFFF0000))


def _unpack_bf16_pairs(u):
    lo = lax.bitcast_convert_type(u << 16, F32)
    hi = lax.bitcast_convert_type(u & jnp.uint32(0xFFFF0000), F32)
    return lo, hi


def _nt_dot(a, b):
    return lax.dot_general(a, b, (((1,), (1,)), ((), ())), preferred_element_type=F32)


def _mm_body(*refs, n_w, nk, epilogue, tm, tiles_per_batch, n_lat, gate_idx):
    a_ref = refs[0]
    w_refs = refs[1:1 + n_w]
    pos = 1 + n_w
    res_ref = mod_ref = None
    if epilogue in ("residual", "glu_residual"):
        res_ref, mod_ref = refs[pos], refs[pos + 1]
        pos += 2
    o_ref = refs[pos]
    acc_refs = refs[pos + 1:]

    def finish(vals):
        val = vals[0] * _sigmoid(vals[1]) if epilogue in ("glu", "glu_residual") else vals[0]
        if res_ref is not None:
            row = (pl.program_id(0) % tiles_per_batch) * tm + lax.broadcasted_iota(jnp.int32, (tm, 1), 0)
            gate = jnp.where(row >= n_lat, mod_ref[1, pl.ds(gate_idx, 1), :], mod_ref[0, pl.ds(gate_idx, 1), :])
            val = res_ref[...] + gate * val
        o_ref[...] = val.astype(o_ref.dtype)

    a = a_ref[...]
    if nk == 1:
        finish([jnp.dot(a, w[...], preferred_element_type=F32) for w in w_refs])
        return
    k = pl.program_id(2)

    @pl.when(k == 0)
    def _():
        for acc in acc_refs:
            acc[...] = jnp.zeros_like(acc)

    for w, acc in zip(w_refs, acc_refs):
        acc[...] += jnp.dot(a, w[...], preferred_element_type=F32)

    @pl.when(k == nk - 1)
    def _():
        finish([acc[...] for acc in acc_refs])


def _matmul(a, w, *, batch, col_blocks, n_out, tm, tn, tk=None, out_dtype=F32, epilogue="store", res=None, mods=None,
            gate_idx=0, use_rows=None, n_lat=None, out_shape=None, out_map=None, name="matmul"):
    k_dim = a.shape[1]
    a3 = a.reshape(batch, a.shape[0] // batch, k_dim)
    use_rows = use_rows or a3.shape[1]
    tk = tk or k_dim
    nk = k_dim // tk
    n_w = len(col_blocks)
    tu = use_rows // tm
    grid = (batch * tu, n_out // tn, nk)
    in_specs = [pl.BlockSpec((None, tm, tk), lambda i, j, k: (i // tu, i % tu, k))]
    args = [a3]
    for off in col_blocks:
        in_specs.append(pl.BlockSpec((tk, tn), lambda i, j, k, off=off: (k, j + off)))
        args.append(w)
    if res is not None:
        in_specs.append(pl.BlockSpec((None, tm, tn), lambda i, j, k: (i // tu, i % tu, j)))
        in_specs.append(pl.BlockSpec((None, 2, 6, tn), lambda i, j, k: (i // tu, 0, 0, j)))
        args += [res.reshape(batch, res.shape[0] // batch, res.shape[1]), mods]
    out_shape = out_shape or (batch * use_rows, n_out)
    out_map = out_map or (lambda i, j, k: (i, j))
    scratch = [pltpu.VMEM((tm, tn), F32) for _ in range(n_w)] if nk > 1 else []
    body = functools.partial(_mm_body, n_w=n_w, nk=nk, epilogue=epilogue, tm=tm, tiles_per_batch=tu,
                             n_lat=n_lat, gate_idx=gate_idx)
    return pl.pallas_call(
        body, grid=grid, in_specs=in_specs, out_specs=pl.BlockSpec((tm, tn), out_map),
        out_shape=jax.ShapeDtypeStruct(out_shape, out_dtype), scratch_shapes=scratch,
        compiler_params=_cparams(("parallel", "parallel", "arbitrary")), name=name)(*args)


def _adaln_body(c_ref, wd_ref, wu_ref, b_ref, o_ref, t_ref):
    @pl.when(pl.program_id(1) == 0)
    def _():
        cv = c_ref[...]
        t_ref[...] = jnp.dot(_silu(cv), wd_ref[...], precision=HIGHEST, preferred_element_type=F32)

    o_ref[...] = jnp.dot(t_ref[...], wu_ref[...], precision=HIGHEST, preferred_element_type=F32) + b_ref[...]


def _adaln(cvecs, mod_down, mod_up, mod_b):
    depth, d, rank = mod_down.shape
    n6 = mod_up.shape[-1]
    rows = cvecs.shape[0]
    tn = _pick(n6, (2048, 1024, 512, 256, 128))
    return pl.pallas_call(
        _adaln_body, grid=(depth, n6 // tn),
        in_specs=[pl.BlockSpec((rows, d), lambda l, j: (0, 0)),
                  pl.BlockSpec((None, d, rank), lambda l, j: (l, 0, 0)),
                  pl.BlockSpec((None, rank, tn), lambda l, j: (l, 0, j)),
                  pl.BlockSpec((None, 1, tn), lambda l, j: (l, 0, j))],
        out_specs=pl.BlockSpec((None, rows, tn), lambda l, j: (l, 0, j)),
        out_shape=jax.ShapeDtypeStruct((depth, rows, n6), F32),
        scratch_shapes=[pltpu.VMEM((rows, rank), F32)],
        compiler_params=_cparams(("parallel", "arbitrary")), name="adaln")(
            cvecs, mod_down, mod_up, mod_b.reshape(depth, 1, n6))


def _modnorm_body(*refs, shift_idx, scale_idx, router):
    if router:
        x_ref, g_ref, mod_ref, wh_ref, wl_ref, br_ref, h_ref, ids_ref, gates_ref, hb_ref = refs
    else:
        x_ref, g_ref, mod_ref, h_ref = refs
    tm, d = x_ref.shape
    half = d // 2
    rb = 16
    cw = min(4 * LANES, half)

    def row_block(r, carry):
        rows = pl.ds(pl.multiple_of(r * rb, rb), rb)
        ss = jnp.zeros((rb, 1), F32)
        for c in range(d // cw):
            xc = x_ref[rows, c * cw:(c + 1) * cw]
            ss = ss + jnp.sum(xc * xc, axis=-1, keepdims=True)
        rinv = lax.rsqrt(ss / d + NORM_EPS)

        def h_chunk(c0):
            cols = slice(c0, c0 + cw)
            gs = g_ref[:, cols] * (1.0 + mod_ref[pl.ds(scale_idx, 1), cols])
            return x_ref[rows, cols] * rinv * gs + mod_ref[pl.ds(shift_idx, 1), cols]

        for c in range(half // cw):
            pieces = []
            for c0 in (c * cw, half + c * cw):
                h = h_chunk(c0)
                if not router:
                    h_ref[rows, c0:c0 + cw] = h.astype(h_ref.dtype)
                    continue
                h_b = h.astype(BF16)
                hb_ref[rows, c0:c0 + cw] = h_b
                pieces.append(lax.bitcast_convert_type(h_b.astype(F32), jnp.uint32))
            if router:
                h_ref[rows, c * cw:(c + 1) * cw] = (pieces[0] >> 16) | pieces[1]
        return carry

    lax.fori_loop(0, tm // rb, row_block, 0, unroll=4)
    if not router:
        return
    h_b = hb_ref[...]
    logits = (jnp.dot(h_b, wh_ref[...], preferred_element_type=F32)
              + jnp.dot(h_b, wl_ref[...], preferred_element_type=F32)) + br_ref[...]
    lane = lax.broadcasted_iota(jnp.int32, logits.shape, 1)
    lane_f = lane.astype(F32)
    neg = -jnp.inf

    def first_lane(hit):
        return jnp.min(jnp.where(hit, lane_f, float(LANES)), axis=-1, keepdims=True).astype(jnp.int32)

    is_grp = lane < MOE_GROUPS
    lg = jnp.where(is_grp, logits, neg)
    mg = jnp.max(lg, axis=-1, keepdims=True)
    grp = first_lane(lg == mg)
    p_grp = 1.0 / jnp.sum(jnp.where(is_grp, jnp.exp(lg - mg), 0.0), axis=-1, keepdims=True)
    lo = MOE_GROUPS + grp * MOE_PER_GROUP
    le = jnp.where((lane >= lo) & (lane < lo + MOE_PER_GROUP), logits, neg)
    m1 = jnp.max(le, axis=-1, keepdims=True)
    i1 = first_lane(le == m1)
    le2 = jnp.where(lane == i1, neg, le)
    m2 = jnp.max(le2, axis=-1, keepdims=True)
    i2 = first_lane(le2 == m2)
    e2 = jnp.exp(m2 - m1)
    g1 = p_grp / (1.0 + e2)
    g2 = p_grp * e2 / (1.0 + e2)
    ids_ref[...] = jnp.where(lane == 0, i1 - MOE_GROUPS, jnp.where(lane == 1, i2 - MOE_GROUPS, 0))
    gates_ref[...] = jnp.where(lane == 0, g1, jnp.where(lane == 1, g2, 0.0))


def _modnorm(x, g, mods2, *, shift_idx, scale_idx, lt, n_lat, out_dtype, router_w=None, router_b=None):
    rows, d = x.shape
    tm = _pick(math.gcd(lt, n_lat), (256, 128))
    tpb = lt // tm
    lat_tiles = n_lat // tm
    router = router_w is not None
    in_specs = [pl.BlockSpec((tm, d), lambda i: (i, 0)),
                pl.BlockSpec((1, d), lambda i: (0, 0)),
                pl.BlockSpec((None, 6, d), lambda i: ((i // tpb) * 2 + ((i % tpb) >= lat_tiles).astype(jnp.int32), 0, 0))]
    args = [x, g.reshape(1, d), mods2]
    out_specs = [pl.BlockSpec((tm, d), lambda i: (i, 0))]
    out_shape = [jax.ShapeDtypeStruct((rows, d), out_dtype)]
    if router:
        out_specs = [pl.BlockSpec((tm, d // 2), lambda i: (i, 0))]
        out_shape = [jax.ShapeDtypeStruct((rows, d // 2), jnp.uint32)]
        w_spec = pl.BlockSpec((d, LANES), lambda i: (0, 0))
        in_specs += [w_spec, w_spec, pl.BlockSpec((1, LANES), lambda i: (0, 0))]
        w_hi = router_w.astype(BF16)
        args += [w_hi, (router_w - w_hi.astype(F32)).astype(BF16), router_b]
        out_specs += [pl.BlockSpec((tm, LANES), lambda i: (i, 0))] * 2
        out_shape += [jax.ShapeDtypeStruct((rows, LANES), jnp.int32), jax.ShapeDtypeStruct((rows, LANES), F32)]
    scratch = [pltpu.VMEM((tm, d), BF16)] if router else []
    body = functools.partial(_modnorm_body, shift_idx=shift_idx, scale_idx=scale_idx, router=router)
    out = pl.pallas_call(body, grid=(rows // tm,), in_specs=in_specs, out_specs=out_specs, out_shape=out_shape,
                         scratch_shapes=scratch,
                         compiler_params=_cparams(("parallel",)), name="modnorm_router" if router else "modnorm")(*args)
    return out if router else out[0]


def _row_gather_start(idx_ref, base, n, src_hbm, dst, sem, dst_base=0, idx_stride=1):
    def body(r, c):
        row = idx_ref[base + r * idx_stride]
        pltpu.make_async_copy(src_hbm.at[pl.ds(row, 1)], dst.at[pl.ds(dst_base + r, 1)], sem).start()
        return c

    lax.fori_loop(0, n, body, 0, unroll=8)


def _row_gather_wait(n, src_hbm, dst, sem):
    def body(r, c):
        pltpu.make_async_copy(src_hbm.at[pl.ds(0, 1)], dst.at[pl.ds(r, 1)], sem).wait()
        return c

    lax.fori_loop(0, n, body, 0, unroll=8)


def _moe_ffn_body(src_ref, blk_e_ref, n_used_ref, h_hbm, w1_ref, w3_ref, w2_ref, o_ref,
                  xbuf, xb, w1b, w3b, w2b, sems, *, blk):
    i = pl.program_id(0)
    slot = i % 2
    n_used = n_used_ref[0]
    half = xb.shape[1] // 2

    @pl.when(i == 0)
    def _():
        _row_gather_start(src_ref, 0, blk, h_hbm, xbuf.at[0], sems.at[0])

    @pl.when(i + 1 < n_used)
    def _():
        _row_gather_start(src_ref, (i + 1) * blk, blk, h_hbm, xbuf.at[1 - slot], sems.at[1 - slot])

    @pl.when(i < n_used)
    def _():
        @pl.when(jnp.logical_or(i == 0, blk_e_ref[i] != blk_e_ref[jnp.maximum(i - 1, 0)]))
        def _():
            w1b[...] = w1_ref[...].astype(BF16)
            w3b[...] = w3_ref[...].astype(BF16)
            w2b[...] = w2_ref[...].astype(BF16)

        _row_gather_wait(blk, h_hbm, xbuf.at[slot], sems.at[slot])
        lo, hi = _unpack_bf16_pairs(xbuf[slot])
        xb[:, :half] = lo.astype(BF16)
        xb[:, half:] = hi.astype(BF16)
        x = xb[...]
        a1 = jnp.dot(x, w1b[...], preferred_element_type=F32)
        a3 = jnp.dot(x, w3b[...], preferred_element_type=F32)
        mid = (_silu(a1) * a3).astype(BF16)
        o_ref[...] = _pack_bf16_pairs(jnp.dot(mid, w2b[...], preferred_element_type=F32))

    @pl.when(i >= n_used)
    def _():
        o_ref[...] = jnp.zeros_like(o_ref)


def _moe_ffn(hpk, src, blk_e, n_used, w1, w3, w2, layer, *, blk):
    n_rows = src.shape[0]
    n_blk = n_rows // blk
    half = hpk.shape[1]
    d = 2 * half
    hid = w1.shape[-1]
    grid_spec = pltpu.PrefetchScalarGridSpec(
        num_scalar_prefetch=3, grid=(n_blk,),
        in_specs=[pl.BlockSpec(memory_space=pl.ANY),
                  pl.BlockSpec((None, None, d, hid), lambda i, s, e, n: (layer, e[i], 0, 0)),
                  pl.BlockSpec((None, None, d, hid), lambda i, s, e, n: (layer, e[i], 0, 0)),
                  pl.BlockSpec((None, None, hid, d), lambda i, s, e, n: (layer, e[i], 0, 0))],
        out_specs=pl.BlockSpec((blk, half), lambda i, s, e, n: (i, 0)),
        scratch_shapes=[pltpu.VMEM((2, blk, half), jnp.uint32), pltpu.VMEM((blk, d), BF16),
                        pltpu.VMEM((d, hid), BF16), pltpu.VMEM((d, hid), BF16), pltpu.VMEM((hid, d), BF16),
                        pltpu.SemaphoreType.DMA((2,))])
    return pl.pallas_call(
        functools.partial(_moe_ffn_body, blk=blk), grid_spec=grid_spec,
        out_shape=jax.ShapeDtypeStruct((n_rows, half), jnp.uint32),
        compiler_params=_cparams(("arbitrary",)), name="moe_ffn")(src, blk_e, n_used, hpk, w1, w3, w2)


def _moe_combine_body(pos_ref, x_ref, gates_ref, mod_ref, yb_hbm, o_ref, ybuf, sems, *, tm, n_tiles):
    i = pl.program_id(0)
    slot = i % 2
    half = x_ref.shape[1] // 2

    def start(tile, s):
        for kk in range(MOE_TOPK):
            _row_gather_start(pos_ref, tile * tm * MOE_TOPK + kk, tm, yb_hbm, ybuf.at[s], sems.at[s],
                              dst_base=kk * tm, idx_stride=MOE_TOPK)

    @pl.when(i == 0)
    def _():
        start(0, 0)

    @pl.when(i + 1 < n_tiles)
    def _():
        start(i + 1, 1 - slot)

    _row_gather_wait(MOE_TOPK * tm, yb_hbm, ybuf.at[slot], sems.at[slot])
    ys = ybuf.at[slot]
    rb, cb = 32, min(4 * LANES, half)
    for r in range(tm // rb):
        rows = pl.ds(r * rb, rb)
        g0 = jnp.broadcast_to(gates_ref[rows, 0:1], (rb, cb))
        g1 = jnp.broadcast_to(gates_ref[rows, 1:2], (rb, cb))
        for c in range(half // cb):
            lo0, hi0 = _unpack_bf16_pairs(ys[rows, c * cb:(c + 1) * cb])
            lo1, hi1 = _unpack_bf16_pairs(ys[pl.ds(tm + r * rb, rb), c * cb:(c + 1) * cb])
            for off, y in ((c * cb, g0 * lo0 + g1 * lo1), (half + c * cb, g0 * hi0 + g1 * hi1)):
                o_ref[rows, off:off + cb] = x_ref[rows, off:off + cb] + mod_ref[pl.ds(5, 1), off:off + cb] * y


def _moe_combine(x, gates, mods2, yb, pos, *, lt, n_lat):
    rows, d = x.shape
    tm = _pick(math.gcd(lt, n_lat), (256, 128))
    tpb = lt // tm
    lat_tiles = n_lat // tm
    n_tiles = rows // tm
    grid_spec = pltpu.PrefetchScalarGridSpec(
        num_scalar_prefetch=1, grid=(n_tiles,),
        in_specs=[pl.BlockSpec((tm, d), lambda i, p: (i, 0)),
                  pl.BlockSpec((tm, LANES), lambda i, p: (i, 0)),
                  pl.BlockSpec((None, 6, d),
                               lambda i, p: ((i // tpb) * 2 + ((i % tpb) >= lat_tiles).astype(jnp.int32), 0, 0)),
                  pl.BlockSpec(memory_space=pl.ANY)],
        out_specs=pl.BlockSpec((tm, d), lambda i, p: (i, 0)),
        scratch_shapes=[pltpu.VMEM((2, MOE_TOPK * tm, d // 2), jnp.uint32), pltpu.SemaphoreType.DMA((2,))])
    return pl.pallas_call(
        functools.partial(_moe_combine_body, tm=tm, n_tiles=n_tiles), grid_spec=grid_spec,
        out_shape=jax.ShapeDtypeStruct((rows, d), F32),
        compiler_params=_cparams(("arbitrary",)), name="moe_combine")(pos, x, gates, mods2, yb)


def _moe_layer(x, mods2, norm_g, router_w, router_b, w1, w3, w2, layer, *, lt, n_lat):
    rows, d = x.shape
    n_exp = w1.shape[1]
    hpk, ids, gates = _modnorm(x, norm_g, mods2, shift_idx=3, scale_idx=4, lt=lt, n_lat=n_lat, out_dtype=F32,
                               router_w=router_w, router_b=router_b)
    blk = MOE_BLOCK
    n_asg = rows * MOE_TOPK
    n_blk = n_asg // blk + n_exp
    flat = ids[:, :MOE_TOPK].reshape(-1)
    order = jnp.argsort(flat).astype(jnp.int32)
    sorted_e = flat[order]
    counts = jnp.sum((flat[:, None] == jnp.arange(n_exp, dtype=jnp.int32)[None, :]).astype(jnp.int32), axis=0)
    padded = (counts + blk - 1) // blk * blk
    pad_end = jnp.cumsum(padded)
    pad_start = pad_end - padded
    start = jnp.cumsum(counts) - counts
    dest = (pad_start[sorted_e] + jnp.arange(n_asg, dtype=jnp.int32) - start[sorted_e]).astype(jnp.int32)
    blk_first = jnp.arange(n_blk, dtype=jnp.int32) * blk
    blk_e = jnp.minimum(jnp.sum((pad_end[None, :] <= blk_first[:, None]).astype(jnp.int32), axis=1), n_exp - 1)
    n_used = (pad_end[-1:] // blk).astype(jnp.int32)
    within = jnp.arange(n_blk * blk, dtype=jnp.int32) - jnp.repeat(pad_start[blk_e], blk)
    valid = within < jnp.repeat(counts[blk_e], blk)
    sorted_idx = jnp.clip(jnp.repeat(start[blk_e], blk) + within, 0, n_asg - 1)
    src = jnp.where(valid, order[sorted_idx] // MOE_TOPK, 0).astype(jnp.int32)
    pos = dest[jnp.argsort(order)]
    yb = _moe_ffn(hpk, src, blk_e.astype(jnp.int32), n_used, w1, w3, w2, layer, blk=blk)
    return _moe_combine(x, gates, mods2, yb, pos.astype(jnp.int32), lt=lt, n_lat=n_lat)


def _rope_tables(n_lat, n_ctx, rot_dim, width):
    rows = n_lat // GRID_W
    row = jnp.repeat(jnp.arange(rows, dtype=F32), GRID_W)
    col = jnp.tile(jnp.arange(GRID_W, dtype=F32), rows)
    axis_dim = rot_dim // 2
    inv_freq = ROPE_BASE ** (-jnp.arange(0, axis_dim, 2, dtype=F32) / axis_dim)
    a0 = row[:, None] * inv_freq
    a1 = col[:, None] * inv_freq
    cos = jnp.concatenate([jnp.cos(a0), jnp.cos(a0), jnp.cos(a1), jnp.cos(a1)], -1)
    sin = jnp.concatenate([-jnp.sin(a0), jnp.sin(a0), -jnp.sin(a1), jnp.sin(a1)], -1)
    cos = jnp.pad(cos, ((0, n_ctx), (0, width - rot_dim)), constant_values=1.0)
    sin = jnp.pad(sin, ((0, n_ctx), (0, width - rot_dim)))
    return cos, sin


def _split2(t):
    hi = t.astype(BF16)
    return hi, (t - hi.astype(F32)).astype(BF16)


def _rowsum_lanes(t):
    ones = jnp.ones((LANES, LANES), BF16)
    hi, lo = _split2(t)
    return jnp.dot(hi, ones, preferred_element_type=F32) + jnp.dot(lo, ones, preferred_element_type=F32)


def _rope_perm(half):
    src = lax.broadcasted_iota(jnp.int32, (LANES, LANES), 0)
    dst = lax.broadcasted_iota(jnp.int32, (LANES, LANES), 1)
    partner = jnp.where((dst & (2 * half - 1)) < half, dst + half, dst - half)
    return (src == partner).astype(BF16)


def _rope(x, cos, sin, perm):
    hi, lo = _split2(x)
    partner = jnp.dot(hi, perm, preferred_element_type=F32) + jnp.dot(lo, perm, preferred_element_type=F32)
    return x * cos + partner * sin


def _swa_prep_body(p_ref, cos_ref, sin_ref, qg_ref, kg_ref, q_ref, k_ref, v_ref, *, n_q, n_kv):
    hd = SWA_HEAD_DIM
    cos, sin = cos_ref[...], sin_ref[...]
    scale = hd ** -0.5

    perm = _rope_perm(hd // 4)

    def norm_rope(t, g):
        t = t * lax.rsqrt(_rowsum_lanes(t * t) / hd + NORM_EPS) * g
        return _rope(t, cos, sin, perm)

    for h in range(n_q):
        q_ref[:, h * hd:(h + 1) * hd] = (norm_rope(p_ref[:, h * hd:(h + 1) * hd], qg_ref[...]) * scale).astype(BF16)
    for h in range(n_kv):
        c0 = (n_q + h) * hd
        k_ref[:, h * hd:(h + 1) * hd] = norm_rope(p_ref[:, c0:c0 + hd], kg_ref[...]).astype(BF16)
    v_ref[...] = p_ref[:, (n_q + n_kv) * hd:].astype(BF16)


def _swa_attn_body(sink_ref, q_ref, *refs, n_lat, grp, nq):
    nw = nq + 2
    k_refs, v_refs, o_ref = refs[:nw + 1], refs[nw + 1:2 * nw + 2], refs[2 * nw + 2]
    kv = pl.program_id(1)
    first = pl.program_id(2) * nq
    blk = SWA_BLOCK
    hd = SWA_HEAD_DIM
    n_ctx = k_refs[nw].shape[0]
    is_lat = first < n_lat // blk
    qpos = first * blk + lax.broadcasted_iota(jnp.int32, (nq * blk, 1), 0)
    biases = []
    for w in range(nw):
        kpos = (first - 1 + w) * blk + lax.broadcasted_iota(jnp.int32, (1, blk), 1)
        valid = (jnp.abs(kpos - qpos) <= SWA_WINDOW) & (kpos >= 0) & (kpos < n_lat) & is_lat
        biases.append(jnp.where(valid, 0.0, -jnp.inf))
    biases.append(jnp.zeros((nq * blk, n_ctx), F32))
    bias = jnp.concatenate(biases, axis=1)
    k_all = jnp.concatenate([r[...] for r in k_refs], axis=0)
    v_all = jnp.concatenate([r[...] for r in v_refs], axis=0)
    for g in range(grp):
        sink = sink_ref[kv * grp + g]
        s = _nt_dot(q_ref[:, g * hd:(g + 1) * hd], k_all) + bias
        m = jnp.maximum(jnp.max(s, axis=-1, keepdims=True), sink)
        p = jnp.exp(s - m)
        denom = jnp.sum(p, axis=-1, keepdims=True) + jnp.exp(sink - m)
        o = jnp.dot(p.astype(BF16), v_all, preferred_element_type=F32) / denom
        o_ref[:, g * hd:(g + 1) * hd] = o.astype(o_ref.dtype)


def _mixer_swa(h, w_in, q_g, k_g, sinks, *, batch, lt, n_lat):
    rows = h.shape[0]
    hd = SWA_HEAD_DIM
    n_q, n_kv = SWA_HEADS, SWA_KV_HEADS
    grp = n_q // n_kv
    n_ctx = lt - n_lat
    n_cols = (n_q + 2 * n_kv) * hd
    tm = _pick(lt, (768, 512, 384, 256, 128))
    tn = _pick(n_cols, (1024, 512, 256, 128))
    p = _matmul(h, w_in, batch=batch, col_blocks=[0], n_out=n_cols, tm=tm, tn=tn, name="swa_in")
    cos, sin = _rope_tables(n_lat, n_ctx, hd, hd)
    tp = _pick(math.gcd(lt, n_lat), (256, 128))
    tpb = lt // tp
    q, k, v = pl.pallas_call(
        functools.partial(_swa_prep_body, n_q=n_q, n_kv=n_kv), grid=(rows // tp,),
        in_specs=[pl.BlockSpec((tp, n_cols), lambda i: (i, 0)),
                  pl.BlockSpec((tp, hd), lambda i: (i % tpb, 0)),
                  pl.BlockSpec((tp, hd), lambda i: (i % tpb, 0)),
                  pl.BlockSpec((1, hd), lambda i: (0, 0)),
                  pl.BlockSpec((1, hd), lambda i: (0, 0))],
        out_specs=[pl.BlockSpec((tp, n_q * hd), lambda i: (i, 0)),
                   pl.BlockSpec((tp, n_kv * hd), lambda i: (i, 0)),
                   pl.BlockSpec((tp, n_kv * hd), lambda i: (i, 0))],
        out_shape=[jax.ShapeDtypeStruct((rows, n_q * hd), BF16),
                   jax.ShapeDtypeStruct((rows, n_kv * hd), BF16),
                   jax.ShapeDtypeStruct((rows, n_kv * hd), BF16)],
        compiler_params=_cparams(("parallel",)), name="swa_prep")(p, cos, sin, q_g.reshape(1, hd), k_g.reshape(1, hd))

    blk = SWA_BLOCK
    bpb = lt // blk
    lat_blocks = n_lat // blk

    nq = 2
    steps = bpb // nq

    def win(off):
        return lambda b, kvh, j: (b * bpb + jnp.clip(j * nq + off, 0, lat_blocks - 1), kvh)

    ctx_spec = pl.BlockSpec((None, n_ctx, hd), lambda b, kvh, j: (b, n_lat // n_ctx, kvh))
    k3 = k.reshape(batch, lt, n_kv * hd)
    v3 = v.reshape(batch, lt, n_kv * hd)
    kv_spec = [pl.BlockSpec((blk, hd), win(off)) for off in range(-1, nq + 1)]
    q_spec = pl.BlockSpec((nq * blk, grp * hd), lambda b, kvh, j: (b * steps + j, kvh))
    n_win = len(kv_spec)
    return pl.pallas_call(
        functools.partial(_swa_attn_body, n_lat=n_lat, grp=grp, nq=nq), grid=(batch, n_kv, steps),
        in_specs=[pl.BlockSpec(memory_space=pltpu.SMEM), q_spec] + kv_spec + [ctx_spec] + kv_spec + [ctx_spec],
        out_specs=q_spec,
        out_shape=jax.ShapeDtypeStruct((rows, n_q * hd), BF16),
        compiler_params=_cparams(("parallel", "parallel", "arbitrary")), name="swa_attn")(
            sinks.astype(F32), q, *([k] * n_win), k3, *([v] * n_win), v3)


def _conv_silu_body(x_ref, prev_ref, next_ref, w_ref, b_ref, o_ref, *, lat_tiles, n_tiles):
    t = pl.program_id(1)
    x = x_ref[...]
    rows = x.shape[0]
    has_prev = jnp.logical_and(t != 0, t != lat_tiles).astype(F32)
    has_next = jnp.logical_and(t != lat_tiles - 1, t != n_tiles - 1).astype(F32)
    row = lax.broadcasted_iota(jnp.int32, x.shape, 0)
    x_m = jnp.where(row == 0, prev_ref[SUBLANES - 1:SUBLANES, :] * has_prev, pltpu.roll(x, 1, 0))
    x_p = jnp.where(row == rows - 1, next_ref[0:1, :] * has_next, pltpu.roll(x, rows - 1, 0))
    y = w_ref[0:1, :] * x_m + w_ref[1:2, :] * x + w_ref[2:3, :] * x_p + b_ref[...]
    o_ref[...] = _silu(y).astype(o_ref.dtype)


def _conv_silu(xbc, conv_w, conv_b, *, batch, lt, n_lat):
    cols = xbc.shape[-1]
    x3 = xbc.reshape(batch, lt, cols)
    tt = _pick(math.gcd(lt, n_lat), (256, 128))
    tc = _pick(cols, (2048, 1024, 512, 256, 128))
    n_tiles = lt // tt
    sub = tt // SUBLANES
    out = pl.pallas_call(
        functools.partial(_conv_silu_body, lat_tiles=n_lat // tt, n_tiles=n_tiles),
        grid=(batch, n_tiles, cols // tc),
        in_specs=[pl.BlockSpec((None, tt, tc), lambda b, t, j: (b, t, j)),
                  pl.BlockSpec((None, SUBLANES, tc), lambda b, t, j: (b, jnp.maximum(t * sub - 1, 0), j)),
                  pl.BlockSpec((None, SUBLANES, tc), lambda b, t, j: (b, jnp.minimum((t + 1) * sub, lt // SUBLANES - 1), j)),
                  pl.BlockSpec((3, tc), lambda b, t, j: (0, j)),
                  pl.BlockSpec((1, tc), lambda b, t, j: (0, j))],
        out_specs=pl.BlockSpec((None, tt, tc), lambda b, t, j: (b, t, j)),
        out_shape=jax.ShapeDtypeStruct((batch, lt, cols), BF16),
        compiler_params=_cparams(("parallel", "parallel", "parallel")), name="ssd_conv")(
            x3, x3, x3, conv_w, conv_b.reshape(1, cols))
    return out


def _ssd_pass_body(*refs, direction, hg, final):
    if final:
        x_ref, b_ref, c_ref, dtt_ref, biast_ref, alogt_ref, y0_ref, z_ref, ng_ref, o_ref, state_ref, g_ref = refs
    else:
        x_ref, b_ref, c_ref, dtt_ref, biast_ref, alogt_ref, dsk_ref, o_ref, state_ref = refs
    t_len = x_ref.shape[0]
    n_state = b_ref.shape[1]

    @pl.when(pl.program_id(2) == 0)
    def _():
        state_ref[...] = jnp.zeros_like(state_ref)

    bm = b_ref[...]
    cm = c_ref[...]
    dt_t = _softplus(dtt_ref[...] + biast_ref[...])
    v_t = dt_t * (-jnp.exp(alogt_ref[...]))
    row = lax.broadcasted_iota(jnp.int32, (t_len, t_len), 0)
    col = lax.broadcasted_iota(jnp.int32, (t_len, t_len), 1)
    tri = (col <= row) if direction == 0 else (col >= row)
    cum_t = lax.dot_general(v_t, tri.astype(F32), (((1,), (1,)), ((), ())), precision=HIGHEST,
                            preferred_element_type=F32)
    total_t = jnp.sum(v_t, axis=1, keepdims=True)
    w_t = dt_t * jnp.exp(total_t - cum_t)
    cb = _nt_dot(cm, bm)
    b_t = bm.astype(F32).T
    left = lax.broadcasted_iota(jnp.int32, (t_len, LANES), 1) < SSD_HEAD_DIM
    left_n = lax.broadcasted_iota(jnp.int32, (n_state, LANES), 1) < SSD_HEAD_DIM

    def split3(t):
        hi = t.astype(BF16).astype(F32)
        mid = (t - hi).astype(BF16).astype(F32)
        return hi, mid, (t - hi - mid).astype(BF16).astype(F32)

    def spread(width):
        hrow = lax.broadcasted_iota(jnp.int32, (hg, hg * width), 0)
        hcol = lax.broadcasted_iota(jnp.int32, (hg, hg * width), 1)
        return jnp.logical_and(hcol >= hrow * width, hcol < (hrow + 1) * width).astype(F32)

    def tn_dot(a_rows, w_rows):
        pad = LANES - a_rows.shape[0]
        a = jnp.concatenate([a_rows, jnp.zeros((pad, a_rows.shape[1]), F32)], axis=0)
        w = jnp.concatenate([w_rows, jnp.zeros((pad, w_rows.shape[1]), F32)], axis=0)
        a_cols = jnp.concatenate([a[:, j * LANES:(j + 1) * LANES].T for j in range(a.shape[1] // LANES)], axis=0)
        return jnp.dot(a_cols.astype(BF16), w.astype(BF16), preferred_element_type=F32)

    e_t = spread(t_len)
    cum_pieces = split3(cum_t)
    seg_all = tn_dot(
        jnp.concatenate(list(cum_pieces) + [jnp.ones((3 * hg, t_len), F32)], axis=0),
        jnp.concatenate([e_t] * 3 + [-jnp.tile(p, (1, hg)) * e_t for p in cum_pieces], axis=0))
    e_x = spread(SSD_HEAD_DIM)
    cols = jnp.concatenate([jnp.exp(cum_t), jnp.broadcast_to(jnp.exp(total_t), (hg, LANES))], axis=1)
    cols_x = tn_dot(jnp.concatenate(split3(cols), axis=0), jnp.concatenate([e_x] * 3, axis=0))
    exp_cum_x = cols_x[:t_len]
    exp_total_x = cols_x[t_len:t_len + 1]

    for pr in range(hg // 2):
        h0 = 2 * pr
        sl = slice(pr * LANES, (pr + 1) * LANES)
        xb = x_ref[:, sl]
        ys, ss = [], []
        for hh in (h0, h0 + 1):
            dec = jnp.exp(jnp.where(tri, seg_all[:, hh * t_len:(hh + 1) * t_len], -jnp.inf))
            ys.append(jnp.dot((cb * dec * dt_t[hh:hh + 1, :]).astype(BF16), xb, preferred_element_type=F32))
            ss.append(jnp.dot((b_t * w_t[hh:hh + 1, :]).astype(BF16), xb, preferred_element_type=F32))
        st = state_ref[:, sl]
        y = jnp.where(left, ys[0], ys[1]) + jnp.dot(cm, st.astype(BF16), preferred_element_type=F32) * exp_cum_x[:, sl]
        state_ref[:, sl] = st * exp_total_x[:, sl] + jnp.where(left_n, ss[0], ss[1])
        if final:
            z = z_ref[:, sl]
            g_ref[:, sl] = (y0_ref[:, sl] + y) * _silu(z)
        else:
            o_ref[:, sl] = dsk_ref[:, sl] * xb.astype(F32) + y
    if final:
        g = g_ref[...]
        o_ref[...] = (g * lax.rsqrt(jnp.mean(g * g, axis=-1, keepdims=True) + NORM_EPS) * ng_ref[...]).astype(o_ref.dtype)


def _chunk_order(i, direction, lat_chunks, ctx_chunks):
    if direction == 0:
        return jnp.where(i < ctx_chunks, lat_chunks + i, i - ctx_chunks)
    return lat_chunks + ctx_chunks - 1 - i


def _ssd_pass(direction, xbc_s, dt_t, bias_t, alog_t, extra, *, batch, lt, n_lat, d_inner, final):
    groups = SSD_GROUPS
    gw = d_inner // groups
    hg = gw // SSD_HEAD_DIM
    n_state = SSD_STATE
    t_len = SSD_CHUNK
    lat_chunks, ctx_chunks = n_lat // t_len, (lt - n_lat) // t_len
    n_chunks = lat_chunks + ctx_chunks
    xb = d_inner // n_state

    def cidx(i):
        return _chunk_order(i, direction, lat_chunks, ctx_chunks)

    in_specs = [
        pl.BlockSpec((None, t_len, gw), lambda b, g, i: (b, cidx(i), g)),
        pl.BlockSpec((None, t_len, n_state), lambda b, g, i: (b, cidx(i), xb + g)),
        pl.BlockSpec((None, t_len, n_state), lambda b, g, i: (b, cidx(i), xb + groups + g)),
        pl.BlockSpec((None, None, None, hg, t_len), lambda b, g, i: (direction, g, b, 0, cidx(i))),
        pl.BlockSpec((None, None, hg, 1), lambda b, g, i: (direction, g, 0, 0)),
        pl.BlockSpec((None, None, hg, 1), lambda b, g, i: (direction, g, 0, 0)),
    ]
    args = [xbc_s, xbc_s, xbc_s, dt_t, bias_t, alog_t]
    row_spec = pl.BlockSpec((None, t_len, gw), lambda b, g, i: (b, cidx(i), g))
    vec_spec = pl.BlockSpec((1, gw), lambda b, g, i: (0, g))
    scratch = [pltpu.VMEM((n_state, gw), F32)]
    if final:
        y0, z, norm_g = extra
        in_specs += [row_spec, row_spec, vec_spec]
        args += [y0, z, norm_g]
        out_dtype = BF16
        scratch.append(pltpu.VMEM((t_len, gw), F32))
    else:
        (dsk,) = extra
        in_specs += [vec_spec]
        args += [dsk]
        out_dtype = F32
    return pl.pallas_call(
        functools.partial(_ssd_pass_body, direction=direction, hg=hg, final=final),
        grid=(batch, groups, n_chunks), in_specs=in_specs, out_specs=row_spec,
        out_shape=jax.ShapeDtypeStruct((batch, lt, d_inner), out_dtype), scratch_shapes=scratch,
        compiler_params=_cparams(("parallel", "parallel", "arbitrary")), name=f"ssd_pass{direction}")(*args)


def _mixer_ssd(h, w_in, conv_w, conv_b, dt_bias, a_log, d_skip, norm_g, *, batch, lt, n_lat):
    rows = h.shape[0]
    groups = SSD_GROUPS
    n_heads = dt_bias.shape[-1]
    d_inner = n_heads * SSD_HEAD_DIM
    hg = n_heads // groups
    gn = groups * SSD_STATE
    tm = _pick(lt, (768, 512, 384, 256, 128))
    tn = _pick(math.gcd(d_inner, 2 * gn), (1024, 512, 256, 128))
    z = _matmul(h, w_in, batch=batch, col_blocks=[0], n_out=d_inner, tm=tm, tn=tn, name="ssd_in_z")
    xbc = _matmul(h, w_in, batch=batch, col_blocks=[d_inner // tn], n_out=d_inner + 2 * gn, tm=tm, tn=tn,
                  name="ssd_in_xbc")
    tdt = _pick(2 * n_heads, (256, 128, 64, 32))
    dt = _matmul(h, w_in, batch=batch, col_blocks=[(2 * d_inner + 2 * gn) // tdt], n_out=2 * n_heads, tm=tm, tn=tdt,
                 name="ssd_in_dt")
    xbc_s = _conv_silu(xbc, conv_w, conv_b, batch=batch, lt=lt, n_lat=n_lat)
    dt_t = dt.reshape(batch, lt, 2, groups, hg).transpose(2, 3, 0, 4, 1)
    bias_t = dt_bias.astype(F32).reshape(2, groups, hg, 1)
    alog_t = a_log.astype(F32).reshape(2, groups, hg, 1)
    dsk = jnp.repeat(d_skip.astype(F32), SSD_HEAD_DIM).reshape(1, d_inner)
    kw = dict(batch=batch, lt=lt, n_lat=n_lat, d_inner=d_inner)
    y0 = _ssd_pass(0, xbc_s, dt_t, bias_t, alog_t, (dsk,), final=False, **kw)
    g = _ssd_pass(1, xbc_s, dt_t, bias_t, alog_t,
                  (y0, z.reshape(batch, lt, d_inner), norm_g.reshape(1, d_inner)), final=True, **kw)
    return g.reshape(rows, d_inner)


def _s5_scan_body(u_ref, bbd_ref, cbd_ref, a_ref, y_ref, xs_ref, s_ref):
    d = pl.program_id(0)
    t_len, n_b, width = u_ref.shape
    ns = a_ref.shape[-1] // 2

    @pl.when(pl.program_id(2) == 0)
    def _():
        s_ref[...] = jnp.zeros_like(s_ref)

    tb = 32
    for r in range(t_len // tb):
        u = u_ref[r * tb:(r + 1) * tb].reshape(tb * n_b, width).astype(BF16)
        xs_ref[r * tb * n_b:(r + 1) * tb * n_b, :] = jnp.dot(u, bbd_ref[...], preferred_element_type=F32)
    ar = a_ref[:, :ns]
    ai = a_ref[:, ns:]

    def step(k, carry):
        sr, si = carry
        t = jnp.where(d == 0, k, t_len - 1 - k)
        r0 = pl.multiple_of(t * n_b, n_b)
        xr = xs_ref[pl.ds(r0, n_b), :ns]
        xi = xs_ref[pl.ds(r0, n_b), ns:]
        nr = ar * sr - ai * si + xr
        ni = ar * si + ai * sr + xi
        xs_ref[pl.ds(r0, n_b), :ns] = nr
        xs_ref[pl.ds(r0, n_b), ns:] = ni
        return nr, ni

    sr, si = lax.fori_loop(0, t_len, step, (s_ref[:, :ns], s_ref[:, ns:]), unroll=4)
    s_ref[:, :ns] = sr
    s_ref[:, ns:] = si
    for r in range(t_len // tb):
        y = jnp.dot(xs_ref[r * tb * n_b:(r + 1) * tb * n_b, :].astype(BF16), cbd_ref[...], preferred_element_type=F32)
        y_ref[r * tb:(r + 1) * tb] = y.reshape(tb, n_b, width)


def _s5_merge_body(u_ref, y0_ref, y1_ref, dsk_ref, o_ref, g_ref):
    tt, n_b, tc = u_ref.shape
    y = dsk_ref[...] * u_ref[...] + y0_ref[...] + y1_ref[...]
    g = 0.5 * y * (1.0 + jnp.tanh(math.sqrt(2.0 / math.pi) * (y + 0.044715 * (y * y * y))))
    g = g.reshape(tt * n_b, tc)
    for cc in range(tc // LANES):
        g_ref[cc] = g[:, cc * LANES:(cc + 1) * LANES]
    for b in range(n_b):
        for cc in range(tc // LANES):
            o_ref[b, :, cc * LANES:(cc + 1) * LANES] = g_ref[cc, pl.ds(b, tt, stride=n_b), :].astype(o_ref.dtype)


def _s5_params(lam_re, lam_im, log_dt, b_re, b_im, c_re, c_im, n_b):
    n_groups, n_state, gsz = b_re.shape
    sg = S5_SLAB_GROUPS
    n_slab = n_groups // sg
    eye = jnp.eye(sg, dtype=F32)
    bbds, a_s = [], []
    br, bi = b_re.astype(F32), b_im.astype(F32)
    for d in range(2):
        lr = lam_re[d].astype(F32)
        li = lam_im[d].astype(F32)
        step = jnp.exp(log_dt[d].astype(F32))[:, None]
        mag = jnp.exp(lr * step)
        ar, ai = mag * jnp.cos(li * step), mag * jnp.sin(li * step)
        den = lr * lr + li * li
        fr = ((ar - 1.0) * lr + ai * li) / den
        fi = (ai * lr - (ar - 1.0) * li) / den
        bbr = fr[..., None] * br - fi[..., None] * bi
        bbi = fr[..., None] * bi + fi[..., None] * br

        def bdiag_in(m):
            m = m.reshape(n_slab, sg, n_state, gsz)
            return jnp.einsum("sgpc,gh->sgchp", m, eye).reshape(n_slab, sg * gsz, sg * n_state)

        bbds.append(jnp.concatenate([bdiag_in(bbr), bdiag_in(bbi)], -1))
        a_cat = jnp.concatenate([ar.reshape(n_slab, sg * n_state), ai.reshape(n_slab, sg * n_state)], -1)
        a_s.append(jnp.broadcast_to(a_cat[:, None, :], (n_slab, n_b, 2 * sg * n_state)))

    def bdiag_out(m):
        m = m.reshape(n_slab, sg, gsz, n_state)
        return jnp.einsum("sgcp,gh->sgphc", m, eye).reshape(n_slab, sg * n_state, sg * gsz)

    cbd = jnp.concatenate([bdiag_out(c_re.astype(F32)), -bdiag_out(c_im.astype(F32))], 1)
    return jnp.stack(bbds).astype(BF16), cbd.astype(BF16), jnp.stack(a_s)


def _mixer_s5(h, w_in, lam_re, lam_im, log_dt, b_re, b_im, c_re, c_im, d_skip, *, batch, lt, n_lat):
    rows, d = h.shape
    width = w_in.shape[1]
    tm = _pick(lt, (768, 512, 384, 256, 128))
    tn = _pick(width, (1024, 512, 256, 128))
    tpb = lt // tm
    ncb = width // tn
    u = _matmul(h, w_in, batch=batch, col_blocks=[0], n_out=width, tm=tm, tn=tn, out_shape=(lt, batch * width),
                out_map=lambda i, j, k: (i % tpb, (i // tpb) * ncb + j), name="s5_in")
    bbd, cbd, a_bc = _s5_params(lam_re, lam_im, log_dt, b_re, b_im, c_re, c_im, batch)
    n_slab, sw = bbd.shape[1], bbd.shape[2]
    ns2 = bbd.shape[3]
    t_len = S5_CHUNK
    lat_chunks, ctx_chunks = n_lat // t_len, (lt - n_lat) // t_len
    n_chunks = lat_chunks + ctx_chunks

    def cidx(dd, i):
        fwd = jnp.where(i < ctx_chunks, lat_chunks + i, i - ctx_chunks)
        return jnp.where(dd == 0, fwd, n_chunks - 1 - i)

    u3 = u.reshape(lt, batch, width)
    ys = pl.pallas_call(
        _s5_scan_body, grid=(2, n_slab, n_chunks),
        in_specs=[pl.BlockSpec((t_len, batch, sw), lambda dd, s, i: (cidx(dd, i), 0, s)),
                  pl.BlockSpec((None, None, sw, ns2), lambda dd, s, i: (dd, s, 0, 0)),
                  pl.BlockSpec((None, ns2, sw), lambda dd, s, i: (s, 0, 0)),
                  pl.BlockSpec((None, None, batch, ns2), lambda dd, s, i: (dd, s, 0, 0))],
        out_specs=pl.BlockSpec((None, t_len, batch, sw), lambda dd, s, i: (dd, cidx(dd, i), 0, s)),
        out_shape=jax.ShapeDtypeStruct((2, lt, batch, width), F32),
        scratch_shapes=[pltpu.VMEM((t_len * batch, ns2), F32), pltpu.VMEM((batch, ns2), F32)],
        compiler_params=_cparams(("parallel", "parallel", "arbitrary")), name="s5_scan")(u3, bbd, cbd, a_bc)
    tt = _pick(lt, (128,))
    tc = _pick(width, (1024, 512, 256, 128))
    g = pl.pallas_call(
        _s5_merge_body, grid=(lt // tt, width // tc),
        in_specs=[pl.BlockSpec((tt, batch, tc), lambda t, j: (t, 0, j)),
                  pl.BlockSpec((None, tt, batch, tc), lambda t, j: (0, t, 0, j)),
                  pl.BlockSpec((None, tt, batch, tc), lambda t, j: (1, t, 0, j)),
                  pl.BlockSpec((1, 1, tc), lambda t, j: (0, 0, j))],
        out_specs=pl.BlockSpec((batch, tt, tc), lambda t, j: (0, t, j)),
        out_shape=jax.ShapeDtypeStruct((batch, lt, width), BF16),
        scratch_shapes=[pltpu.VMEM((tc // LANES, tt * batch, LANES), F32)],
        compiler_params=_cparams(("parallel", "parallel")), name="s5_merge")(
            u3, ys, ys, d_skip.astype(F32).reshape(1, 1, width))
    return g.reshape(rows, width)


def _mla_norm_body(p_ref, qg_ref, kvg_ref, cq_ref, ckv_ref, kr_ref):
    def rms(t, g):
        return t * lax.rsqrt(jnp.mean(t * t, axis=-1, keepdims=True) + NORM_EPS) * g

    cq_ref[...] = rms(p_ref[:, :MLA_Q_RANK], qg_ref[...]).astype(cq_ref.dtype)
    ckv_ref[...] = rms(p_ref[:, MLA_Q_RANK:MLA_Q_RANK + MLA_KV_RANK], kvg_ref[...]).astype(ckv_ref.dtype)
    kr_ref[...] = p_ref[:, MLA_Q_RANK + MLA_KV_RANK:MLA_Q_RANK + MLA_KV_RANK + LANES]


def _mla_prep_body(q_ref, kv_ref, kr_ref, cos_ref, sin_ref, qg_ref, kg_ref, qo_ref, ko_ref, vo_ref, *, heads):
    dk = MLA_NOPE + MLA_ROPE
    hw = 2 * LANES
    scale = dk ** -0.5
    cos, sin = cos_ref[...], sin_ref[...]
    perm = _rope_perm(MLA_ROPE // 4)
    qg_n, qg_r = qg_ref[:, :LANES], qg_ref[:, LANES:]
    kg_n, kg_r = kg_ref[:, :LANES], kg_ref[:, LANES:]
    kr = kr_ref[...]
    kr_sq = kr * kr
    kr_rot = _rope(kr * kg_r, cos, sin, perm)
    for h in range(heads):
        c0 = h * hw
        qn = q_ref[:, c0:c0 + LANES]
        qr = q_ref[:, c0 + LANES:c0 + hw]
        rinv = lax.rsqrt(_rowsum_lanes(qn * qn + qr * qr) / dk + NORM_EPS)
        qo_ref[:, c0:c0 + LANES] = (qn * rinv * qg_n * scale).astype(BF16)
        qo_ref[:, c0 + LANES:c0 + hw] = (_rope(qr * rinv * qg_r, cos, sin, perm) * scale).astype(BF16)
        kn = kv_ref[:, c0:c0 + LANES]
        rinv = lax.rsqrt(_rowsum_lanes(kn * kn + kr_sq) / dk + NORM_EPS)
        ko_ref[:, c0:c0 + LANES] = (kn * rinv * kg_n).astype(BF16)
        ko_ref[:, c0 + LANES:c0 + hw] = (kr_rot * rinv).astype(BF16)
        vo_ref[:, h * LANES:(h + 1) * LANES] = kv_ref[:, c0 + LANES:c0 + hw].astype(BF16)


def _mla_attn_body(q_ref, k_ref, v_ref, o_ref, *, sub):
    k = k_ref[...]
    v = v_ref[...]
    for qi in range(q_ref.shape[0] // sub):
        sl = slice(qi * sub, (qi + 1) * sub)
        s = _nt_dot(q_ref[sl, :], k)
        m = jnp.max(s, axis=-1, keepdims=True)
        p = jnp.exp(s - m)
        denom = jnp.sum(p, axis=-1, keepdims=True)
        o_ref[sl, :] = (jnp.dot(p.astype(BF16), v, preferred_element_type=F32) / denom).astype(o_ref.dtype)


def _mixer_mla(h, w_in, q_a_g, kv_a_g, w_uq, w_ukv, q_g, k_g, *, batch, lt, n_lat):
    rows, d = h.shape
    heads = MLA_HEADS
    dk = MLA_NOPE + MLA_ROPE
    hw = 2 * LANES
    n_ctx = lt - n_lat
    n_in = MLA_Q_RANK + MLA_KV_RANK + MLA_ROPE
    n_in_pad = MLA_Q_RANK + MLA_KV_RANK + 2 * LANES
    w_in_p = jnp.pad(w_in, ((0, 0), (0, n_in_pad - n_in))).astype(BF16)
    w_uq_p = jnp.pad(w_uq.reshape(MLA_Q_RANK, heads, dk), ((0, 0), (0, 0), (0, hw - dk))).reshape(
        MLA_Q_RANK, heads * hw).astype(BF16)
    w_ukv_b = w_ukv.astype(BF16)
    qg_p = jnp.pad(q_g.astype(F32), (0, hw - dk)).reshape(1, hw)
    kg_p = jnp.pad(k_g.astype(F32), (0, hw - dk)).reshape(1, hw)
    tm = _pick(lt, (768, 512, 384, 256, 128))
    p = _matmul(h, w_in_p, batch=batch, col_blocks=[0], n_out=n_in_pad, tm=tm, tn=_pick(n_in_pad, (256, 128)),
                name="mla_in")
    tp = _pick(rows, (512, 256, 128))
    cq, ckv, kr = pl.pallas_call(
        _mla_norm_body, grid=(rows // tp,),
        in_specs=[pl.BlockSpec((tp, n_in_pad), lambda i: (i, 0)),
                  pl.BlockSpec((1, MLA_Q_RANK), lambda i: (0, 0)),
                  pl.BlockSpec((1, MLA_KV_RANK), lambda i: (0, 0))],
        out_specs=[pl.BlockSpec((tp, MLA_Q_RANK), lambda i: (i, 0)),
                   pl.BlockSpec((tp, MLA_KV_RANK), lambda i: (i, 0)),
                   pl.BlockSpec((tp, LANES), lambda i: (i, 0))],
        out_shape=[jax.ShapeDtypeStruct((rows, MLA_Q_RANK), BF16),
                   jax.ShapeDtypeStruct((rows, MLA_KV_RANK), BF16),
                   jax.ShapeDtypeStruct((rows, LANES), F32)],
        compiler_params=_cparams(("parallel",)), name="mla_norm")(
            p, q_a_g.reshape(1, MLA_Q_RANK), kv_a_g.reshape(1, MLA_KV_RANK))
    tn = _pick(heads * hw, (1024, 512, 256))
    q_full = _matmul(cq, w_uq_p, batch=batch, col_blocks=[0], n_out=heads * hw, tm=tm, tn=tn, name="mla_uq")
    kv = _matmul(ckv, w_ukv_b, batch=batch, col_blocks=[0], n_out=heads * hw, tm=tm, tn=tn, name="mla_ukv")
    cos, sin = _rope_tables(n_lat, n_ctx, MLA_ROPE, LANES)
    tr = _pick(math.gcd(lt, n_lat), (256, 128))
    trb = lt // tr
    hb = _pick(heads, (8, 4, 2, 1))
    q_cat, k_cat, v = pl.pallas_call(
        functools.partial(_mla_prep_body, heads=hb), grid=(rows // tr, heads // hb),
        in_specs=[pl.BlockSpec((tr, hb * hw), lambda i, j: (i, j)),
                  pl.BlockSpec((tr, hb * hw), lambda i, j: (i, j)),
                  pl.BlockSpec((tr, LANES), lambda i, j: (i, 0)),
                  pl.BlockSpec((tr, LANES), lambda i, j: (i % trb, 0)),
                  pl.BlockSpec((tr, LANES), lambda i, j: (i % trb, 0)),
                  pl.BlockSpec((1, hw), lambda i, j: (0, 0)),
                  pl.BlockSpec((1, hw), lambda i, j: (0, 0))],
        out_specs=[pl.BlockSpec((tr, hb * hw), lambda i, j: (i, j)),
                   pl.BlockSpec((tr, hb * hw), lambda i, j: (i, j)),
                   pl.BlockSpec((tr, hb * LANES), lambda i, j: (i, j))],
        out_shape=[jax.ShapeDtypeStruct((rows, heads * hw), BF16),
                   jax.ShapeDtypeStruct((rows, heads * hw), BF16),
                   jax.ShapeDtypeStruct((rows, heads * LANES), BF16)],
        compiler_params=_cparams(("parallel", "parallel")), name="mla_prep")(q_full, kv, kr, cos, sin, qg_p, kg_p)
    tq = _pick(n_lat, (1024, 512, 256, 128))
    sub = min(tq, 256)
    q3 = q_cat.reshape(batch, lt, heads * hw)
    k3 = k_cat.reshape(batch, lt, heads * hw)
    v3 = v.reshape(batch, lt, heads * LANES)
    o = pl.pallas_call(
        functools.partial(_mla_attn_body, sub=sub), grid=(batch, heads, n_lat // tq),
        in_specs=[pl.BlockSpec((None, tq, hw), lambda b, hh, i: (b, i, hh)),
                  pl.BlockSpec((None, lt, hw), lambda b, hh, i: (b, 0, hh)),
                  pl.BlockSpec((None, lt, LANES), lambda b, hh, i: (b, 0, hh))],
        out_specs=pl.BlockSpec((None, tq, LANES), lambda b, hh, i: (b, i, hh)),
        out_shape=jax.ShapeDtypeStruct((batch, n_lat, heads * LANES), BF16),
        compiler_params=_cparams(("parallel", "parallel", "arbitrary")), name="mla_attn")(q3, k3, v3)
    return o.reshape(batch * n_lat, heads * LANES)


def kernel(x, c, ctx, c_ctx, mod_down, mod_up, mod_b, norm1_g, norm2_g, swa_w_in, swa_q_g, swa_k_g, swa_sinks, swa_w_out, ssd_w_in, ssd_conv_w, ssd_conv_b, ssd_dt_bias, ssd_a_log, ssd_d, ssd_norm_g, ssd_w_out, s5_w_in, s5_lam_re, s5_lam_im, s5_log_dt, s5_b_re, s5_b_im, s5_c_re, s5_c_im, s5_d, s5_w_glu, mla_w_in, mla_q_a_g, mla_kv_a_g, mla_w_uq, mla_w_ukv, mla_q_g, mla_k_g, mla_w_out, moe_w_group, moe_b_group, moe_w_expert, moe_b_expert, moe_w1, moe_w3, moe_w2):
    batch, n_lat, d = x.shape
    n_ctx = ctx.shape[1]
    lt = n_lat + n_ctx
    rows = batch * lt
    depth = mod_down.shape[0]
    dims = dict(batch=batch, lt=lt, n_lat=n_lat)
    tm = _pick(lt, (768, 512, 384, 256, 128))
    tn = _pick(d, (512, 256, 128))

    xs = jnp.concatenate([x, ctx], axis=1).reshape(rows, d)

    pad_rows = -(batch + 1) % SUBLANES
    cvecs = jnp.concatenate([c, c_ctx[None], jnp.zeros((pad_rows, d), F32)], axis=0)
    mod_all = _adaln(cvecs, mod_down, mod_up, mod_b).reshape(depth, batch + 1 + pad_rows, 6, d)

    n_moe_logits = MOE_GROUPS + moe_w_expert.shape[-1]
    router_w = jnp.pad(jnp.concatenate([moe_w_group, moe_w_expert], -1), ((0, 0), (0, 0), (0, LANES - n_moe_logits)))
    router_b = jnp.pad(jnp.concatenate([moe_b_group, moe_b_expert], -1), ((0, 0), (0, LANES - n_moe_logits)))

    def out_proj(o, w, res, mods4, name, last, epilogue="residual", cols=(0,)):
        tk = _pick(o.shape[1], (8192, 4096, 2048, 1024, 512, 256, 128))
        use_rows = n_lat if last else lt
        tm_o = _pick(n_lat, (1024, 512, 256, 128)) if last else tm
        return _matmul(o, w.astype(BF16), batch=batch, col_blocks=list(cols), n_out=d, tm=tm_o, tn=tn, tk=tk,
                       epilogue=epilogue, res=res, mods=mods4, gate_idx=2, use_rows=use_rows, n_lat=n_lat, name=name)

    for i in range(depth):
        kind, slot = i % 4, i // 4
        last = i == depth - 1
        ml = mod_all[i, :batch]
        mc = jnp.broadcast_to(mod_all[i, batch][None], (batch, 6, d))
        mods4 = jnp.stack([ml, mc], axis=1)
        mods2 = mods4.reshape(batch * 2, 6, d)
        h = _modnorm(xs, norm1_g[i], mods2, shift_idx=0, scale_idx=1, lt=lt, n_lat=n_lat, out_dtype=BF16)
        if kind == 0:
            o = _mixer_swa(h, swa_w_in[slot].astype(BF16), swa_q_g[slot], swa_k_g[slot], swa_sinks[slot], **dims)
            xs = out_proj(o, swa_w_out[slot], xs, mods4, "swa_out", last)
        elif kind == 1:
            o = _mixer_ssd(h, ssd_w_in[slot].astype(BF16), ssd_conv_w[slot], ssd_conv_b[slot], ssd_dt_bias[slot],
                           ssd_a_log[slot], ssd_d[slot], ssd_norm_g[slot], **dims)
            xs = out_proj(o, ssd_w_out[slot], xs, mods4, "ssd_out", last)
        elif kind == 2:
            o = _mixer_s5(h, s5_w_in[slot].astype(BF16), s5_lam_re[slot], s5_lam_im[slot], s5_log_dt[slot],
                          s5_b_re[slot], s5_b_im[slot], s5_c_re[slot], s5_c_im[slot], s5_d[slot], **dims)
            xs = out_proj(o, s5_w_glu[slot], xs, mods4, "s5_glu", last, epilogue="glu_residual", cols=(0, d // tn))
        else:
            if not last:
                raise NotImplementedError("the MLA mixer is implemented for the last layer only (latent queries)")
            o = _mixer_mla(h, mla_w_in[slot], mla_q_a_g[slot], mla_kv_a_g[slot], mla_w_uq[slot], mla_w_ukv[slot],
                           mla_q_g[slot], mla_k_g[slot], **dims)
            xs = out_proj(o, mla_w_out[slot], xs, mods4, "mla_out", last)
        xs = _moe_layer(xs, mods2, norm2_g[i], router_w[i], router_b[i].reshape(1, LANES), moe_w1, moe_w3, moe_w2, i,
                        lt=n_lat if last else lt, n_lat=n_lat)
    return xs.reshape(batch, n_lat, d)
```

```python
import functools
import math

import jax
import jax.numpy as jnp
from jax import lax
from jax.experimental import pallas as pl
from jax.experimental.pallas import tpu as pltpu

F32 = jnp.float32
BF16 = jnp.bfloat16
HIGHEST = lax.Precision.HIGHEST

GRID_W = 64
ROPE_BASE = 10000.0
NORM_EPS = 1e-6

SWA_HEADS = 32
SWA_KV_HEADS = 8
SWA_HEAD_DIM = 128
SWA_WINDOW = 128
SWA_BLOCK = 128

SSD_HEAD_DIM = 64
SSD_GROUPS = 8
SSD_STATE = 128
SSD_CHUNK = 128

S5_GROUP = 16
S5_STATE = 64
S5_CHUNK = 256
S5_SLAB_GROUPS = 16

MLA_HEADS = 32
MLA_Q_RANK = 1024
MLA_KV_RANK = 512
MLA_NOPE = 128
MLA_ROPE = 64
MLA_V = 128

MOE_GROUPS = 4
MOE_PER_GROUP = 8
MOE_TOPK = 2
MOE_BLOCK = 256

LANES = 128
SUBLANES = 8
VMEM_LIMIT = 56 * 1024 * 1024


def _cparams(sem, vmem=VMEM_LIMIT):
    return pltpu.CompilerParams(dimension_semantics=sem, vmem_limit_bytes=vmem)


def _pick(n, cands):
    for c in cands:
        if n % c == 0:
            return c
    raise ValueError(f"no tile in {cands} divides {n}")


def _sigmoid(x):
    return 1.0 / (1.0 + jnp.exp(-x))


def _silu(x):
    return x * _sigmoid(x)


def _softplus(x):
    return jnp.maximum(x, 0.0) + jnp.log1p(jnp.exp(-jnp.abs(x)))


def _pack_bf16_pairs(t):
    n = t.shape[1] // 2

    def rne_bits(v):
        u = lax.bitcast_convert_type(v, jnp.uint32)
        return u + jnp.uint32(0x7FFF) + ((u >> 16) & jnp.uint32(1))

    return (rne_bits(t[:, :n]) >> 16) | (rne_bits(t[:, n:]) & jnp.uint32(0xFFFF0000))


def _unpack_bf16_pairs(u):
    lo = lax.bitcast_convert_type(u << 16, F32)
    hi = lax.bitcast_convert_type(u & jnp.uint32(0xFFFF0000), F32)
    return lo, hi


def _nt_dot(a, b):
    return lax.dot_general(a, b, (((1,), (1,)), ((), ())), preferred_element_type=F32)


def _mm_body(*refs, n_w, nk, epilogue, tm, tiles_per_batch, n_lat, gate_idx):
    a_ref = refs[0]
    w_refs = refs[1:1 + n_w]
    pos = 1 + n_w
    res_ref = mod_ref = None
    if epilogue in ("residual", "glu_residual"):
        res_ref, mod_ref = refs[pos], refs[pos + 1]
        pos += 2
    o_ref = refs[pos]
    acc_refs = refs[pos + 1:]

    def finish(vals):
        val = vals[0] * _sigmoid(vals[1]) if epilogue in ("glu", "glu_residual") else vals[0]
        if res_ref is not None:
            row = (pl.program_id(0) % tiles_per_batch) * tm + lax.broadcasted_iota(jnp.int32, (tm, 1), 0)
            gate = jnp.where(row >= n_lat, mod_ref[1, pl.ds(gate_idx, 1), :], mod_ref[0, pl.ds(gate_idx, 1), :])
            val = res_ref[...] + gate * val
        o_ref[...] = val.astype(o_ref.dtype)

    a = a_ref[...]
    if nk == 1:
        finish([jnp.dot(a, w[...], preferred_element_type=F32) for w in w_refs])
        return
    k = pl.program_id(2)

    @pl.when(k == 0)
    def _():
        for acc in acc_refs:
            acc[...] = jnp.zeros_like(acc)

    for w, acc in zip(w_refs, acc_refs):
        acc[...] += jnp.dot(a, w[...], preferred_element_type=F32)

    @pl.when(k == nk - 1)
    def _():
        finish([acc[...] for acc in acc_refs])


def _matmul(a, w, *, batch, col_blocks, n_out, tm, tn, tk=None, out_dtype=F32, epilogue="store", res=None, mods=None,
            gate_idx=0, use_rows=None, n_lat=None, out_shape=None, out_map=None, name="matmul"):
    k_dim = a.shape[1]
    a3 = a.reshape(batch, a.shape[0] // batch, k_dim)
    use_rows = use_rows or a3.shape[1]
    tk = tk or k_dim
    nk = k_dim // tk
    n_w = len(col_blocks)
    tu = use_rows // tm
    grid = (batch * tu, n_out // tn, nk)
    in_specs = [pl.BlockSpec((None, tm, tk), lambda i, j, k: (i // tu, i % tu, k))]
    args = [a3]
    for off in col_blocks:
        in_specs.append(pl.BlockSpec((tk, tn), lambda i, j, k, off=off: (k, j + off)))
        args.append(w)
    if res is not None:
        in_specs.append(pl.BlockSpec((None, tm, tn), lambda i, j, k: (i // tu, i % tu, j)))
        in_specs.append(pl.BlockSpec((None, 2, 6, tn), lambda i, j, k: (i // tu, 0, 0, j)))
        args += [res.reshape(batch, res.shape[0] // batch, res.shape[1]), mods]
    out_shape = out_shape or (batch * use_rows, n_out)
    out_map = out_map or (lambda i, j, k: (i, j))
    scratch = [pltpu.VMEM((tm, tn), F32) for _ in range(n_w)] if nk > 1 else []
    body = functools.partial(_mm_body, n_w=n_w, nk=nk, epilogue=epilogue, tm=tm, tiles_per_batch=tu,
                             n_lat=n_lat, gate_idx=gate_idx)
    return pl.pallas_call(
        body, grid=grid, in_specs=in_specs, out_specs=pl.BlockSpec((tm, tn), out_map),
        out_shape=jax.ShapeDtypeStruct(out_shape, out_dtype), scratch_shapes=scratch,
        compiler_params=_cparams(("parallel", "parallel", "arbitrary")), name=name)(*args)


def _adaln_body(c_ref, wd_ref, wu_ref, b_ref, o_ref, t_ref):
    @pl.when(pl.program_id(1) == 0)
    def _():
        cv = c_ref[...]
        t_ref[...] = jnp.dot(_silu(cv), wd_ref[...], precision=HIGHEST, preferred_element_type=F32)

    o_ref[...] = jnp.dot(t_ref[...], wu_ref[...], precision=HIGHEST, preferred_element_type=F32) + b_ref[...]


def _adaln(cvecs, mod_down, mod_up, mod_b):
    depth, d, rank = mod_down.shape
    n6 = mod_up.shape[-1]
    rows = cvecs.shape[0]
    tn = _pick(n6, (2048, 1024, 512, 256, 128))
    return pl.pallas_call(
        _adaln_body, grid=(depth, n6 // tn),
        in_specs=[pl.BlockSpec((rows, d), lambda l, j: (0, 0)),
                  pl.BlockSpec((None, d, rank), lambda l, j: (l, 0, 0)),
                  pl.BlockSpec((None, rank, tn), lambda l, j: (l, 0, j)),
                  pl.BlockSpec((None, 1, tn), lambda l, j: (l, 0, j))],
        out_specs=pl.BlockSpec((None, rows, tn), lambda l, j: (l, 0, j)),
        out_shape=jax.ShapeDtypeStruct((depth, rows, n6), F32),
        scratch_shapes=[pltpu.VMEM((rows, rank), F32)],
        compiler_params=_cparams(("parallel", "arbitrary")), name="adaln")(
            cvecs, mod_down, mod_up, mod_b.reshape(depth, 1, n6))


def _modnorm_body(*refs, shift_idx, scale_idx, router):
    if router:
        x_ref, g_ref, mod_ref, wh_ref, wl_ref, br_ref, h_ref, ids_ref, gates_ref, hb_ref = refs
    else:
        x_ref, g_ref, mod_ref, h_ref = refs
    tm, d = x_ref.shape
    half = d // 2
    rb = 16
    cw = min(4 * LANES, half)

    def row_block(r, carry):
        rows = pl.ds(pl.multiple_of(r * rb, rb), rb)
        ss = jnp.zeros((rb, 1), F32)
        for c in range(d // cw):
            xc = x_ref[rows, c * cw:(c + 1) * cw]
            ss = ss + jnp.sum(xc * xc, axis=-1, keepdims=True)
        rinv = lax.rsqrt(ss / d + NORM_EPS)

        def h_chunk(c0):
            cols = slice(c0, c0 + cw)
            gs = g_ref[:, cols] * (1.0 + mod_ref[pl.ds(scale_idx, 1), cols])
            return x_ref[rows, cols] * rinv * gs + mod_ref[pl.ds(shift_idx, 1), cols]

        for c in range(half // cw):
            pieces = []
            for c0 in (c * cw, half + c * cw):
                h = h_chunk(c0)
                if not router:
                    h_ref[rows, c0:c0 + cw] = h.astype(h_ref.dtype)
                    continue
                h_b = h.astype(BF16)
                hb_ref[rows, c0:c0 + cw] = h_b
                pieces.append(lax.bitcast_convert_type(h_b.astype(F32), jnp.uint32))
            if router:
                h_ref[rows, c * cw:(c + 1) * cw] = (pieces[0] >> 16) | pieces[1]
        return carry

    lax.fori_loop(0, tm // rb, row_block, 0, unroll=4)
    if not router:
        return
    h_b = hb_ref[...]
    logits = (jnp.dot(h_b, wh_ref[...], preferred_element_type=F32)
              + jnp.dot(h_b, wl_ref[...], preferred_element_type=F32)) + br_ref[...]
    lane = lax.broadcasted_iota(jnp.int32, logits.shape, 1)
    lane_f = lane.astype(F32)
    neg = -jnp.inf

    def first_lane(hit):
        return jnp.min(jnp.where(hit, lane_f, float(LANES)), axis=-1, keepdims=True).astype(jnp.int32)

    is_grp = lane < MOE_GROUPS
    lg = jnp.where(is_grp, logits, neg)
    mg = jnp.max(lg, axis=-1, keepdims=True)
    grp = first_lane(lg == mg)
    p_grp = 1.0 / jnp.sum(jnp.where(is_grp, jnp.exp(lg - mg), 0.0), axis=-1, keepdims=True)
    lo = MOE_GROUPS + grp * MOE_PER_GROUP
    le = jnp.where((lane >= lo) & (lane < lo + MOE_PER_GROUP), logits, neg)
    m1 = jnp.max(le, axis=-1, keepdims=True)
    i1 = first_lane(le == m1)
    le2 = jnp.where(lane == i1, neg, le)
    m2 = jnp.max(le2, axis=-1, keepdims=True)
    i2 = first_lane(le2 == m2)
    e2 = jnp.exp(m2 - m1)
    g1 = p_grp / (1.0 + e2)
    g2 = p_grp * e2 / (1.0 + e2)
    ids_ref[...] = jnp.where(lane == 0, i1 - MOE_GROUPS, jnp.where(lane == 1, i2 - MOE_GROUPS, 0))
    gates_ref[...] = jnp.where(lane == 0, g1, jnp.where(lane == 1, g2, 0.0))


def _modnorm(x, g, mods2, *, shift_idx, scale_idx, lt, n_lat, out_dtype, router_w=None, router_b=None):
    rows, d = x.shape
    tm = _pick(math.gcd(lt, n_lat), (256, 128))
    tpb = lt // tm
    lat_tiles = n_lat // tm
    router = router_w is not None
    in_specs = [pl.BlockSpec((tm, d), lambda i: (i, 0)),
                pl.BlockSpec((1, d), lambda i: (0, 0)),
                pl.BlockSpec((None, 6, d), lambda i: ((i // tpb) * 2 + ((i % tpb) >= lat_tiles).astype(jnp.int32), 0, 0))]
    args = [x, g.reshape(1, d), mods2]
    out_specs = [pl.BlockSpec((tm, d), lambda i: (i, 0))]
    out_shape = [jax.ShapeDtypeStruct((rows, d), out_dtype)]
    if router:
        out_specs = [pl.BlockSpec((tm, d // 2), lambda i: (i, 0))]
        out_shape = [jax.ShapeDtypeStruct((rows, d // 2), jnp.uint32)]
        w_spec = pl.BlockSpec((d, LANES), lambda i: (0, 0))
        in_specs += [w_spec, w_spec, pl.BlockSpec((1, LANES), lambda i: (0, 0))]
        w_hi = router_w.astype(BF16)
        args += [w_hi, (router_w - w_hi.astype(F32)).astype(BF16), router_b]
        out_specs += [pl.BlockSpec((tm, LANES), lambda i: (i, 0))] * 2
        out_shape += [jax.ShapeDtypeStruct((rows, LANES), jnp.int32), jax.ShapeDtypeStruct((rows, LANES), F32)]
    scratch = [pltpu.VMEM((tm, d), BF16)] if router else []
    body = functools.partial(_modnorm_body, shift_idx=shift_idx, scale_idx=scale_idx, router=router)
    out = pl.pallas_call(body, grid=(rows // tm,), in_specs=in_specs, out_specs=out_specs, out_shape=out_shape,
                         scratch_shapes=scratch,
                         compiler_params=_cparams(("parallel",)), name="modnorm_router" if router else "modnorm")(*args)
    return out if router else out[0]


def _row_gather_start(idx_ref, base, n, src_hbm, dst, sem, dst_base=0, idx_stride=1):
    group = 8

    def body(g, c):
        for j in range(group):
            r = g * group + j
            row = idx_ref[base + r * idx_stride]
            pltpu.make_async_copy(src_hbm.at[pl.ds(row, 1)], dst.at[pl.ds(dst_base + r, 1)], sem).start(priority=j % 2)
        return c

    lax.fori_loop(0, n // group, body, 0)


def _row_gather_wait(n, src_hbm, dst, sem):
    def body(r, c):
        pltpu.make_async_copy(src_hbm.at[pl.ds(0, 1)], dst.at[pl.ds(r, 1)], sem).wait()
        return c

    lax.fori_loop(0, n, body, 0, unroll=8)


def _moe_ffn_body(src_ref, blk_e_ref, n_used_ref, h_hbm, w1_ref, w3_ref, w2_ref, o_ref,
                  xbuf, xb, w1b, w3b, w2b, sems, *, blk):
    i = pl.program_id(0)
    slot = i % 2
    n_used = n_used_ref[0]
    half = xb.shape[1] // 2

    @pl.when(i == 0)
    def _():
        _row_gather_start(src_ref, 0, blk, h_hbm, xbuf.at[0], sems.at[0])

    @pl.when(i + 1 < n_used)
    def _():
        _row_gather_start(src_ref, (i + 1) * blk, blk, h_hbm, xbuf.at[1 - slot], sems.at[1 - slot])

    @pl.when(i < n_used)
    def _():
        @pl.when(jnp.logical_or(i == 0, blk_e_ref[i] != blk_e_ref[jnp.maximum(i - 1, 0)]))
        def _():
            w1b[...] = w1_ref[...].astype(BF16)
            w3b[...] = w3_ref[...].astype(BF16)
            w2b[...] = w2_ref[...].astype(BF16)

        _row_gather_wait(blk, h_hbm, xbuf.at[slot], sems.at[slot])
        lo, hi = _unpack_bf16_pairs(xbuf[slot])
        xb[:, :half] = lo.astype(BF16)
        xb[:, half:] = hi.astype(BF16)
        x = xb[...]
        a1 = jnp.dot(x, w1b[...], preferred_element_type=F32)
        a3 = jnp.dot(x, w3b[...], preferred_element_type=F32)
        mid = (_silu(a1) * a3).astype(BF16)
        o_ref[...] = _pack_bf16_pairs(jnp.dot(mid, w2b[...], preferred_element_type=F32))

    @pl.when(i >= n_used)
    def _():
        o_ref[...] = jnp.zeros_like(o_ref)


def _moe_ffn(hpk, src, blk_e, n_used, w1, w3, w2, layer, *, blk):
    n_rows = src.shape[0]
    n_blk = n_rows // blk
    half = hpk.shape[1]
    d = 2 * half
    hid = w1.shape[-1]
    grid_spec = pltpu.PrefetchScalarGridSpec(
        num_scalar_prefetch=3, grid=(n_blk,),
        in_specs=[pl.BlockSpec(memory_space=pl.ANY),
                  pl.BlockSpec((None, None, d, hid), lambda i, s, e, n: (layer, e[i], 0, 0)),
                  pl.BlockSpec((None, None, d, hid), lambda i, s, e, n: (layer, e[i], 0, 0)),
                  pl.BlockSpec((None, None, hid, d), lambda i, s, e, n: (layer, e[i], 0, 0))],
        out_specs=pl.BlockSpec((blk, half), lambda i, s, e, n: (i, 0)),
        scratch_shapes=[pltpu.VMEM((2, blk, half), jnp.uint32), pltpu.VMEM((blk, d), BF16),
                        pltpu.VMEM((d, hid), BF16), pltpu.VMEM((d, hid), BF16), pltpu.VMEM((hid, d), BF16),
                        pltpu.SemaphoreType.DMA((2,))])
    return pl.pallas_call(
        functools.partial(_moe_ffn_body, blk=blk), grid_spec=grid_spec,
        out_shape=jax.ShapeDtypeStruct((n_rows, half), jnp.uint32),
        compiler_params=_cparams(("arbitrary",)), name="moe_ffn")(src, blk_e, n_used, hpk, w1, w3, w2)


def _moe_combine_body(pos_ref, x_ref, gates_ref, mod_ref, yb_hbm, o_ref, ybuf, sems, *, tm, n_tiles):
    i = pl.program_id(0)
    slot = i % 2
    half = x_ref.shape[1] // 2

    def start(tile, s):
        for kk in range(MOE_TOPK):
            _row_gather_start(pos_ref, tile * tm * MOE_TOPK + kk, tm, yb_hbm, ybuf.at[s], sems.at[s],
                              dst_base=kk * tm, idx_stride=MOE_TOPK)

    @pl.when(i == 0)
    def _():
        start(0, 0)

    @pl.when(i + 1 < n_tiles)
    def _():
        start(i + 1, 1 - slot)

    _row_gather_wait(MOE_TOPK * tm, yb_hbm, ybuf.at[slot], sems.at[slot])
    ys = ybuf.at[slot]
    rb, cb = 32, min(4 * LANES, half)
    for r in range(tm // rb):
        rows = pl.ds(r * rb, rb)
        g0 = jnp.broadcast_to(gates_ref[rows, 0:1], (rb, cb))
        g1 = jnp.broadcast_to(gates_ref[rows, 1:2], (rb, cb))
        for c in range(half // cb):
            lo0, hi0 = _unpack_bf16_pairs(ys[rows, c * cb:(c + 1) * cb])
            lo1, hi1 = _unpack_bf16_pairs(ys[pl.ds(tm + r * rb, rb), c * cb:(c + 1) * cb])
            for off, y in ((c * cb, g0 * lo0 + g1 * lo1), (half + c * cb, g0 * hi0 + g1 * hi1)):
                o_ref[rows, off:off + cb] = x_ref[rows, off:off + cb] + mod_ref[pl.ds(5, 1), off:off + cb] * y


def _moe_combine(x, gates, mods2, yb, pos, *, lt, n_lat):
    rows, d = x.shape
    tm = _pick(math.gcd(lt, n_lat), (256, 128))
    tpb = lt // tm
    lat_tiles = n_lat // tm
    n_tiles = rows // tm
    grid_spec = pltpu.PrefetchScalarGridSpec(
        num_scalar_prefetch=1, grid=(n_tiles,),
        in_specs=[pl.BlockSpec((tm, d), lambda i, p: (i, 0)),
                  pl.BlockSpec((tm, LANES), lambda i, p: (i, 0)),
                  pl.BlockSpec((None, 6, d),
                               lambda i, p: ((i // tpb) * 2 + ((i % tpb) >= lat_tiles).astype(jnp.int32), 0, 0)),
                  pl.BlockSpec(memory_space=pl.ANY)],
        out_specs=pl.BlockSpec((tm, d), lambda i, p: (i, 0)),
        scratch_shapes=[pltpu.VMEM((2, MOE_TOPK * tm, d // 2), jnp.uint32), pltpu.SemaphoreType.DMA((2,))])
    return pl.pallas_call(
        functools.partial(_moe_combine_body, tm=tm, n_tiles=n_tiles), grid_spec=grid_spec,
        out_shape=jax.ShapeDtypeStruct((rows, d), F32),
        compiler_params=_cparams(("arbitrary",)), name="moe_combine")(pos, x, gates, mods2, yb)


def _moe_layer(x, mods2, norm_g, router_w, router_b, w1, w3, w2, layer, *, lt, n_lat):
    rows, d = x.shape
    n_exp = w1.shape[1]
    hpk, ids, gates = _modnorm(x, norm_g, mods2, shift_idx=3, scale_idx=4, lt=lt, n_lat=n_lat, out_dtype=F32,
                               router_w=router_w, router_b=router_b)
    blk = MOE_BLOCK
    n_asg = rows * MOE_TOPK
    n_blk = n_asg // blk + n_exp
    flat = ids[:, :MOE_TOPK].reshape(-1)
    order = jnp.argsort(flat).astype(jnp.int32)
    sorted_e = flat[order]
    counts = jnp.sum((flat[:, None] == jnp.arange(n_exp, dtype=jnp.int32)[None, :]).astype(jnp.int32), axis=0)
    padded = (counts + blk - 1) // blk * blk
    pad_end = jnp.cumsum(padded)
    pad_start = pad_end - padded
    start = jnp.cumsum(counts) - counts
    dest = (pad_start[sorted_e] + jnp.arange(n_asg, dtype=jnp.int32) - start[sorted_e]).astype(jnp.int32)
    blk_first = jnp.arange(n_blk, dtype=jnp.int32) * blk
    blk_e = jnp.minimum(jnp.sum((pad_end[None, :] <= blk_first[:, None]).astype(jnp.int32), axis=1), n_exp - 1)
    n_used = (pad_end[-1:] // blk).astype(jnp.int32)
    within = jnp.arange(n_blk * blk, dtype=jnp.int32) - jnp.repeat(pad_start[blk_e], blk)
    valid = within < jnp.repeat(counts[blk_e], blk)
    sorted_idx = jnp.clip(jnp.repeat(start[blk_e], blk) + within, 0, n_asg - 1)
    src = jnp.where(valid, order[sorted_idx] // MOE_TOPK, 0).astype(jnp.int32)
    pos = dest[jnp.argsort(order)]
    yb = _moe_ffn(hpk, src, blk_e.astype(jnp.int32), n_used, w1, w3, w2, layer, blk=blk)
    return _moe_combine(x, gates, mods2, yb, pos.astype(jnp.int32), lt=lt, n_lat=n_lat)


def _rope_tables(n_lat, n_ctx, rot_dim, width):
    rows = n_lat // GRID_W
    row = jnp.repeat(jnp.arange(rows, dtype=F32), GRID_W)
    col = jnp.tile(jnp.arange(GRID_W, dtype=F32), rows)
    axis_dim = rot_dim // 2
    inv_freq = ROPE_BASE ** (-jnp.arange(0, axis_dim, 2, dtype=F32) / axis_dim)
    a0 = row[:, None] * inv_freq
    a1 = col[:, None] * inv_freq
    cos = jnp.concatenate([jnp.cos(a0), jnp.cos(a0), jnp.cos(a1), jnp.cos(a1)], -1)
    sin = jnp.concatenate([-jnp.sin(a0), jnp.sin(a0), -jnp.sin(a1), jnp.sin(a1)], -1)
    cos = jnp.pad(cos, ((0, n_ctx), (0, width - rot_dim)), constant_values=1.0)
    sin = jnp.pad(sin, ((0, n_ctx), (0, width - rot_dim)))
    return cos, sin


def _split2(t):
    hi = t.astype(BF16)
    return hi, (t - hi.astype(F32)).astype(BF16)


def _rowsum_lanes(t):
    ones = jnp.ones((LANES, LANES), BF16)
    hi, lo = _split2(t)
    return jnp.dot(hi, ones, preferred_element_type=F32) + jnp.dot(lo, ones, preferred_element_type=F32)


def _rope_perm(half):
    src = lax.broadcasted_iota(jnp.int32, (LANES, LANES), 0)
    dst = lax.broadcasted_iota(jnp.int32, (LANES, LANES), 1)
    partner = jnp.where((dst & (2 * half - 1)) < half, dst + half, dst - half)
    return (src == partner).astype(BF16)


def _rope(x, cos, sin, perm):
    hi, lo = _split2(x)
    partner = jnp.dot(hi, perm, preferred_element_type=F32) + jnp.dot(lo, perm, preferred_element_type=F32)
    return x * cos + partner * sin


def _swa_prep_body(p_ref, cos_ref, sin_ref, qg_ref, kg_ref, q_ref, k_ref, v_ref, *, n_q, n_kv):
    hd = SWA_HEAD_DIM
    cos, sin = cos_ref[...], sin_ref[...]
    scale = hd ** -0.5

    perm = _rope_perm(hd // 4)

    def norm_rope(t, g):
        t = t * lax.rsqrt(_rowsum_lanes(t * t) / hd + NORM_EPS) * g
        return _rope(t, cos, sin, perm)

    for h in range(n_q):
        q_ref[:, h * hd:(h + 1) * hd] = (norm_rope(p_ref[:, h * hd:(h + 1) * hd], qg_ref[...]) * scale).astype(BF16)
    for h in range(n_kv):
        c0 = (n_q + h) * hd
        k_ref[:, h * hd:(h + 1) * hd] = norm_rope(p_ref[:, c0:c0 + hd], kg_ref[...]).astype(BF16)
    v_ref[...] = p_ref[:, (n_q + n_kv) * hd:].astype(BF16)


def _swa_attn_body(sink_ref, q_ref, *refs, n_lat, grp, nq):
    nw = nq + 2
    k_refs, v_refs, o_ref = refs[:nw + 1], refs[nw + 1:2 * nw + 2], refs[2 * nw + 2]
    kv = pl.program_id(1)
    first = pl.program_id(2) * nq
    blk = SWA_BLOCK
    hd = SWA_HEAD_DIM
    n_ctx = k_refs[nw].shape[0]
    is_lat = first < n_lat // blk
    qpos = first * blk + lax.broadcasted_iota(jnp.int32, (nq * blk, 1), 0)
    biases = []
    for w in range(nw):
        kpos = (first - 1 + w) * blk + lax.broadcasted_iota(jnp.int32, (1, blk), 1)
        valid = (jnp.abs(kpos - qpos) <= SWA_WINDOW) & (kpos >= 0) & (kpos < n_lat) & is_lat
        biases.append(jnp.where(valid, 0.0, -jnp.inf))
    biases.append(jnp.zeros((nq * blk, n_ctx), F32))
    bias = jnp.concatenate(biases, axis=1)
    k_all = jnp.concatenate([r[...] for r in k_refs], axis=0)
    v_all = jnp.concatenate([r[...] for r in v_refs], axis=0)
    for g in range(grp):
        sink = sink_ref[kv * grp + g]
        s = _nt_dot(q_ref[:, g * hd:(g + 1) * hd], k_all) + bias
        m = jnp.maximum(jnp.max(s, axis=-1, keepdims=True), sink)
        p = jnp.exp(s - m)
        denom = jnp.sum(p, axis=-1, keepdims=True) + jnp.exp(sink - m)
        o = jnp.dot(p.astype(BF16), v_all, preferred_element_type=F32) / denom
        o_ref[:, g * hd:(g + 1) * hd] = o.astype(o_ref.dtype)


def _mixer_swa(h, w_in, q_g, k_g, sinks, *, batch, lt, n_lat):
    rows = h.shape[0]
    hd = SWA_HEAD_DIM
    n_q, n_kv = SWA_HEADS, SWA_KV_HEADS
    grp = n_q // n_kv
    n_ctx = lt - n_lat
    n_cols = (n_q + 2 * n_kv) * hd
    tm = _pick(lt, (768, 512, 384, 256, 128))
    tn = _pick(n_cols, (1024, 512, 256, 128))
    p = _matmul(h, w_in, batch=batch, col_blocks=[0], n_out=n_cols, tm=tm, tn=tn, name="swa_in")
    cos, sin = _rope_tables(n_lat, n_ctx, hd, hd)
    tp = _pick(math.gcd(lt, n_lat), (256, 128))
    tpb = lt // tp
    q, k, v = pl.pallas_call(
        functools.partial(_swa_prep_body, n_q=n_q, n_kv=n_kv), grid=(rows // tp,),
        in_specs=[pl.BlockSpec((tp, n_cols), lambda i: (i, 0)),
                  pl.BlockSpec((tp, hd), lambda i: (i % tpb, 0)),
                  pl.BlockSpec((tp, hd), lambda i: (i % tpb, 0)),
                  pl.BlockSpec((1, hd), lambda i: (0, 0)),
                  pl.BlockSpec((1, hd), lambda i: (0, 0))],
        out_specs=[pl.BlockSpec((tp, n_q * hd), lambda i: (i, 0)),
                   pl.BlockSpec((tp, n_kv * hd), lambda i: (i, 0)),
                   pl.BlockSpec((tp, n_kv * hd), lambda i: (i, 0))],
        out_shape=[jax.ShapeDtypeStruct((rows, n_q * hd), BF16),
                   jax.ShapeDtypeStruct((rows, n_kv * hd), BF16),
                   jax.ShapeDtypeStruct((rows, n_kv * hd), BF16)],
        compiler_params=_cparams(("parallel",)), name="swa_prep")(p, cos, sin, q_g.reshape(1, hd), k_g.reshape(1, hd))

    blk = SWA_BLOCK
    bpb = lt // blk
    lat_blocks = n_lat // blk

    nq = 2
    steps = bpb // nq

    def win(off):
        return lambda b, kvh, j: (b * bpb + jnp.clip(j * nq + off, 0, lat_blocks - 1), kvh)

    ctx_spec = pl.BlockSpec((None, n_ctx, hd), lambda b, kvh, j: (b, n_lat // n_ctx, kvh))
    k3 = k.reshape(batch, lt, n_kv * hd)
    v3 = v.reshape(batch, lt, n_kv * hd)
    kv_spec = [pl.BlockSpec((blk, hd), win(off)) for off in range(-1, nq + 1)]
    q_spec = pl.BlockSpec((nq * blk, grp * hd), lambda b, kvh, j: (b * steps + j, kvh))
    n_win = len(kv_spec)
    return pl.pallas_call(
        functools.partial(_swa_attn_body, n_lat=n_lat, grp=grp, nq=nq), grid=(batch, n_kv, steps),
        in_specs=[pl.BlockSpec(memory_space=pltpu.SMEM), q_spec] + kv_spec + [ctx_spec] + kv_spec + [ctx_spec],
        out_specs=q_spec,
        out_shape=jax.ShapeDtypeStruct((rows, n_q * hd), BF16),
        compiler_params=_cparams(("parallel", "parallel", "arbitrary")), name="swa_attn")(
            sinks.astype(F32), q, *([k] * n_win), k3, *([v] * n_win), v3)


def _conv_silu_body(x_ref, prev_ref, next_ref, w_ref, b_ref, o_ref, *, lat_tiles, n_tiles):
    t = pl.program_id(1)
    x = x_ref[...]
    rows = x.shape[0]
    has_prev = jnp.logical_and(t != 0, t != lat_tiles).astype(F32)
    has_next = jnp.logical_and(t != lat_tiles - 1, t != n_tiles - 1).astype(F32)
    row = lax.broadcasted_iota(jnp.int32, x.shape, 0)
    x_m = jnp.where(row == 0, prev_ref[SUBLANES - 1:SUBLANES, :] * has_prev, pltpu.roll(x, 1, 0))
    x_p = jnp.where(row == rows - 1, next_ref[0:1, :] * has_next, pltpu.roll(x, rows - 1, 0))
    y = w_ref[0:1, :] * x_m + w_ref[1:2, :] * x + w_ref[2:3, :] * x_p + b_ref[...]
    o_ref[...] = _silu(y).astype(o_ref.dtype)


def _conv_silu(xbc, conv_w, conv_b, *, batch, lt, n_lat):
    cols = xbc.shape[-1]
    x3 = xbc.reshape(batch, lt, cols)
    tt = _pick(math.gcd(lt, n_lat), (256, 128))
    tc = _pick(cols, (2048, 1024, 512, 256, 128))
    n_tiles = lt // tt
    sub = tt // SUBLANES
    out = pl.pallas_call(
        functools.partial(_conv_silu_body, lat_tiles=n_lat // tt, n_tiles=n_tiles),
        grid=(batch, n_tiles, cols // tc),
        in_specs=[pl.BlockSpec((None, tt, tc), lambda b, t, j: (b, t, j)),
                  pl.BlockSpec((None, SUBLANES, tc), lambda b, t, j: (b, jnp.maximum(t * sub - 1, 0), j)),
                  pl.BlockSpec((None, SUBLANES, tc), lambda b, t, j: (b, jnp.minimum((t + 1) * sub, lt // SUBLANES - 1), j)),
                  pl.BlockSpec((3, tc), lambda b, t, j: (0, j)),
                  pl.BlockSpec((1, tc), lambda b, t, j: (0, j))],
        out_specs=pl.BlockSpec((None, tt, tc), lambda b, t, j: (b, t, j)),
        out_shape=jax.ShapeDtypeStruct((batch, lt, cols), BF16),
        compiler_params=_cparams(("parallel", "parallel", "parallel")), name="ssd_conv")(
            x3, x3, x3, conv_w, conv_b.reshape(1, cols))
    return out


def _ssd_pass_body(*refs, direction, hg, final):
    if final:
        x_ref, b_ref, c_ref, dtt_ref, biast_ref, alogt_ref, y0_ref, z_ref, ng_ref, o_ref, state_ref, g_ref = refs
    else:
        x_ref, b_ref, c_ref, dtt_ref, biast_ref, alogt_ref, dsk_ref, o_ref, state_ref = refs
    t_len = x_ref.shape[0]
    n_state = b_ref.shape[1]

    @pl.when(pl.program_id(2) == 0)
    def _():
        state_ref[...] = jnp.zeros_like(state_ref)

    bm = b_ref[...]
    cm = c_ref[...]
    dt_t = _softplus(dtt_ref[...] + biast_ref[...])
    v_t = dt_t * (-jnp.exp(alogt_ref[...]))
    row = lax.broadcasted_iota(jnp.int32, (t_len, t_len), 0)
    col = lax.broadcasted_iota(jnp.int32, (t_len, t_len), 1)
    tri = (col <= row) if direction == 0 else (col >= row)
    cum_t = lax.dot_general(v_t, tri.astype(F32), (((1,), (1,)), ((), ())), precision=HIGHEST,
                            preferred_element_type=F32)
    total_t = jnp.sum(v_t, axis=1, keepdims=True)
    w_t = dt_t * jnp.exp(total_t - cum_t)
    cb = _nt_dot(cm, bm)
    b_t = bm.astype(F32).T
    left = lax.broadcasted_iota(jnp.int32, (t_len, LANES), 1) < SSD_HEAD_DIM
    left_n = lax.broadcasted_iota(jnp.int32, (n_state, LANES), 1) < SSD_HEAD_DIM

    def split3(t):
        hi = t.astype(BF16).astype(F32)
        mid = (t - hi).astype(BF16).astype(F32)
        return hi, mid, (t - hi - mid).astype(BF16).astype(F32)

    def spread(width):
        hrow = lax.broadcasted_iota(jnp.int32, (hg, hg * width), 0)
        hcol = lax.broadcasted_iota(jnp.int32, (hg, hg * width), 1)
        return jnp.logical_and(hcol >= hrow * width, hcol < (hrow + 1) * width).astype(F32)

    def tn_dot(a_rows, w_rows):
        pad = LANES - a_rows.shape[0]
        a = jnp.concatenate([a_rows, jnp.zeros((pad, a_rows.shape[1]), F32)], axis=0)
        w = jnp.concatenate([w_rows, jnp.zeros((pad, w_rows.shape[1]), F32)], axis=0)
        a_cols = jnp.concatenate([a[:, j * LANES:(j + 1) * LANES].T for j in range(a.shape[1] // LANES)], axis=0)
        return jnp.dot(a_cols.astype(BF16), w.astype(BF16), preferred_element_type=F32)

    e_t = spread(t_len)
    cum_pieces = split3(cum_t)
    seg_all = tn_dot(
        jnp.concatenate(list(cum_pieces) + [jnp.ones((3 * hg, t_len), F32)], axis=0),
        jnp.concatenate([e_t] * 3 + [-jnp.tile(p, (1, hg)) * e_t for p in cum_pieces], axis=0))
    e_x = spread(SSD_HEAD_DIM)
    cols = jnp.concatenate([jnp.exp(cum_t), jnp.broadcast_to(jnp.exp(total_t), (hg, LANES))], axis=1)
    cols_x = tn_dot(jnp.concatenate(split3(cols), axis=0), jnp.concatenate([e_x] * 3, axis=0))
    exp_cum_x = cols_x[:t_len]
    exp_total_x = cols_x[t_len:t_len + 1]

    for pr in range(hg // 2):
        h0 = 2 * pr
        sl = slice(pr * LANES, (pr + 1) * LANES)
        xb = x_ref[:, sl]
        ys, ss = [], []
        for hh in (h0, h0 + 1):
            dec = jnp.exp(jnp.where(tri, seg_all[:, hh * t_len:(hh + 1) * t_len], -jnp.inf))
            ys.append(jnp.dot((cb * dec * dt_t[hh:hh + 1, :]).astype(BF16), xb, preferred_element_type=F32))
            ss.append(jnp.dot((b_t * w_t[hh:hh + 1, :]).astype(BF16), xb, preferred_element_type=F32))
        st = state_ref[:, sl]
        y = jnp.where(left, ys[0], ys[1]) + jnp.dot(cm, st.astype(BF16), preferred_element_type=F32) * exp_cum_x[:, sl]
        state_ref[:, sl] = st * exp_total_x[:, sl] + jnp.where(left_n, ss[0], ss[1])
        if final:
            z = z_ref[:, sl]
            g_ref[:, sl] = (y0_ref[:, sl] + y) * _silu(z)
        else:
            o_ref[:, sl] = dsk_ref[:, sl] * xb.astype(F32) + y
    if final:
        g = g_ref[...]
        o_ref[...] = (g * lax.rsqrt(jnp.mean(g * g, axis=-1, keepdims=True) + NORM_EPS) * ng_ref[...]).astype(o_ref.dtype)


def _chunk_order(i, direction, lat_chunks, ctx_chunks):
    if direction == 0:
        return jnp.where(i < ctx_chunks, lat_chunks + i, i - ctx_chunks)
    return lat_chunks + ctx_chunks - 1 - i


def _ssd_pass(direction, xbc_s, dt_t, bias_t, alog_t, extra, *, batch, lt, n_lat, d_inner, final):
    groups = SSD_GROUPS
    gw = d_inner // groups
    hg = gw // SSD_HEAD_DIM
    n_state = SSD_STATE
    t_len = SSD_CHUNK
    lat_chunks, ctx_chunks = n_lat // t_len, (lt - n_lat) // t_len
    n_chunks = lat_chunks + ctx_chunks
    xb = d_inner // n_state

    def cidx(i):
        return _chunk_order(i, direction, lat_chunks, ctx_chunks)

    in_specs = [
        pl.BlockSpec((None, t_len, gw), lambda b, g, i: (b, cidx(i), g)),
        pl.BlockSpec((None, t_len, n_state), lambda b, g, i: (b, cidx(i), xb + g)),
        pl.BlockSpec((None, t_len, n_state), lambda b, g, i: (b, cidx(i), xb + groups + g)),
        pl.BlockSpec((None, None, None, hg, t_len), lambda b, g, i: (direction, g, b, 0, cidx(i))),
        pl.BlockSpec((None, None, hg, 1), lambda b, g, i: (direction, g, 0, 0)),
        pl.BlockSpec((None, None, hg, 1), lambda b, g, i: (direction, g, 0, 0)),
    ]
    args = [xbc_s, xbc_s, xbc_s, dt_t, bias_t, alog_t]
    row_spec = pl.BlockSpec((None, t_len, gw), lambda b, g, i: (b, cidx(i), g))
    vec_spec = pl.BlockSpec((1, gw), lambda b, g, i: (0, g))
    scratch = [pltpu.VMEM((n_state, gw), F32)]
    if final:
        y0, z, norm_g = extra
        in_specs += [row_spec, row_spec, vec_spec]
        args += [y0, z, norm_g]
        out_dtype = BF16
        scratch.append(pltpu.VMEM((t_len, gw), F32))
    else:
        (dsk,) = extra
        in_specs += [vec_spec]
        args += [dsk]
        out_dtype = F32
    return pl.pallas_call(
        functools.partial(_ssd_pass_body, direction=direction, hg=hg, final=final),
        grid=(batch, groups, n_chunks), in_specs=in_specs, out_specs=row_spec,
        out_shape=jax.ShapeDtypeStruct((batch, lt, d_inner), out_dtype), scratch_shapes=scratch,
        compiler_params=_cparams(("parallel", "parallel", "arbitrary")), name=f"ssd_pass{direction}")(*args)


def _mixer_ssd(h, w_in, conv_w, conv_b, dt_bias, a_log, d_skip, norm_g, *, batch, lt, n_lat):
    rows = h.shape[0]
    groups = SSD_GROUPS
    n_heads = dt_bias.shape[-1]
    d_inner = n_heads * SSD_HEAD_DIM
    hg = n_heads // groups
    gn = groups * SSD_STATE
    tm = _pick(lt, (768, 512, 384, 256, 128))
    tn = _pick(math.gcd(d_inner, 2 * gn), (1024, 512, 256, 128))
    z = _matmul(h, w_in, batch=batch, col_blocks=[0], n_out=d_inner, tm=tm, tn=tn, name="ssd_in_z")
    xbc = _matmul(h, w_in, batch=batch, col_blocks=[d_inner // tn], n_out=d_inner + 2 * gn, tm=tm, tn=tn,
                  name="ssd_in_xbc")
    tdt = _pick(2 * n_heads, (256, 128, 64, 32))
    dt = _matmul(h, w_in, batch=batch, col_blocks=[(2 * d_inner + 2 * gn) // tdt], n_out=2 * n_heads, tm=tm, tn=tdt,
                 name="ssd_in_dt")
    xbc_s = _conv_silu(xbc, conv_w, conv_b, batch=batch, lt=lt, n_lat=n_lat)
    dt_t = dt.reshape(batch, lt, 2, groups, hg).transpose(2, 3, 0, 4, 1)
    bias_t = dt_bias.astype(F32).reshape(2, groups, hg, 1)
    alog_t = a_log.astype(F32).reshape(2, groups, hg, 1)
    dsk = jnp.repeat(d_skip.astype(F32), SSD_HEAD_DIM).reshape(1, d_inner)
    kw = dict(batch=batch, lt=lt, n_lat=n_lat, d_inner=d_inner)
    y0 = _ssd_pass(0, xbc_s, dt_t, bias_t, alog_t, (dsk,), final=False, **kw)
    g = _ssd_pass(1, xbc_s, dt_t, bias_t, alog_t,
                  (y0, z.reshape(batch, lt, d_inner), norm_g.reshape(1, d_inner)), final=True, **kw)
    return g.reshape(rows, d_inner)


def _s5_scan_body(u_ref, bbd_ref, cbd_ref, a_ref, y_ref, xs_ref, s_ref):
    d = pl.program_id(0)
    t_len, n_b, width = u_ref.shape
    ns = a_ref.shape[-1] // 2

    @pl.when(pl.program_id(2) == 0)
    def _():
        s_ref[...] = jnp.zeros_like(s_ref)

    tb = 32
    for r in range(t_len // tb):
        u = u_ref[r * tb:(r + 1) * tb].reshape(tb * n_b, width).astype(BF16)
        xs_ref[r * tb * n_b:(r + 1) * tb * n_b, :] = jnp.dot(u, bbd_ref[...], preferred_element_type=F32)
    ar = a_ref[:, :ns]
    ai = a_ref[:, ns:]

    def step(k, carry):
        sr, si = carry
        t = jnp.where(d == 0, k, t_len - 1 - k)
        r0 = pl.multiple_of(t * n_b, n_b)
        xr = xs_ref[pl.ds(r0, n_b), :ns]
        xi = xs_ref[pl.ds(r0, n_b), ns:]
        nr = ar * sr - ai * si + xr
        ni = ar * si + ai * sr + xi
        xs_ref[pl.ds(r0, n_b), :ns] = nr
        xs_ref[pl.ds(r0, n_b), ns:] = ni
        return nr, ni

    sr, si = lax.fori_loop(0, t_len, step, (s_ref[:, :ns], s_ref[:, ns:]), unroll=4)
    s_ref[:, :ns] = sr
    s_ref[:, ns:] = si
    for r in range(t_len // tb):
        y = jnp.dot(xs_ref[r * tb * n_b:(r + 1) * tb * n_b, :].astype(BF16), cbd_ref[...], preferred_element_type=F32)
        y_ref[r * tb:(r + 1) * tb] = y.reshape(tb, n_b, width)


def _s5_merge_body(u_ref, y0_ref, y1_ref, dsk_ref, o_ref, g_ref):
    tt, n_b, tc = u_ref.shape
    y = dsk_ref[...] * u_ref[...] + y0_ref[...] + y1_ref[...]
    g = 0.5 * y * (1.0 + jnp.tanh(math.sqrt(2.0 / math.pi) * (y + 0.044715 * (y * y * y))))
    g = g.reshape(tt * n_b, tc)
    for cc in range(tc // LANES):
        g_ref[cc] = g[:, cc * LANES:(cc + 1) * LANES]
    for b in range(n_b):
        for cc in range(tc // LANES):
            o_ref[b, :, cc * LANES:(cc + 1) * LANES] = g_ref[cc, pl.ds(b, tt, stride=n_b), :].astype(o_ref.dtype)


def _s5_params(lam_re, lam_im, log_dt, b_re, b_im, c_re, c_im, n_b):
    n_groups, n_state, gsz = b_re.shape
    sg = S5_SLAB_GROUPS
    n_slab = n_groups // sg
    eye = jnp.eye(sg, dtype=F32)
    bbds, a_s = [], []
    br, bi = b_re.astype(F32), b_im.astype(F32)
    for d in range(2):
        lr = lam_re[d].astype(F32)
        li = lam_im[d].astype(F32)
        step = jnp.exp(log_dt[d].astype(F32))[:, None]
        mag = jnp.exp(lr * step)
        ar, ai = mag * jnp.cos(li * step), mag * jnp.sin(li * step)
        den = lr * lr + li * li
        fr = ((ar - 1.0) * lr + ai * li) / den
        fi = (ai * lr - (ar - 1.0) * li) / den
        bbr = fr[..., None] * br - fi[..., None] * bi
        bbi = fr[..., None] * bi + fi[..., None] * br

        def bdiag_in(m):
            m = m.reshape(n_slab, sg, n_state, gsz)
            return jnp.einsum("sgpc,gh->sgchp", m, eye).reshape(n_slab, sg * gsz, sg * n_state)

        bbds.append(jnp.concatenate([bdiag_in(bbr), bdiag_in(bbi)], -1))
        a_cat = jnp.concatenate([ar.reshape(n_slab, sg * n_state), ai.reshape(n_slab, sg * n_state)], -1)
        a_s.append(jnp.broadcast_to(a_cat[:, None, :], (n_slab, n_b, 2 * sg * n_state)))

    def bdiag_out(m):
        m = m.reshape(n_slab, sg, gsz, n_state)
        return jnp.einsum("sgcp,gh->sgphc", m, eye).reshape(n_slab, sg * n_state, sg * gsz)

    cbd = jnp.concatenate([bdiag_out(c_re.astype(F32)), -bdiag_out(c_im.astype(F32))], 1)
    return jnp.stack(bbds).astype(BF16), cbd.astype(BF16), jnp.stack(a_s)


def _mixer_s5(h, w_in, lam_re, lam_im, log_dt, b_re, b_im, c_re, c_im, d_skip, *, batch, lt, n_lat):
    rows, d = h.shape
    width = w_in.shape[1]
    tm = _pick(lt, (768, 512, 384, 256, 128))
    tn = _pick(width, (1024, 512, 256, 128))
    tpb = lt // tm
    ncb = width // tn
    u = _matmul(h, w_in, batch=batch, col_blocks=[0], n_out=width, tm=tm, tn=tn, out_shape=(lt, batch * width),
                out_map=lambda i, j, k: (i % tpb, (i // tpb) * ncb + j), name="s5_in")
    bbd, cbd, a_bc = _s5_params(lam_re, lam_im, log_dt, b_re, b_im, c_re, c_im, batch)
    n_slab, sw = bbd.shape[1], bbd.shape[2]
    ns2 = bbd.shape[3]
    t_len = S5_CHUNK
    lat_chunks, ctx_chunks = n_lat // t_len, (lt - n_lat) // t_len
    n_chunks = lat_chunks + ctx_chunks

    def cidx(dd, i):
        fwd = jnp.where(i < ctx_chunks, lat_chunks + i, i - ctx_chunks)
        return jnp.where(dd == 0, fwd, n_chunks - 1 - i)

    u3 = u.reshape(lt, batch, width)
    ys = pl.pallas_call(
        _s5_scan_body, grid=(2, n_slab, n_chunks),
        in_specs=[pl.BlockSpec((t_len, batch, sw), lambda dd, s, i: (cidx(dd, i), 0, s)),
                  pl.BlockSpec((None, None, sw, ns2), lambda dd, s, i: (dd, s, 0, 0)),
                  pl.BlockSpec((None, ns2, sw), lambda dd, s, i: (s, 0, 0)),
                  pl.BlockSpec((None, None, batch, ns2), lambda dd, s, i: (dd, s, 0, 0))],
        out_specs=pl.BlockSpec((None, t_len, batch, sw), lambda dd, s, i: (dd, cidx(dd, i), 0, s)),
        out_shape=jax.ShapeDtypeStruct((2, lt, batch, width), F32),
        scratch_shapes=[pltpu.VMEM((t_len * batch, ns2), F32), pltpu.VMEM((batch, ns2), F32)],
        compiler_params=_cparams(("parallel", "parallel", "arbitrary")), name="s5_scan")(u3, bbd, cbd, a_bc)
    tt = _pick(lt, (128,))
    tc = _pick(width, (1024, 512, 256, 128))
    g = pl.pallas_call(
        _s5_merge_body, grid=(lt // tt, width // tc),
        in_specs=[pl.BlockSpec((tt, batch, tc), lambda t, j: (t, 0, j)),
                  pl.BlockSpec((None, tt, batch, tc), lambda t, j: (0, t, 0, j)),
                  pl.BlockSpec((None, tt, batch, tc), lambda t, j: (1, t, 0, j)),
                  pl.BlockSpec((1, 1, tc), lambda t, j: (0, 0, j))],
        out_specs=pl.BlockSpec((batch, tt, tc), lambda t, j: (0, t, j)),
        out_shape=jax.ShapeDtypeStruct((batch, lt, width), BF16),
        scratch_shapes=[pltpu.VMEM((tc // LANES, tt * batch, LANES), F32)],
        compiler_params=_cparams(("parallel", "parallel")), name="s5_merge")(
            u3, ys, ys, d_skip.astype(F32).reshape(1, 1, width))
    return g.reshape(rows, width)


def _mla_norm_body(p_ref, qg_ref, kvg_ref, cq_ref, ckv_ref, kr_ref):
    def rms(t, g):
        return t * lax.rsqrt(jnp.mean(t * t, axis=-1, keepdims=True) + NORM_EPS) * g

    cq_ref[...] = rms(p_ref[:, :MLA_Q_RANK], qg_ref[...]).astype(cq_ref.dtype)
    ckv_ref[...] = rms(p_ref[:, MLA_Q_RANK:MLA_Q_RANK + MLA_KV_RANK], kvg_ref[...]).astype(ckv_ref.dtype)
    kr_ref[...] = p_ref[:, MLA_Q_RANK + MLA_KV_RANK:MLA_Q_RANK + MLA_KV_RANK + LANES]


def _mla_prep_body(q_ref, kv_ref, kr_ref, cos_ref, sin_ref, qg_ref, kg_ref, qo_ref, ko_ref, vo_ref, *, heads):
    dk = MLA_NOPE + MLA_ROPE
    hw = 2 * LANES
    scale = dk ** -0.5
    cos, sin = cos_ref[...], sin_ref[...]
    perm = _rope_perm(MLA_ROPE // 4)
    qg_n, qg_r = qg_ref[:, :LANES], qg_ref[:, LANES:]
    kg_n, kg_r = kg_ref[:, :LANES], kg_ref[:, LANES:]
    kr = kr_ref[...]
    kr_sq = kr * kr
    kr_rot = _rope(kr * kg_r, cos, sin, perm)
    for h in range(heads):
        c0 = h * hw
        qn = q_ref[:, c0:c0 + LANES]
        qr = q_ref[:, c0 + LANES:c0 + hw]
        rinv = lax.rsqrt(_rowsum_lanes(qn * qn + qr * qr) / dk + NORM_EPS)
        qo_ref[:, c0:c0 + LANES] = (qn * rinv * qg_n * scale).astype(BF16)
        qo_ref[:, c0 + LANES:c0 + hw] = (_rope(qr * rinv * qg_r, cos, sin, perm) * scale).astype(BF16)
        kn = kv_ref[:, c0:c0 + LANES]
        rinv = lax.rsqrt(_rowsum_lanes(kn * kn + kr_sq) / dk + NORM_EPS)
        ko_ref[:, c0:c0 + LANES] = (kn * rinv * kg_n).astype(BF16)
        ko_ref[:, c0 + LANES:c0 + hw] = (kr_rot * rinv).astype(BF16)
        vo_ref[:, h * LANES:(h + 1) * LANES] = kv_ref[:, c0 + LANES:c0 + hw].astype(BF16)


def _mla_attn_body(q_ref, k_ref, v_ref, o_ref, *, sub):
    k = k_ref[...]
    v = v_ref[...]
    for qi in range(q_ref.shape[0] // sub):
        sl = slice(qi * sub, (qi + 1) * sub)
        s = _nt_dot(q_ref[sl, :], k)
        m = jnp.max(s, axis=-1, keepdims=True)
        p = jnp.exp(s - m)
        denom = jnp.sum(p, axis=-1, keepdims=True)
        o_ref[sl, :] = (jnp.dot(p.astype(BF16), v, preferred_element_type=F32) / denom).astype(o_ref.dtype)


def _mixer_mla(h, w_in, q_a_g, kv_a_g, w_uq, w_ukv, q_g, k_g, *, batch, lt, n_lat):
    rows, d = h.shape
    heads = MLA_HEADS
    dk = MLA_NOPE + MLA_ROPE
    hw = 2 * LANES
    n_ctx = lt - n_lat
    n_in = MLA_Q_RANK + MLA_KV_RANK + MLA_ROPE
    n_in_pad = MLA_Q_RANK + MLA_KV_RANK + 2 * LANES
    w_in_p = jnp.pad(w_in, ((0, 0), (0, n_in_pad - n_in))).astype(BF16)
    w_uq_p = jnp.pad(w_uq.reshape(MLA_Q_RANK, heads, dk), ((0, 0), (0, 0), (0, hw - dk))).reshape(
        MLA_Q_RANK, heads * hw).astype(BF16)
    w_ukv_b = w_ukv.astype(BF16)
    qg_p = jnp.pad(q_g.astype(F32), (0, hw - dk)).reshape(1, hw)
    kg_p = jnp.pad(k_g.astype(F32), (0, hw - dk)).reshape(1, hw)
    tm = _pick(lt, (768, 512, 384, 256, 128))
    p = _matmul(h, w_in_p, batch=batch, col_blocks=[0], n_out=n_in_pad, tm=tm, tn=_pick(n_in_pad, (256, 128)),
                name="mla_in")
    tp = _pick(rows, (512, 256, 128))
    cq, ckv, kr = pl.pallas_call(
        _mla_norm_body, grid=(rows // tp,),
        in_specs=[pl.BlockSpec((tp, n_in_pad), lambda i: (i, 0)),
                  pl.BlockSpec((1, MLA_Q_RANK), lambda i: (0, 0)),
                  pl.BlockSpec((1, MLA_KV_RANK), lambda i: (0, 0))],
        out_specs=[pl.BlockSpec((tp, MLA_Q_RANK), lambda i: (i, 0)),
                   pl.BlockSpec((tp, MLA_KV_RANK), lambda i: (i, 0)),
                   pl.BlockSpec((tp, LANES), lambda i: (i, 0))],
        out_shape=[jax.ShapeDtypeStruct((rows, MLA_Q_RANK), BF16),
                   jax.ShapeDtypeStruct((rows, MLA_KV_RANK), BF16),
                   jax.ShapeDtypeStruct((rows, LANES), F32)],
        compiler_params=_cparams(("parallel",)), name="mla_norm")(
            p, q_a_g.reshape(1, MLA_Q_RANK), kv_a_g.reshape(1, MLA_KV_RANK))
    tn = _pick(heads * hw, (1024, 512, 256))
    q_full = _matmul(cq, w_uq_p, batch=batch, col_blocks=[0], n_out=heads * hw, tm=tm, tn=tn, name="mla_uq")
    kv = _matmul(ckv, w_ukv_b, batch=batch, col_blocks=[0], n_out=heads * hw, tm=tm, tn=tn, name="mla_ukv")
    cos, sin = _rope_tables(n_lat, n_ctx, MLA_ROPE, LANES)
    tr = _pick(math.gcd(lt, n_lat), (256, 128))
    trb = lt // tr
    hb = _pick(heads, (8, 4, 2, 1))
    q_cat, k_cat, v = pl.pallas_call(
        functools.partial(_mla_prep_body, heads=hb), grid=(rows // tr, heads // hb),
        in_specs=[pl.BlockSpec((tr, hb * hw), lambda i, j: (i, j)),
                  pl.BlockSpec((tr, hb * hw), lambda i, j: (i, j)),
                  pl.BlockSpec((tr, LANES), lambda i, j: (i, 0)),
                  pl.BlockSpec((tr, LANES), lambda i, j: (i % trb, 0)),
                  pl.BlockSpec((tr, LANES), lambda i, j: (i % trb, 0)),
                  pl.BlockSpec((1, hw), lambda i, j: (0, 0)),
                  pl.BlockSpec((1, hw), lambda i, j: (0, 0))],
        out_specs=[pl.BlockSpec((tr, hb * hw), lambda i, j: (i, j)),
                   pl.BlockSpec((tr, hb * hw), lambda i, j: (i, j)),
                   pl.BlockSpec((tr, hb * LANES), lambda i, j: (i, j))],
        out_shape=[jax.ShapeDtypeStruct((rows, heads * hw), BF16),
                   jax.ShapeDtypeStruct((rows, heads * hw), BF16),
                   jax.ShapeDtypeStruct((rows, heads * LANES), BF16)],
        compiler_params=_cparams(("parallel", "parallel")), name="mla_prep")(q_full, kv, kr, cos, sin, qg_p, kg_p)
    tq = _pick(n_lat, (1024, 512, 256, 128))
    sub = min(tq, 256)
    q3 = q_cat.reshape(batch, lt, heads * hw)
    k3 = k_cat.reshape(batch, lt, heads * hw)
    v3 = v.reshape(batch, lt, heads * LANES)
    o = pl.pallas_call(
        functools.partial(_mla_attn_body, sub=sub), grid=(batch, heads, n_lat // tq),
        in_specs=[pl.BlockSpec((None, tq, hw), lambda b, hh, i: (b, i, hh)),
                  pl.BlockSpec((None, lt, hw), lambda b, hh, i: (b, 0, hh)),
                  pl.BlockSpec((None, lt, LANES), lambda b, hh, i: (b, 0, hh))],
        out_specs=pl.BlockSpec((None, tq, LANES), lambda b, hh, i: (b, i, hh)),
        out_shape=jax.ShapeDtypeStruct((batch, n_lat, heads * LANES), BF16),
        compiler_params=_cparams(("parallel", "parallel", "arbitrary")), name="mla_attn")(q3, k3, v3)
    return o.reshape(batch * n_lat, heads * LANES)


def kernel(x, c, ctx, c_ctx, mod_down, mod_up, mod_b, norm1_g, norm2_g, swa_w_in, swa_q_g, swa_k_g, swa_sinks, swa_w_out, ssd_w_in, ssd_conv_w, ssd_conv_b, ssd_dt_bias, ssd_a_log, ssd_d, ssd_norm_g, ssd_w_out, s5_w_in, s5_lam_re, s5_lam_im, s5_log_dt, s5_b_re, s5_b_im, s5_c_re, s5_c_im, s5_d, s5_w_glu, mla_w_in, mla_q_a_g, mla_kv_a_g, mla_w_uq, mla_w_ukv, mla_q_g, mla_k_g, mla_w_out, moe_w_group, moe_b_group, moe_w_expert, moe_b_expert, moe_w1, moe_w3, moe_w2):
    batch, n_lat, d = x.shape
    n_ctx = ctx.shape[1]
    lt = n_lat + n_ctx
    rows = batch * lt
    depth = mod_down.shape[0]
    dims = dict(batch=batch, lt=lt, n_lat=n_lat)
    tm = _pick(lt, (768, 512, 384, 256, 128))
    tn = _pick(d, (512, 256, 128))

    xs = jnp.concatenate([x, ctx], axis=1).reshape(rows, d)

    pad_rows = -(batch + 1) % SUBLANES
    cvecs = jnp.concatenate([c, c_ctx[None], jnp.zeros((pad_rows, d), F32)], axis=0)
    mod_all = _adaln(cvecs, mod_down, mod_up, mod_b).reshape(depth, batch + 1 + pad_rows, 6, d)

    n_moe_logits = MOE_GROUPS + moe_w_expert.shape[-1]
    router_w = jnp.pad(jnp.concatenate([moe_w_group, moe_w_expert], -1), ((0, 0), (0, 0), (0, LANES - n_moe_logits)))
    router_b = jnp.pad(jnp.concatenate([moe_b_group, moe_b_expert], -1), ((0, 0), (0, LANES - n_moe_logits)))

    def out_proj(o, w, res, mods4, name, last, epilogue="residual", cols=(0,)):
        tk = _pick(o.shape[1], (8192, 4096, 2048, 1024, 512, 256, 128))
        use_rows = n_lat if last else lt
        tm_o = _pick(n_lat, (1024, 512, 256, 128)) if last else tm
        return _matmul(o, w.astype(BF16), batch=batch, col_blocks=list(cols), n_out=d, tm=tm_o, tn=tn, tk=tk,
                       epilogue=epilogue, res=res, mods=mods4, gate_idx=2, use_rows=use_rows, n_lat=n_lat, name=name)

    for i in range(depth):
        kind, slot = i % 4, i // 4
        last = i == depth - 1
        ml = mod_all[i, :batch]
        mc = jnp.broadcast_to(mod_all[i, batch][None], (batch, 6, d))
        mods4 = jnp.stack([ml, mc], axis=1)
        mods2 = mods4.reshape(batch * 2, 6, d)
        h = _modnorm(xs, norm1_g[i], mods2, shift_idx=0, scale_idx=1, lt=lt, n_lat=n_lat, out_dtype=BF16)
        if kind == 0:
            o = _mixer_swa(h, swa_w_in[slot].astype(BF16), swa_q_g[slot], swa_k_g[slot], swa_sinks[slot], **dims)
            xs = out_proj(o, swa_w_out[slot], xs, mods4, "swa_out", last)
        elif kind == 1:
            o = _mixer_ssd(h, ssd_w_in[slot].astype(BF16), ssd_conv_w[slot], ssd_conv_b[slot], ssd_dt_bias[slot],
                           ssd_a_log[slot], ssd_d[slot], ssd_norm_g[slot], **dims)
            xs = out_proj(o, ssd_w_out[slot], xs, mods4, "ssd_out", last)
        elif kind == 2:
            o = _mixer_s5(h, s5_w_in[slot].astype(BF16), s5_lam_re[slot], s5_lam_im[slot], s5_log_dt[slot],
                          s5_b_re[slot], s5_b_im[slot], s5_c_re[slot], s5_c_im[slot], s5_d[slot], **dims)
            xs = out_proj(o, s5_w_glu[slot], xs, mods4, "s5_glu", last, epilogue="glu_residual", cols=(0, d // tn))
        else:
            if not last:
                raise NotImplementedError("the MLA mixer is implemented for the last layer only (latent queries)")
            o = _mixer_mla(h, mla_w_in[slot], mla_q_a_g[slot], mla_kv_a_g[slot], mla_w_uq[slot], mla_w_ukv[slot],
                           mla_q_g[slot], mla_k_g[slot], **dims)
            xs = out_proj(o, mla_w_out[slot], xs, mods4, "mla_out", last)
        xs = _moe_layer(xs, mods2, norm2_g[i], router_w[i], router_b[i].reshape(1, LANES), moe_w1, moe_w3, moe_w2, i,
                        lt=n_lat if last else lt, n_lat=n_lat)
    return xs.reshape(batch, n_lat, d)
```
